```python
import math
import jax, jax.numpy as jnp
from jax import lax
import numpy as np

D_MODEL = 1024
BATCH = 2
SEQ = 8192
DEPTH = 1

N_META = 16
D_MIX = 2 * D_MODEL
D_SSD = D_MIX // 2
SSD_HEAD_DIM = 64
SSD_HEADS = D_SSD // SSD_HEAD_DIM
SSD_GROUPS = 2
SSD_HEADS_PER_GROUP = SSD_HEADS // SSD_GROUPS
SSD_STATE = 128
SSD_CONV = 4
SSD_CHUNK = 256
D_XBC = D_SSD + 2 * SSD_GROUPS * SSD_STATE
D_S5 = D_MIX - D_SSD
S5_GROUP_WIDTH = 16
S5_GROUPS = D_S5 // S5_GROUP_WIDTH
S5_STATE = 64
D_FF = 4 * D_MODEL
D_IN_PROJ = D_SSD + D_XBC + SSD_HEADS + D_S5
NORM_EPS = 1e-5
DT_MIN = 0.001
DT_MAX = 0.1

kernel_name = 'hymba_ssd_s5_hybrid_block'


def _rmsnorm(x, g):
    xf = x.astype(jnp.float32)
    y = xf * lax.rsqrt(jnp.mean(xf * xf, axis=-1, keepdims=True) + NORM_EPS)
    return (y * g.astype(jnp.float32)).astype(x.dtype)


def _causal_depthwise_conv(x, w, b):
    k, c = w.shape
    y = lax.conv_general_dilated(x, w[:, None, :].astype(x.dtype), window_strides=(1,), padding=[(k - 1, 0)], dimension_numbers=('NWC', 'WIO', 'NWC'), feature_group_count=c)
    return y + b.astype(x.dtype)


def _ssd_mixer(xbc, dt_raw, z, dt_bias, a_log, d_skip, g_norm):
    bsz, length, _ = xbc.shape
    f32 = jnp.float32
    xbc = xbc.astype(f32)
    x_in = xbc[..., :D_SSD]
    b_in = xbc[..., D_SSD:D_SSD + SSD_GROUPS * SSD_STATE]
    c_in = xbc[..., D_SSD + SSD_GROUPS * SSD_STATE:]
    dt = jax.nn.softplus(dt_raw.astype(f32) + dt_bias.astype(f32))
    front = SSD_CHUNK - N_META
    n_real_chunks = -(-(length - N_META) // SSD_CHUNK)
    total = SSD_CHUNK * (1 + n_real_chunks)
    back = total - front - length
    pad = lambda t: jnp.pad(t, ((0, 0), (front, back), (0, 0)))
    n_chunks = total // SSD_CHUNK
    shp = (bsz, n_chunks, SSD_CHUNK, SSD_GROUPS)
    xc = pad(x_in).reshape(shp + (SSD_HEADS_PER_GROUP, SSD_HEAD_DIM))
    bc = pad(b_in).reshape(shp + (SSD_STATE,))
    cc = pad(c_in).reshape(shp + (SSD_STATE,))
    dtc = pad(dt).reshape(shp + (SSD_HEADS_PER_GROUP,))
    a = -jnp.exp(a_log.astype(f32)).reshape(SSD_GROUPS, SSD_HEADS_PER_GROUP)
    a_cs = jnp.cumsum(dtc * a, axis=2)
    xdt = xc * dtc[..., None]
    causal = jnp.tril(jnp.ones((SSD_CHUNK, SSD_CHUNK), dtype=bool))[:, :, None, None]
    seg = a_cs[:, :, :, None] - a_cs[:, :, None, :]
    decay = jnp.exp(jnp.where(causal, seg, -jnp.inf))
    cb = jnp.einsum('bclgn,bcsgn->bclsg', cc, bc)
    y_diag = jnp.einsum('bclsg,bclsgr,bcsgrp->bclgrp', cb, decay, xdt)
    decay_to_end = jnp.exp(a_cs[:, :, -1:] - a_cs)
    chunk_states = jnp.einsum('bclgn,bclgr,bclgrp->bcgrpn', bc, decay_to_end, xdt)
    chunk_decay = jnp.exp(a_cs[:, :, -1])

    def step(state, inp):
        dec, st = inp
        return state * dec[..., None, None] + st, state

    init = jnp.zeros((bsz, SSD_GROUPS, SSD_HEADS_PER_GROUP, SSD_HEAD_DIM, SSD_STATE), f32)
    _, prev = lax.scan(step, init, (jnp.moveaxis(chunk_decay, 1, 0), jnp.moveaxis(chunk_states, 1, 0)))
    prev = jnp.moveaxis(prev, 0, 1)
    y_off = jnp.einsum('bclgn,bcgrpn,bclgr->bclgrp', cc, prev, jnp.exp(a_cs))
    d = d_skip.astype(f32).reshape(SSD_GROUPS, SSD_HEADS_PER_GROUP, 1)
    y = (y_diag + y_off + xc * d).reshape(bsz, total, D_SSD)[:, front:front + length]
    y = y * jax.nn.silu(z.astype(f32))
    return _rmsnorm(y, g_norm)


def _s5_mixer(u, lam_re, lam_im, log_step, b_re, b_im, c_re, c_im, d_skip, w_glu, b_glu, g_norm):
    bsz, length, _ = u.shape
    f32 = jnp.float32
    u = u.astype(f32).reshape(bsz, length, S5_GROUPS, S5_GROUP_WIDTH)
    lr = lam_re.astype(f32)
    li = lam_im.astype(f32)
    step = jnp.exp(log_step.astype(f32))[:, None]
    mag = jnp.exp(lr * step)
    ab_re = mag * jnp.cos(li * step)
    ab_im = mag * jnp.sin(li * step)
    den = lr * lr + li * li
    coef_re = ((ab_re - 1.0) * lr + ab_im * li) / den
    coef_im = (ab_im * lr - (ab_re - 1.0) * li) / den
    br = b_re.astype(f32)
    bi = b_im.astype(f32)
    bb_re = coef_re[..., None] * br - coef_im[..., None] * bi
    bb_im = coef_re[..., None] * bi + coef_im[..., None] * br
    bu_re = jnp.einsum('blgh,gph->blgp', u, bb_re)
    bu_im = jnp.einsum('blgh,gph->blgp', u, bb_im)
    a_re = jnp.broadcast_to(ab_re, (1, length) + ab_re.shape)
    a_im = jnp.broadcast_to(ab_im, (1, length) + ab_im.shape)

    def combine(e_i, e_j):
        ar_i, ai_i, br_i, bi_i = e_i
        ar_j, ai_j, br_j, bi_j = e_j
        return (ar_j * ar_i - ai_j * ai_i,
                ar_j * ai_i + ai_j * ar_i,
                ar_j * br_i - ai_j * bi_i + br_j,
                ar_j * bi_i + ai_j * br_i + bi_j)

    _, _, s_re, s_im = lax.associative_scan(combine, (a_re, a_im, bu_re, bu_im), axis=1)
    y = (jnp.einsum('blgp,ghp->blgh', s_re, c_re.astype(f32))
         - jnp.einsum('blgp,ghp->blgh', s_im, c_im.astype(f32))
         + u * d_skip.astype(f32))
    y = jax.nn.gelu(y.reshape(bsz, length, D_S5), approximate=False)
    v = y @ w_glu.astype(f32) + b_glu.astype(f32)
    y = v[..., :D_S5] * jax.nn.sigmoid(v[..., D_S5:])
    return _rmsnorm(y, g_norm)


def setup_inputs(seed: int = 0) -> dict:
    key = jax.random.key(seed)
    ks = jax.random.split(key, 32)
    f32 = jnp.float32
    nrm = lambda k, s, sc: jax.random.normal(k, s, f32) * sc
    gain = lambda k, s: 1.0 + 0.01 * jax.random.normal(k, s, f32)
    dt = jnp.exp(jax.random.uniform(ks[6], (DEPTH, SSD_HEADS), f32) * (math.log(DT_MAX) - math.log(DT_MIN)) + math.log(DT_MIN))
    dt = jnp.maximum(dt, 1e-4)
    dt_bias = dt + jnp.log(-jnp.expm1(-dt))
    n_idx = jnp.arange(S5_STATE, dtype=f32)
    return {
        'x': nrm(ks[0], (BATCH, SEQ, D_MODEL), 1.0),
        'meta_tokens': nrm(ks[1], (N_META, D_MODEL), 1.0),
        'g_mix': gain(ks[2], (DEPTH, D_MODEL)),
        'w_in': nrm(ks[3], (DEPTH, D_MODEL, D_IN_PROJ), D_MODEL ** -0.5),
        'conv_w': nrm(ks[4], (DEPTH, SSD_CONV, D_XBC), SSD_CONV ** -0.5),
        'conv_b': nrm(ks[5], (DEPTH, D_XBC), 0.01),
        'dt_bias': dt_bias,
        'a_log': jnp.log(jax.random.uniform(ks[7], (DEPTH, SSD_HEADS), f32, 1.0, 16.0)),
        'd_ssd': gain(ks[8], (DEPTH, SSD_HEADS)),
        'g_ssd': gain(ks[9], (DEPTH, D_SSD)),
        'lam_re': -0.5 + nrm(ks[10], (DEPTH, S5_GROUPS, S5_STATE), 0.01),
        'lam_im': math.pi * n_idx + nrm(ks[11], (DEPTH, S5_GROUPS, S5_STATE), 0.01),
        'log_step': jax.random.uniform(ks[12], (DEPTH, S5_GROUPS), f32, math.log(DT_MIN), math.log(DT_MAX)),
        'b_re': nrm(ks[13], (DEPTH, S5_GROUPS, S5_STATE, S5_GROUP_WIDTH), (2 * S5_GROUP_WIDTH) ** -0.5),
        'b_im': nrm(ks[14], (DEPTH, S5_GROUPS, S5_STATE, S5_GROUP_WIDTH), (2 * S5_GROUP_WIDTH) ** -0.5),
        'c_re': nrm(ks[15], (DEPTH, S5_GROUPS, S5_GROUP_WIDTH, S5_STATE), S5_STATE ** -0.5),
        'c_im': nrm(ks[16], (DEPTH, S5_GROUPS, S5_GROUP_WIDTH, S5_STATE), S5_STATE ** -0.5),
        'd_s5': nrm(ks[17], (DEPTH, S5_GROUPS, S5_GROUP_WIDTH), 1.0),
        'w_glu': nrm(ks[18], (DEPTH, D_S5, 2 * D_S5), D_S5 ** -0.5),
        'b_glu': nrm(ks[19], (DEPTH, 2 * D_S5), 0.01),
        'g_s5': gain(ks[20], (DEPTH, D_S5)),
        'w_out': nrm(ks[21], (DEPTH, D_MIX, D_MODEL), D_MIX ** -0.5),
        'g_mlp': gain(ks[22], (DEPTH, D_MODEL)),
        'w_up': nrm(ks[23], (DEPTH, D_MODEL, D_FF), D_MODEL ** -0.5),
        'w_down': nrm(ks[24], (DEPTH, D_FF, D_MODEL), D_FF ** -0.5),
        'g_final': gain(ks[25], (D_MODEL,)),
    }


def reference(x, meta_tokens, g_mix, w_in, conv_w, conv_b, dt_bias, a_log, d_ssd, g_ssd, lam_re, lam_im, log_step, b_re, b_im, c_re, c_im, d_s5, w_glu, b_glu, g_s5, w_out, g_mlp, w_up, w_down, g_final):
    bsz = x.shape[0]
    meta = jnp.broadcast_to(meta_tokens.astype(x.dtype)[None], (bsz, N_META, D_MODEL))
    h = jnp.concatenate([meta, x], axis=1)
    o_xbc = D_SSD
    o_dt = D_SSD + D_XBC
    o_u = o_dt + SSD_HEADS
    for layer in range(DEPTH):
        n = _rmsnorm(h, g_mix[layer])
        proj = n @ w_in[layer]
        z = proj[..., :o_xbc]
        xbc = jax.nn.silu(_causal_depthwise_conv(proj[..., o_xbc:o_dt], conv_w[layer], conv_b[layer]))
        dt_raw = proj[..., o_dt:o_u]
        u = proj[..., o_u:]
        y_ssd = _ssd_mixer(xbc, dt_raw, z, dt_bias[layer], a_log[layer], d_ssd[layer], g_ssd[layer])
        y_s5 = _s5_mixer(u, lam_re[layer], lam_im[layer], log_step[layer], b_re[layer], b_im[layer], c_re[layer], c_im[layer], d_s5[layer], w_glu[layer], b_glu[layer], g_s5[layer])
        mix = jnp.concatenate([y_ssd.astype(h.dtype), y_s5.astype(h.dtype)], axis=-1)
        h = h + mix @ w_out[layer]
        m = _rmsnorm(h, g_mlp[layer]) @ w_up[layer]
        h = h + jnp.square(jax.nn.relu(m)) @ w_down[layer]
    return _rmsnorm(h, g_final)[:, N_META:].astype(x.dtype)
```

```python
import functools

import jax
import jax.numpy as jnp
from jax import lax
from jax.experimental import pallas as pl
from jax.experimental.pallas import tpu as pltpu

F32 = jnp.float32
BF16 = jnp.bfloat16

NORM_EPS = 1e-5
N_META = 16
SSD_CHUNK = 256
SSD_HEAD_DIM = 64
SSD_GROUPS = 2
SSD_STATE = 128
SSD_CONV = 4
S5_WIDTH = 16
S5_STATE = 64
S5_BLOCK = 16
S5_FLAT = S5_BLOCK * S5_WIDTH
LANES = 128
SUBLANES = 8
VMEM_LIMIT_BYTES = 56 * 1024 * 1024


def _rmsnorm(x, g):
    return x * lax.rsqrt(jnp.mean(x * x, axis=-1, keepdims=True) + NORM_EPS) * g


def _gelu(x):
    return 0.5 * x * (1.0 + lax.erf(x * (2.0 ** -0.5)))


def _dot(a, b):
    return jnp.dot(a, b, preferred_element_type=F32)


def _dot_split(a, m01, parts):
    acc = None
    rem = a
    for _ in range(parts):
        piece = rem.astype(BF16)
        rem = rem - piece.astype(F32)
        term = _dot(piece, m01)
        acc = term if acc is None else acc + term
    return acc


def _params(sem):
    return pltpu.CompilerParams(dimension_semantics=sem, vmem_limit_bytes=VMEM_LIMIT_BYTES)


def _const_spec(shape):
    nd = len(shape)
    return pl.BlockSpec(shape, lambda *_: (0,) * nd)


def _in_proj_kernel(x_ref, g_ref, wz_ref, wxbc_ref, wdt_ref, wu_ref, z_ref, xbc_ref, dt_ref, u_ref):
    n = _rmsnorm(x_ref[...], g_ref[...]).astype(BF16)
    z_ref[...] = _dot(n, wz_ref[...])
    xbc_ref[...] = _dot(n, wxbc_ref[...])
    dt_ref[...] = _dot(n, wdt_ref[...])
    u_ref[...] = _dot(n, wu_ref[...])


def _in_proj(x2d, g, wz, wxbc, wdt, wu, tm):
    rows, d = x2d.shape
    widths = (wz.shape[1], wxbc.shape[1], wdt.shape[1], wu.shape[1])
    row_spec = lambda w: pl.BlockSpec((tm, w), lambda i: (i, 0))
    return pl.pallas_call(
        _in_proj_kernel,
        grid=(rows // tm,),
        in_specs=[row_spec(d), _const_spec(g.shape), _const_spec(wz.shape), _const_spec(wxbc.shape),
                  _const_spec(wdt.shape), _const_spec(wu.shape)],
        out_specs=[row_spec(w) for w in widths],
        out_shape=[jax.ShapeDtypeStruct((rows, w), F32) for w in widths],
        compiler_params=_params(("parallel",)),
        name="in_proj",
    )(x2d, g, wz, wxbc, wdt, wu)


def _ssd_kernel(xbc_r_ref, xbc_m_ref, z_ref, dt_r_ref, dt_m_ref, convw_ref, convb_ref, dtb_ref, alog_ref,
                dskip_ref, g_ref, y_ref, xs_ref, st_ref, yd_ref):
    q = SSD_CHUNK
    d_ssd = z_ref.shape[-1]
    n_heads = d_ssd // SSD_HEAD_DIM
    heads_per_group = n_heads // SSD_GROUPS
    gw = heads_per_group * SSD_HEAD_DIM
    is_meta = pl.program_id(1) == 0

    @pl.when(is_meta)
    def _():
        xs_ref[0:SUBLANES, :] = jnp.zeros((SUBLANES, xs_ref.shape[1]), F32)
        st_ref[...] = jnp.zeros(st_ref.shape, F32)

    xbc_pre = jnp.where(is_meta, xbc_m_ref[...], xbc_r_ref[0])
    dt_raw = jnp.where(is_meta, dt_m_ref[...], dt_r_ref[0])

    xs_ref[SUBLANES:SUBLANES + q, :] = xbc_pre
    acc = convb_ref[...] + convw_ref[0:1, :] * xs_ref[SUBLANES - 3:SUBLANES - 3 + q, :]
    for k in range(1, SSD_CONV):
        off = SUBLANES - (SSD_CONV - 1) + k
        acc = acc + convw_ref[k:k + 1, :] * xs_ref[off:off + q, :]
    xs_ref[0:SUBLANES, :] = xbc_pre[q - SUBLANES:q, :]
    xbc = acc * jax.nn.sigmoid(acc)
    x_in = xbc[:, :d_ssd]
    b_in = xbc[:, d_ssd:d_ssd + SSD_GROUPS * SSD_STATE].astype(BF16)
    c_in = xbc[:, d_ssd + SSD_GROUPS * SSD_STATE:].astype(BF16)

    dt = jax.nn.softplus(dt_raw + dtb_ref[...])
    row = lax.broadcasted_iota(jnp.int32, dt.shape, 0)
    dt = jnp.where(jnp.logical_and(is_meta, row < q - N_META), 0.0, dt)
    d_a = dt * (-jnp.exp(alog_ref[...]))
    r_i = lax.broadcasted_iota(jnp.int32, (q, q), 0)
    c_i = lax.broadcasted_iota(jnp.int32, (q, q), 1)
    causal = r_i >= c_i
    a_cs_t = _dot_split(d_a.T, jnp.where(c_i >= r_i, 1.0, 0.0).astype(BF16), 3)
    a_cs = a_cs_t.T

    e_r = lax.broadcasted_iota(jnp.int32, (LANES, d_ssd), 0)
    e_c = lax.broadcasted_iota(jnp.int32, (LANES, d_ssd), 1)
    expand = jnp.where(e_c // SSD_HEAD_DIM == e_r, 1.0, 0.0).astype(BF16)
    dt_x = _dot_split(dt, expand, 2)
    a_x = _dot_split(a_cs, expand, 3)
    a_last = a_x[q - 1:q, :]
    xdt = x_in * dt_x
    xdt_b = xdt.astype(BF16)
    xw_b = (xdt * jnp.exp(a_last - a_x)).astype(BF16)
    ea_x = jnp.exp(a_x)

    for g in range(SSD_GROUPS):
        cg = c_in[:, g * SSD_STATE:(g + 1) * SSD_STATE]
        bg = b_in[:, g * SSD_STATE:(g + 1) * SSD_STATE]
        cb = lax.dot_general(cg, bg, (((1,), (1,)), ((), ())), preferred_element_type=F32)
        for r in range(heads_per_group):
            h = g * heads_per_group + r
            seg = a_cs[:, h:h + 1] - a_cs_t[h:h + 1, :]
            att = (cb * jnp.exp(jnp.where(causal, seg, -jnp.inf))).astype(BF16)
            lo = h * SSD_HEAD_DIM
            yd_ref[:, lo:lo + SSD_HEAD_DIM] = _dot(att, xdt_b[:, lo:lo + SSD_HEAD_DIM])
        s_prev = st_ref[g]
        y_off = _dot(cg, s_prev.astype(BF16)) * ea_x[:, g * gw:(g + 1) * gw]
        yd_ref[:, g * gw:(g + 1) * gw] += y_off
        s_new = lax.dot_general(bg, xw_b[:, g * gw:(g + 1) * gw], (((0,), (0,)), ((), ())),
                                preferred_element_type=F32)
        st_ref[g] = s_prev * jnp.exp(a_last[:, g * gw:(g + 1) * gw]) + s_new

    y = yd_ref[...] + x_in * dskip_ref[...]
    zz = z_ref[0]
    y = y * (zz * jax.nn.sigmoid(zz))
    y_ref[0] = _rmsnorm(y, g_ref[...]).astype(y_ref.dtype)


def _ssd(xbc_r, xbc_m, z_r, dt_r, dt_m, convw, convb, dtb, alog, dskip, g):
    bsz, seq, d_xbc = xbc_r.shape
    d_ssd = z_r.shape[-1]
    q = SSD_CHUNK
    n_chunks = seq // q
    real = lambda w: pl.BlockSpec((1, q, w), lambda b, c: (b, jnp.maximum(c - 1, 0), 0))
    gw = d_ssd // SSD_GROUPS
    return pl.pallas_call(
        _ssd_kernel,
        grid=(bsz, n_chunks + 1),
        in_specs=[real(d_xbc), _const_spec(xbc_m.shape), real(d_ssd), real(LANES), _const_spec(dt_m.shape),
                  _const_spec(convw.shape), _const_spec(convb.shape), _const_spec(dtb.shape),
                  _const_spec(alog.shape), _const_spec(dskip.shape), _const_spec(g.shape)],
        out_specs=real(d_ssd),
        out_shape=jax.ShapeDtypeStruct((bsz, seq, d_ssd), BF16),
        scratch_shapes=[pltpu.VMEM((SUBLANES + q, d_xbc), F32),
                        pltpu.VMEM((SSD_GROUPS, SSD_STATE, gw), F32),
                        pltpu.VMEM((q, d_ssd), F32)],
        compiler_params=_params(("arbitrary", "arbitrary")),
        name="ssd",
    )(xbc_r, xbc_m, z_r, dt_r, dt_m, convw, convb, dtb, alog, dskip, g)


def _lam_pow(lr, li, step, k):
    kf = k.astype(F32)
    mag = jnp.exp(kf * (lr * step))
    ang = kf * (li * step)
    return mag * jnp.cos(ang), mag * jnp.sin(ang)


def _s5_prep_kernel(lr_c_ref, li_c_ref, lr_r_ref, li_r_ref, ls_ref, bt_re_ref, bt_im_ref, ct_re_ref, ct_im_ref,
                    mt_ref, w_re_ref, w_im_ref, e_re_ref, e_im_ref, dp_re_ref, dp_im_ref):
    t, hw, p = S5_BLOCK, S5_WIDTH, S5_STATE
    step = jnp.exp(ls_ref[0])
    lr_c, li_c = lr_c_ref[0], li_c_ref[0]
    lr_r, li_r = lr_r_ref[0], li_r_ref[0]

    one = jnp.ones((1, 1), jnp.int32)
    ab_re, ab_im = _lam_pow(lr_r, li_r, step, one)
    den = lr_r * lr_r + li_r * li_r
    coef_re = ((ab_re - 1.0) * lr_r + ab_im * li_r) / den
    coef_im = (ab_im * lr_r - (ab_re - 1.0) * li_r) / den
    bt_re, bt_im = bt_re_ref[0], bt_im_ref[0]
    bb_re = coef_re * bt_re - coef_im * bt_im
    bb_im = coef_re * bt_im + coef_im * bt_re

    ct_re, ct_im = ct_re_ref[0], ct_im_ref[0]
    k_lane = lax.broadcasted_iota(jnp.int32, (1, t * hw), 1) // hw
    l0_re, l0_im = _lam_pow(lr_c, li_c, step, k_lane)
    x0_re = l0_re * ct_re - l0_im * ct_im
    x0_im = l0_re * ct_im + l0_im * ct_re
    hi = lax.Precision.HIGHEST
    kt = (jnp.dot(bb_re, x0_re, precision=hi, preferred_element_type=F32)
          - jnp.dot(bb_im, x0_im, precision=hi, preferred_element_type=F32))
    lane = lax.broadcasted_iota(jnp.int32, kt.shape, 1)
    for s in range(t):
        blk = kt if s == 0 else jnp.where(lane >= s * hw, pltpu.roll(kt, s * hw, axis=1), 0.0)
        mt_ref[0, s * hw:(s + 1) * hw, :] = blk.astype(mt_ref.dtype)

    l1_re, l1_im = _lam_pow(lr_c, li_c, step, k_lane + 1)
    e_re_ref[0] = (l1_re * ct_re - l1_im * ct_im).astype(e_re_ref.dtype)
    e_im_ref[0] = (-(l1_re * ct_im + l1_im * ct_re)).astype(e_im_ref.dtype)

    k_row = (t - 1) - lax.broadcasted_iota(jnp.int32, (t * hw, 1), 0) // hw
    lw_re, lw_im = _lam_pow(lr_r, li_r, step, k_row)
    bbt_re = jnp.concatenate([bb_re] * t, axis=0)
    bbt_im = jnp.concatenate([bb_im] * t, axis=0)
    w_re_ref[0] = (lw_re * bbt_re - lw_im * bbt_im).astype(w_re_ref.dtype)
    w_im_ref[0] = (lw_re * bbt_im + lw_im * bbt_re).astype(w_im_ref.dtype)

    k_blk = t * lax.broadcasted_iota(jnp.int32, (2 * SUBLANES, 1), 0)
    dp_re, dp_im = _lam_pow(lr_r, li_r, step, k_blk)
    dp_re_ref[0] = dp_re
    dp_im_ref[0] = dp_im


def _s5_prep(lr, li, ls, bt_re, bt_im, ct_re, ct_im):
    g, p = lr.shape
    fl = S5_FLAT
    spec = lambda *s: pl.BlockSpec((1,) + s, lambda i: (i, 0, 0))
    out = lambda *s, dt=BF16: jax.ShapeDtypeStruct((g,) + s, dt)
    return pl.pallas_call(
        _s5_prep_kernel,
        grid=(g,),
        in_specs=[spec(p, 1), spec(p, 1), spec(1, p), spec(1, p), spec(1, 1), spec(S5_WIDTH, p), spec(S5_WIDTH, p),
                  spec(p, fl), spec(p, fl)],
        out_specs=[spec(fl, fl), spec(fl, p), spec(fl, p), spec(p, fl), spec(p, fl),
                   spec(2 * SUBLANES, p), spec(2 * SUBLANES, p)],
        out_shape=[out(fl, fl), out(fl, p), out(fl, p), out(p, fl), out(p, fl),
                   out(2 * SUBLANES, p, dt=F32), out(2 * SUBLANES, p, dt=F32)],
        compiler_params=_params(("parallel",)),
        name="s5_prep",
    )(lr[:, :, None], li[:, :, None], lr[:, None, :], li[:, None, :], ls[:, None, None], bt_re, bt_im, ct_re, ct_im)


def _s5_core_kernel(f_ref, mt_ref, w_re_ref, w_im_ref, e_re_ref, e_im_ref, dp_re_ref, dp_im_ref, dskip_ref,
                    y_ref, wu_re_ref, wu_im_ref, s_re_ref, s_im_ref, *, n_batch):
    rows = f_ref.shape[1]
    per_batch = rows // n_batch
    n_tiles = per_batch // SUBLANES
    p = S5_STATE
    f = f_ref[0]
    fb = f.astype(BF16)
    wu_re_ref[...] = _dot(fb, w_re_ref[0])
    wu_im_ref[...] = _dot(fb, w_im_ref[0])

    dp_re, dp_im = dp_re_ref[0], dp_im_ref[0]
    sub = lax.broadcasted_iota(jnp.int32, (SUBLANES, p), 0)

    def cmul_add(a_re, a_im, d_re, d_im, x_re, x_im):
        return a_re + d_re * x_re - d_im * x_im, a_im + d_re * x_im + d_im * x_re

    def shift_down(x, k):
        return jnp.where(sub >= k, pltpu.roll(x, k, axis=0), 0.0)

    def tile_step(i, carry):
        new = []
        for b in range(n_batch):
            c_re, c_im = carry[2 * b], carry[2 * b + 1]
            base = pl.multiple_of(b * per_batch + i * SUBLANES, SUBLANES)
            v_re = wu_re_ref[pl.ds(base, SUBLANES), :]
            v_im = wu_im_ref[pl.ds(base, SUBLANES), :]
            for lvl in range(3):
                k = 1 << lvl
                v_re, v_im = cmul_add(v_re, v_im, dp_re[k:k + 1, :], dp_im[k:k + 1, :],
                                      shift_down(v_re, k), shift_down(v_im, k))
            e_re, e_im = cmul_add(shift_down(v_re, 1), shift_down(v_im, 1), dp_re[0:SUBLANES, :],
                                  dp_im[0:SUBLANES, :], c_re, c_im)
            s_re_ref[pl.ds(base, SUBLANES), :] = e_re
            s_im_ref[pl.ds(base, SUBLANES), :] = e_im
            n_re, n_im = cmul_add(v_re[SUBLANES - 1:, :], v_im[SUBLANES - 1:, :], dp_re[SUBLANES:SUBLANES + 1, :],
                                  dp_im[SUBLANES:SUBLANES + 1, :], c_re, c_im)
            new += [n_re, n_im]
        return tuple(new)

    zero = jnp.zeros((1, p), F32)
    lax.fori_loop(0, n_tiles, tile_step, (zero,) * (2 * n_batch))

    y = (_dot(fb, mt_ref[0]) + _dot(s_re_ref[...].astype(BF16), e_re_ref[0])
         + _dot(s_im_ref[...].astype(BF16), e_im_ref[0]) + f * dskip_ref[0])
    y_ref[0] = _gelu(y).astype(y_ref.dtype)


def _s5_core(f, mt, w_re, w_im, e_re, e_im, dp_re, dp_im, dskip, n_batch):
    g, rows, fl = f.shape
    p = S5_STATE
    spec = lambda *s: pl.BlockSpec((1,) + s, lambda i: (i, 0, 0))
    return pl.pallas_call(
        functools.partial(_s5_core_kernel, n_batch=n_batch),
        grid=(g,),
        in_specs=[spec(rows, fl), spec(fl, fl), spec(fl, p), spec(fl, p), spec(p, fl), spec(p, fl),
                  spec(2 * SUBLANES, p), spec(2 * SUBLANES, p), spec(1, fl)],
        out_specs=spec(rows, fl),
        out_shape=jax.ShapeDtypeStruct((g, rows, fl), BF16),
        scratch_shapes=[pltpu.VMEM((rows, p), F32)] * 4,
        compiler_params=_params(("parallel",)),
        name="s5_core",
    )(f, mt, w_re, w_im, e_re, e_im, dp_re, dp_im, dskip)


def _s5_glu_kernel(y_ref, w_ref, b_ref, g_ref, o_ref):
    d = o_ref.shape[-1]
    v = _dot(y_ref[...], w_ref[...]) + b_ref[...]
    o = v[:, :d] * jax.nn.sigmoid(v[:, d:])
    o_ref[...] = _rmsnorm(o, g_ref[...]).astype(o_ref.dtype)


def _s5_glu(y2d, w, b, g, tm):
    rows, d = y2d.shape
    return pl.pallas_call(
        _s5_glu_kernel,
        grid=(rows // tm,),
        in_specs=[pl.BlockSpec((tm, d), lambda i: (i, 0)), _const_spec(w.shape), _const_spec(b.shape),
                  _const_spec(g.shape)],
        out_specs=pl.BlockSpec((tm, d), lambda i: (i, 0)),
        out_shape=jax.ShapeDtypeStruct((rows, d), BF16),
        compiler_params=_params(("parallel",)),
        name="s5_glu",
    )(y2d, w, b, g)


def _out_mlp_kernel(x_ref, ya_ref, yb_ref, woa_ref, wob_ref, gm_ref, wup_ref, wdn_ref, gf_ref, o_ref, *, ff_tile):
    h = x_ref[...] + _dot(ya_ref[...], woa_ref[...]) + _dot(yb_ref[...], wob_ref[...])
    n = _rmsnorm(h, gm_ref[...]).astype(BF16)
    d_ff = wup_ref.shape[1]
    mlp = None
    for k in range(d_ff // ff_tile):
        m = _dot(n, wup_ref[:, k * ff_tile:(k + 1) * ff_tile])
        m = jnp.square(jnp.maximum(m, 0.0)).astype(BF16)
        part = _dot(m, wdn_ref[k * ff_tile:(k + 1) * ff_tile, :])
        mlp = part if mlp is None else mlp + part
    o_ref[...] = _rmsnorm(h + mlp, gf_ref[...])


def _out_mlp(x2d, ya, yb, woa, wob, gm, wup, wdn, gf, tm, ff_tile):
    rows, d = x2d.shape
    row_spec = pl.BlockSpec((tm, d), lambda i: (i, 0))
    return pl.pallas_call(
        functools.partial(_out_mlp_kernel, ff_tile=ff_tile),
        grid=(rows // tm,),
        in_specs=[row_spec, row_spec, row_spec, _const_spec(woa.shape), _const_spec(wob.shape), _const_spec(gm.shape),
                  _const_spec(wup.shape), _const_spec(wdn.shape), _const_spec(gf.shape)],
        out_specs=row_spec,
        out_shape=jax.ShapeDtypeStruct((rows, d), F32),
        compiler_params=_params(("parallel",)),
        name="out_mlp",
    )(x2d, ya, yb, woa, wob, gm, wup, wdn, gf)


def _row_tile(rows, want):
    tm = min(rows, want)
    assert rows % tm == 0
    return tm


def kernel(x, meta_tokens, g_mix, w_in, conv_w, conv_b, dt_bias, a_log, d_ssd, g_ssd, lam_re, lam_im, log_step,
           b_re, b_im, c_re, c_im, d_s5, w_glu, b_glu, g_s5, w_out, g_mlp, w_up, w_down, g_final):
    bsz, seq, d_model = x.shape
    assert w_in.shape[0] == 1, "single-layer block: meta-token outputs are only consumed as state"
    assert seq % SSD_CHUNK == 0
    n_heads = dt_bias.shape[-1]
    d_ssd_w = n_heads * SSD_HEAD_DIM
    d_xbc = conv_w.shape[-1]
    d_s5_w = d_s5.shape[-2] * d_s5.shape[-1]
    n_grp = d_s5.shape[-2]
    o_dt = d_ssd_w + d_xbc
    o_u = o_dt + n_heads
    row = lambda v: v.reshape(1, -1).astype(F32)

    wi = w_in[0]
    wz = wi[:, :d_ssd_w].astype(BF16)
    wxbc = wi[:, d_ssd_w:o_dt].astype(BF16)
    wdt = jnp.pad(wi[:, o_dt:o_u], ((0, 0), (0, LANES - n_heads))).astype(BF16)
    wu = wi[:, o_u:].astype(BF16)
    pad_heads = lambda v: jnp.pad(row(v), ((0, 0), (0, LANES - n_heads)))

    x2d = x.reshape(bsz * seq, d_model)
    meta_chunk = jnp.concatenate([jnp.zeros((SSD_CHUNK - N_META, d_model), x.dtype), meta_tokens.astype(x.dtype)], axis=0)
    gmix = row(g_mix[0])
    tm = _row_tile(bsz * seq, 512)
    z_r, xbc_r, dt_r, u_r = _in_proj(x2d, gmix, wz, wxbc, wdt, wu, tm)
    _, xbc_m, dt_m, u_m = _in_proj(meta_chunk, gmix, wz, wxbc, wdt, wu, SSD_CHUNK)

    y_ssd = _ssd(xbc_r.reshape(bsz, seq, d_xbc), xbc_m, z_r.reshape(bsz, seq, d_ssd_w), dt_r.reshape(bsz, seq, LANES),
                 dt_m, conv_w[0].astype(F32), row(conv_b[0]), pad_heads(dt_bias[0]), pad_heads(a_log[0]),
                 row(jnp.repeat(d_ssd[0], SSD_HEAD_DIM)), row(g_ssd[0]))

    t = S5_BLOCK
    bt = lambda v: jnp.swapaxes(v[0], 1, 2).astype(F32)
    ct = lambda v: jnp.tile(jnp.swapaxes(v[0], 1, 2).astype(F32), (1, 1, t))
    mt, w_re, w_im, e_re, e_im, dp_re, dp_im = _s5_prep(lam_re[0].astype(F32), lam_im[0].astype(F32),
                                                        log_step[0].astype(F32), bt(b_re), bt(b_im), ct(c_re), ct(c_im))
    u_all = jnp.concatenate([jnp.broadcast_to(u_m[None], (bsz,) + u_m.shape), u_r.reshape(bsz, seq, d_s5_w)], axis=1)
    blocks = (seq + SSD_CHUNK) // t
    f = u_all.reshape(bsz * blocks, t, n_grp, S5_WIDTH).transpose(2, 0, 1, 3).reshape(n_grp, bsz * blocks, S5_FLAT)
    dsk = jnp.tile(d_s5[0].astype(F32), (1, t)).reshape(n_grp, 1, S5_FLAT)
    yg = _s5_core(f, mt, w_re, w_im, e_re, e_im, dp_re, dp_im, dsk, bsz)
    yg = yg.reshape(n_grp, bsz, blocks, t, S5_WIDTH)[:, :, SSD_CHUNK // t:]
    y_tok = yg.transpose(1, 2, 3, 0, 4).reshape(bsz * seq, d_s5_w)
    y_s5 = _s5_glu(y_tok, w_glu[0].astype(BF16), row(b_glu[0]), row(g_s5[0]), tm)

    wo = w_out[0].astype(BF16)
    out = _out_mlp(x2d, y_ssd.reshape(bsz * seq, d_ssd_w), y_s5, wo[:d_ssd_w], wo[d_ssd_w:], row(g_mlp[0]),
                   w_up[0].astype(BF16), w_down[0].astype(BF16), row(g_final), tm, 1024)
    return out.reshape(bsz, seq, d_model).astype(x.dtype)
```

```python
import functools

import jax
import jax.numpy as jnp
from jax import lax
from jax.experimental import pallas as pl
from jax.experimental.pallas import tpu as pltpu

F32 = jnp.float32
BF16 = jnp.bfloat16

NORM_EPS = 1e-5
N_META = 16
SSD_CHUNK = 256
SSD_HEAD_DIM = 64
SSD_GROUPS = 2
SSD_STATE = 128
SSD_CONV = 4
S5_WIDTH = 16
S5_STATE = 64
S5_BLOCK = 16
S5_FLAT = S5_BLOCK * S5_WIDTH
S5_PREP_GROUPS = 8
LANES = 128
SUBLANES = 8
LANE_BLOCKS = LANES // S5_WIDTH
VMEM_LIMIT_BYTES = 56 * 1024 * 1024


def _rmsnorm(x, g):
    return x * lax.rsqrt(jnp.mean(x * x, axis=-1, keepdims=True) + NORM_EPS) * g


def _silu(x):
    return x * jax.nn.sigmoid(x)


def _gelu(x):
    return 0.5 * x * (1.0 + lax.erf(x * (2.0 ** -0.5)))


def _dot(a, b):
    return jnp.dot(a, b, preferred_element_type=F32)


def _dot_nt(a, b, **kw):
    return lax.dot_general(a, b, (((1,), (1,)), ((), ())), preferred_element_type=F32, **kw)


def _dot_tn(a, b):
    return lax.dot_general(a, b, (((0,), (0,)), ((), ())), preferred_element_type=F32)


def _dot_split(a, m01, parts):
    acc = None
    rem = a
    for _ in range(parts):
        piece = rem.astype(BF16)
        rem = rem - piece.astype(F32)
        term = _dot(piece, m01)
        acc = term if acc is None else acc + term
    return acc


def _params(sem):
    return pltpu.CompilerParams(dimension_semantics=sem, vmem_limit_bytes=VMEM_LIMIT_BYTES)


def _const_spec(shape):
    nd = len(shape)
    return pl.BlockSpec(shape, lambda *_: (0,) * nd)


def _transpose_lane_blocks(planes, lane_blk):
    out = []
    for j in range(LANE_BLOCKS):
        acc = None
        for i in range(LANE_BLOCKS):
            shift = ((i - j) % LANE_BLOCKS) * S5_WIDTH
            v = planes[i] if shift == 0 else pltpu.roll(planes[i], shift, axis=1)
            acc = v if acc is None else jnp.where(lane_blk == i, v, acc)
        out.append(acc)
    return out


def _in_proj_kernel(x_ref, hist_ref, g_ref, wz_ref, wxbc_ref, wdt_ref, wu_ref, convw_ref, convb_ref,
                    z_ref, xa_ref, bc_ref, dt_ref, f_ref, tail_ref, hist_scr, u_scr, *, tiles_per_seq):
    tm = x_ref.shape[0]
    d_x = xa_ref.shape[1]
    n = _rmsnorm(x_ref[...], g_ref[...]).astype(BF16)
    z_ref[...] = _dot(n, wz_ref[...])
    dt_ref[...] = _dot(n, wdt_ref[...])

    @pl.when(lax.rem(pl.program_id(0), tiles_per_seq) == 0)
    def _():
        hist_scr[...] = hist_ref[...]

    pre = _dot(n, wxbc_ref[...])
    w = [convw_ref[k:k + 1, :] for k in range(SSD_CONV)]

    def conv(v):
        acc = convb_ref[...] + w[SSD_CONV - 1] * v
        for k in range(1, SSD_CONV):
            acc = acc + w[SSD_CONV - 1 - k] * pltpu.roll(v, k, axis=0)
        return _silu(acc)

    act = conv(pre)
    head = conv(jnp.concatenate([hist_scr[...], pre[:SUBLANES]], axis=0))[SUBLANES:]
    hist_scr[...] = pre[tm - SUBLANES:]
    tail_ref[...] = pre[tm - SUBLANES:]
    row = lax.broadcasted_iota(jnp.int32, (2 * SUBLANES, 1), 0)
    first = jnp.where(row < SUBLANES, jnp.concatenate([head, head], axis=0), act[:2 * SUBLANES])
    xa_ref[...] = act[:, :d_x]
    xa_ref[0:2 * SUBLANES, :] = first[:, :d_x]
    bc_ref[...] = act[:, d_x:].astype(BF16)
    bc_ref[0:2 * SUBLANES, :] = first[:, d_x:].astype(BF16)

    u = _dot(n, wu_ref[...])
    nb = tm // S5_BLOCK
    lane_blk = lax.broadcasted_iota(jnp.int32, (nb, LANES), 1) // S5_WIDTH
    for oct_ in range(u_scr.shape[0]):
        u_scr[oct_] = u[:, oct_ * LANES:(oct_ + 1) * LANES]
        for half in range(S5_BLOCK // LANE_BLOCKS):
            tok = [u_scr[oct_, pl.ds(half * LANE_BLOCKS + s, nb, stride=S5_BLOCK), :] for s in range(LANE_BLOCKS)]
            grp = _transpose_lane_blocks(tok, lane_blk)
            for gi in range(LANE_BLOCKS):
                f_ref[oct_ * LANE_BLOCKS + gi, :, half * LANES:(half + 1) * LANES] = grp[gi].astype(BF16)


def _in_proj(x2d, hist, g, wz, wxbc, wdt, wu, convw, convb, tm, tiles_per_seq):
    rows, d = x2d.shape
    d_z, d_xbc, d_u = wz.shape[1], wxbc.shape[1], wu.shape[1]
    d_bc = d_xbc - d_z
    n_grp = d_u // S5_WIDTH
    nb = tm // S5_BLOCK
    row_spec = lambda w: pl.BlockSpec((tm, w), lambda i: (i, 0))
    return pl.pallas_call(
        functools.partial(_in_proj_kernel, tiles_per_seq=tiles_per_seq),
        grid=(rows // tm,),
        in_specs=[row_spec(d)] + [_const_spec(a.shape) for a in (hist, g, wz, wxbc, wdt, wu, convw, convb)],
        out_specs=[row_spec(d_z), row_spec(d_z), row_spec(d_bc), row_spec(LANES),
                   pl.BlockSpec((n_grp, nb, S5_FLAT), lambda i: (0, i, 0)), _const_spec((SUBLANES, d_xbc))],
        out_shape=[jax.ShapeDtypeStruct((rows, d_z), F32), jax.ShapeDtypeStruct((rows, d_z), F32),
                   jax.ShapeDtypeStruct((rows, d_bc), BF16), jax.ShapeDtypeStruct((rows, LANES), F32),
                   jax.ShapeDtypeStruct((n_grp, rows // S5_BLOCK, S5_FLAT), BF16),
                   jax.ShapeDtypeStruct((SUBLANES, d_xbc), F32)],
        scratch_shapes=[pltpu.VMEM((SUBLANES, d_xbc), F32), pltpu.VMEM((d_u // LANES, tm, LANES), F32)],
        compiler_params=_params(("arbitrary",)),
        name="in_proj",
    )(x2d, hist, g, wz, wxbc, wdt, wu, convw, convb)


def _ssd_kernel(xa_r_ref, xa_m_ref, bc_r_ref, bc_m_ref, z_ref, dt_r_ref, dt_m_ref, dtb_ref, alog_ref,
                dskip_ref, g_ref, expand_ref, y_ref, st_ref, yd_ref):
    q = SSD_CHUNK
    d_ssd = z_ref.shape[-1]
    n_heads = d_ssd // SSD_HEAD_DIM
    heads_per_group = n_heads // SSD_GROUPS
    gw = heads_per_group * SSD_HEAD_DIM
    is_meta = pl.program_id(1) == 0

    @pl.when(is_meta)
    def _():
        st_ref[...] = jnp.zeros(st_ref.shape, F32)

    x_in = jnp.where(is_meta, xa_m_ref[...], xa_r_ref[0])
    bc = jnp.where(is_meta, bc_m_ref[...], bc_r_ref[0])
    dt_raw = jnp.where(is_meta, dt_m_ref[...], dt_r_ref[0])
    b_in = bc[:, :SSD_GROUPS * SSD_STATE]
    c_in = bc[:, SSD_GROUPS * SSD_STATE:]

    dt = jax.nn.softplus(dt_raw + dtb_ref[...])
    row = lax.broadcasted_iota(jnp.int32, dt.shape, 0)
    dt = jnp.where(jnp.logical_and(is_meta, row < q - N_META), 0.0, dt)
    d_a = dt * (-jnp.exp(alog_ref[...]))
    r_i = lax.broadcasted_iota(jnp.int32, (q, q), 0)
    c_i = lax.broadcasted_iota(jnp.int32, (q, q), 1)
    causal = r_i >= c_i
    a_cs_t = _dot_split(d_a.T, jnp.where(c_i >= r_i, 1.0, 0.0).astype(BF16), 3)
    a_cs = a_cs_t.T

    expand = expand_ref[...]
    dt_x = _dot_split(dt, expand, 2)
    a_x = _dot_split(a_cs, expand, 3)
    a_last = a_x[q - 1:q, :]
    xdt = x_in * dt_x
    xdt_b = xdt.astype(BF16)
    xw_b = (xdt * jnp.exp(a_last - a_x)).astype(BF16)
    ea_x = jnp.exp(a_x)

    for g in range(SSD_GROUPS):
        cg = c_in[:, g * SSD_STATE:(g + 1) * SSD_STATE]
        bg = b_in[:, g * SSD_STATE:(g + 1) * SSD_STATE]
        cb = _dot_nt(cg, bg)
        for r in range(heads_per_group):
            h = g * heads_per_group + r
            seg = a_cs[:, h:h + 1] - a_cs_t[h:h + 1, :]
            att = (cb * jnp.exp(jnp.where(causal, seg, -jnp.inf))).astype(BF16)
            lo = h * SSD_HEAD_DIM
            yd_ref[:, lo:lo + SSD_HEAD_DIM] = _dot(att, xdt_b[:, lo:lo + SSD_HEAD_DIM])
        s_prev = st_ref[g]
        y_off = _dot(cg, s_prev.astype(BF16)) * ea_x[:, g * gw:(g + 1) * gw]
        yd_ref[:, g * gw:(g + 1) * gw] += y_off
        s_new = _dot_tn(bg, xw_b[:, g * gw:(g + 1) * gw])
        st_ref[g] = s_prev * jnp.exp(a_last[:, g * gw:(g + 1) * gw]) + s_new

    y = yd_ref[...] + x_in * dskip_ref[...]
    y = y * _silu(z_ref[0])
    y_ref[0] = _rmsnorm(y, g_ref[...]).astype(y_ref.dtype)


def _ssd(xa_r, xa_m, bc_r, bc_m, z_r, dt_r, dt_m, dtb, alog, dskip, g, expand):
    bsz, seq, d_ssd = xa_r.shape
    d_bc = bc_r.shape[-1]
    q = SSD_CHUNK
    n_chunks = seq // q
    real = lambda w: pl.BlockSpec((1, q, w), lambda b, c: (b, jnp.maximum(c - 1, 0), 0))
    gw = d_ssd // SSD_GROUPS
    return pl.pallas_call(
        _ssd_kernel,
        grid=(bsz, n_chunks + 1),
        in_specs=[real(d_ssd), _const_spec(xa_m.shape), real(d_bc), _const_spec(bc_m.shape), real(d_ssd),
                  real(LANES), _const_spec(dt_m.shape), _const_spec(dtb.shape), _const_spec(alog.shape),
                  _const_spec(dskip.shape), _const_spec(g.shape), _const_spec(expand.shape)],
        out_specs=real(d_ssd),
        out_shape=jax.ShapeDtypeStruct((bsz, seq, d_ssd), BF16),
        scratch_shapes=[pltpu.VMEM((SSD_GROUPS, SSD_STATE, gw), F32), pltpu.VMEM((q, d_ssd), F32)],
        compiler_params=_params(("arbitrary", "arbitrary")),
        name="ssd",
    )(xa_r, xa_m, bc_r, bc_m, z_r, dt_r, dt_m, dtb, alog, dskip, g, expand)


def _lam_pow(lr, li, step, k):
    kf = k.astype(F32)
    mag = jnp.exp(kf * (lr * step))
    ang = kf * (li * step)
    return mag * jnp.cos(ang), mag * jnp.sin(ang)


def _s5_prep_kernel(lr_ref, li_ref, ls_ref, bt_re_ref, bt_im_ref, c_re_ref, c_im_ref,
                    mt_ref, wc_ref, ec_ref, dp_re_ref, dp_im_ref):
    t, hw, p = S5_BLOCK, S5_WIDTH, S5_STATE
    lr, li = lr_ref[0], li_ref[0]
    step = jnp.exp(ls_ref[0])
    k_up = lax.broadcasted_iota(jnp.int32, (3 * SUBLANES, 1), 0)
    pw_re, pw_im = _lam_pow(lr, li, step, k_up)
    k_dn = (t - 1) - lax.broadcasted_iota(jnp.int32, (t, 1), 0)
    pr_re, pr_im = _lam_pow(lr, li, step, k_dn)
    k_blk = t * lax.broadcasted_iota(jnp.int32, (2 * SUBLANES, 1), 0)
    dp_re, dp_im = _lam_pow(lr, li, step, k_blk)

    ab_re, ab_im = pw_re[1:2, :], pw_im[1:2, :]
    den = lr * lr + li * li
    coef_re = ((ab_re - 1.0) * lr + ab_im * li) / den
    coef_im = (ab_im * lr - (ab_re - 1.0) * li) / den
    bt_re, bt_im = bt_re_ref[0], bt_im_ref[0]
    bb_re = coef_re * bt_re - coef_im * bt_im
    bb_im = coef_re * bt_im + coef_im * bt_re
    c_re, c_im = c_re_ref[0], c_im_ref[0]

    def rep(m):
        return jnp.broadcast_to(m[:, None, :], (t, hw, p)).reshape(t * hw, p)

    def tile(m):
        return jnp.concatenate([m] * t, axis=0)

    hi = lax.Precision.HIGHEST
    lane = lax.broadcasted_iota(jnp.int32, (hw, t * hw), 1)
    for a in range(S5_PREP_GROUPS):
        sl = slice(a * p, (a + 1) * p)
        ct_re, ct_im = tile(c_re[:, sl]), tile(c_im[:, sl])
        l0_re, l0_im = rep(pw_re[0:t, sl]), rep(pw_im[0:t, sl])
        l1_re, l1_im = rep(pw_re[1:t + 1, sl]), rep(pw_im[1:t + 1, sl])
        x0_re = l0_re * ct_re - l0_im * ct_im
        x0_im = l0_re * ct_im + l0_im * ct_re
        kt = _dot_nt(bb_re[:, sl], x0_re, precision=hi) - _dot_nt(bb_im[:, sl], x0_im, precision=hi)
        for s in range(t):
            blk = kt if s == 0 else jnp.where(lane >= s * hw, pltpu.roll(kt, s * hw, axis=1), 0.0)
            mt_ref[a, s * hw:(s + 1) * hw, :] = blk.astype(mt_ref.dtype)
        x1_re = l1_re * ct_re - l1_im * ct_im
        x1_im = l1_re * ct_im + l1_im * ct_re
        ec_ref[a] = jnp.concatenate([x1_re, -x1_im], axis=1).T.astype(ec_ref.dtype)
        lw_re, lw_im = rep(pr_re[:, sl]), rep(pr_im[:, sl])
        bt_re_a, bt_im_a = tile(bb_re[:, sl]), tile(bb_im[:, sl])
        wc_ref[a] = jnp.concatenate([lw_re * bt_re_a - lw_im * bt_im_a, lw_re * bt_im_a + lw_im * bt_re_a],
                                    axis=1).astype(wc_ref.dtype)
        dp_re_ref[a] = jnp.concatenate([dp_re[:, sl]] * 2, axis=1)
        dp_im_ref[a] = jnp.concatenate([dp_im[:, sl]] * 2, axis=1)


def _s5_prep(lr, li, ls, bt_re, bt_im, c_re, c_im):
    steps = lr.shape[0]
    gs, p, fl = S5_PREP_GROUPS, S5_STATE, S5_FLAT
    n_grp = steps * gs
    in_spec = lambda a: pl.BlockSpec((1,) + a.shape[1:], lambda i: (i, 0, 0))
    out_spec = lambda *s: pl.BlockSpec((gs,) + s, lambda i: (i, 0, 0))
    out = lambda *s, dt=BF16: jax.ShapeDtypeStruct((n_grp,) + s, dt)
    args = (lr, li, ls, bt_re, bt_im, c_re, c_im)
    return pl.pallas_call(
        _s5_prep_kernel,
        grid=(steps,),
        in_specs=[in_spec(a) for a in args],
        out_specs=[out_spec(fl, fl), out_spec(fl, 2 * p), out_spec(2 * p, fl), out_spec(2 * SUBLANES, 2 * p),
                   out_spec(2 * SUBLANES, 2 * p)],
        out_shape=[out(fl, fl), out(fl, 2 * p), out(2 * p, fl), out(2 * SUBLANES, 2 * p, dt=F32),
                   out(2 * SUBLANES, 2 * p, dt=F32)],
        compiler_params=_params(("parallel",)),
        name="s5_prep",
    )(*args)


def _s5_core_kernel(f_r_ref, f_m_ref, mt_ref, wc_ref, ec_ref, dp_re_ref, dp_im_ref, dskip_ref,
                    y_ref, wu_re_ref, wu_im_ref, s_re_ref, s_im_ref, *, n_batch):
    p = S5_STATE
    rows_m = f_m_ref.shape[1]
    rows_r = f_r_ref.shape[1] // n_batch
    per_batch = rows_m + rows_r
    n_tiles = per_batch // SUBLANES
    lane_lo = lax.broadcasted_iota(jnp.int32, (1, 2 * p), 1) < p

    def pack(a, b):
        return jnp.where(lane_lo, a, pltpu.roll(b, p, axis=1)), jnp.where(lane_lo, pltpu.roll(a, p, axis=1), b)

    wu_m = [_dot(f_m_ref[a], wc_ref[a]) for a in range(2)]
    wu_r = [_dot(f_r_ref[a], wc_ref[a]) for a in range(2)]
    m_re, m_im = pack(*wu_m)
    r_re, r_im = pack(*wu_r)
    for b in range(n_batch):
        wu_re_ref[b * per_batch:b * per_batch + rows_m, :] = m_re
        wu_im_ref[b * per_batch:b * per_batch + rows_m, :] = m_im
        wu_re_ref[b * per_batch + rows_m:(b + 1) * per_batch, :] = r_re[b * rows_r:(b + 1) * rows_r]
        wu_im_ref[b * per_batch + rows_m:(b + 1) * per_batch, :] = r_im[b * rows_r:(b + 1) * rows_r]

    dp_re = jnp.where(lane_lo, dp_re_ref[0], dp_re_ref[1])
    dp_im = jnp.where(lane_lo, dp_im_ref[0], dp_im_ref[1])
    sub = lax.broadcasted_iota(jnp.int32, (SUBLANES, 2 * p), 0)

    def cmul_add(a_re, a_im, d_re, d_im, x_re, x_im):
        return a_re + d_re * x_re - d_im * x_im, a_im + d_re * x_im + d_im * x_re

    def shift_down(x, k):
        return jnp.where(sub >= k, pltpu.roll(x, k, axis=0), 0.0)

    def tile_step(i, carry):
        new = []
        for b in range(n_batch):
            c_re, c_im = carry[2 * b], carry[2 * b + 1]
            base = pl.multiple_of(b * per_batch + i * SUBLANES, SUBLANES)
            v_re = wu_re_ref[pl.ds(base, SUBLANES), :]
            v_im = wu_im_ref[pl.ds(base, SUBLANES), :]
            for lvl in range(3):
                k = 1 << lvl
                v_re, v_im = cmul_add(v_re, v_im, dp_re[k:k + 1, :], dp_im[k:k + 1, :],
                                      shift_down(v_re, k), shift_down(v_im, k))
            e_re, e_im = cmul_add(shift_down(v_re, 1), shift_down(v_im, 1), dp_re[0:SUBLANES, :],
                                  dp_im[0:SUBLANES, :], c_re, c_im)
            s_re_ref[pl.ds(base, SUBLANES), :] = e_re
            s_im_ref[pl.ds(base, SUBLANES), :] = e_im
            n_re, n_im = cmul_add(v_re[SUBLANES - 1:, :], v_im[SUBLANES - 1:, :], dp_re[SUBLANES:SUBLANES + 1, :],
                                  dp_im[SUBLANES:SUBLANES + 1, :], c_re, c_im)
            new += [n_re, n_im]
        return tuple(new)

    zero = jnp.zeros((1, 2 * p), F32)
    lax.fori_loop(0, n_tiles, tile_step, (zero,) * (2 * n_batch))

    for b in range(n_batch):
        lo = b * per_batch + rows_m
        s_a, s_b = pack(s_re_ref[lo:lo + rows_r, :], s_im_ref[lo:lo + rows_r, :])
        for a, s_ab in enumerate((s_a, s_b)):
            f = f_r_ref[a, b * rows_r:(b + 1) * rows_r, :]
            y = _dot(f, mt_ref[a]) + _dot(s_ab.astype(BF16), ec_ref[a]) + f.astype(F32) * dskip_ref[a]
            y_ref[a, b * rows_r:(b + 1) * rows_r, :] = _gelu(y).astype(y_ref.dtype)


def _s5_core(f_r, f_m, mt, wc, ec, dp_re, dp_im, dskip, n_batch):
    n_grp, rows_r, fl = f_r.shape
    rows_m = f_m.shape[1]
    p2 = 2 * S5_STATE
    rows = n_batch * rows_m + rows_r
    spec = lambda *s: pl.BlockSpec((2,) + s, lambda i: (i, 0, 0))
    return pl.pallas_call(
        functools.partial(_s5_core_kernel, n_batch=n_batch),
        grid=(n_grp // 2,),
        in_specs=[spec(rows_r, fl), spec(rows_m, fl), spec(fl, fl), spec(fl, p2), spec(p2, fl),
                  spec(2 * SUBLANES, p2), spec(2 * SUBLANES, p2), spec(1, fl)],
        out_specs=spec(rows_r, fl),
        out_shape=jax.ShapeDtypeStruct((n_grp, rows_r, fl), BF16),
        scratch_shapes=[pltpu.VMEM((rows, p2), F32)] * 4,
        compiler_params=_params(("parallel",)),
        name="s5_core",
    )(f_r, f_m, mt, wc, ec, dp_re, dp_im, dskip)


def _s5_glu_kernel(yg_ref, w_ref, b_ref, g_ref, o_ref, y_scr):
    d = o_ref.shape[-1]
    nb = yg_ref.shape[1]
    lane_blk = lax.broadcasted_iota(jnp.int32, (nb, LANES), 1) // S5_WIDTH
    for oct_ in range(d // LANES):
        for half in range(S5_BLOCK // LANE_BLOCKS):
            grp = [yg_ref[oct_ * LANE_BLOCKS + gi, :, half * LANES:(half + 1) * LANES].astype(F32)
                   for gi in range(LANE_BLOCKS)]
            tok = _transpose_lane_blocks(grp, lane_blk)
            for s in range(LANE_BLOCKS):
                y_scr[oct_, pl.ds(half * LANE_BLOCKS + s, nb, stride=S5_BLOCK), :] = tok[s]
    y = jnp.concatenate([y_scr[oct_] for oct_ in range(d // LANES)], axis=1).astype(BF16)
    v = _dot(y, w_ref[...]) + b_ref[...]
    o = v[:, :d] * jax.nn.sigmoid(v[:, d:])
    o_ref[...] = _rmsnorm(o, g_ref[...]).astype(o_ref.dtype)


def _s5_glu(yg, w, b, g, tm):
    n_grp, blocks, fl = yg.shape
    rows, d = blocks * S5_BLOCK, n_grp * S5_WIDTH
    nb = tm // S5_BLOCK
    return pl.pallas_call(
        _s5_glu_kernel,
        grid=(rows // tm,),
        in_specs=[pl.BlockSpec((n_grp, nb, fl), lambda i: (0, i, 0)), _const_spec(w.shape), _const_spec(b.shape),
                  _const_spec(g.shape)],
        out_specs=pl.BlockSpec((tm, d), lambda i: (i, 0)),
        out_shape=jax.ShapeDtypeStruct((rows, d), BF16),
        scratch_shapes=[pltpu.VMEM((d // LANES, tm, LANES), F32)],
        compiler_params=_params(("parallel",)),
        name="s5_glu",
    )(yg, w, b, g)


def _out_mlp_kernel(x_ref, ya_ref, yb_ref, woa_ref, wob_ref, gm_ref, wup_ref, wdn_ref, gf_ref, o_ref, *, ff_tile):
    h = x_ref[...] + _dot(ya_ref[...], woa_ref[...]) + _dot(yb_ref[...], wob_ref[...])
    n = _rmsnorm(h, gm_ref[...]).astype(BF16)
    d_ff = wup_ref.shape[1]
    mlp = None
    for k in range(d_ff // ff_tile):
        m = _dot(n, wup_ref[:, k * ff_tile:(k + 1) * ff_tile])
        m = jnp.square(jnp.maximum(m, 0.0)).astype(BF16)
        part = _dot(m, wdn_ref[k * ff_tile:(k + 1) * ff_tile, :])
        mlp = part if mlp is None else mlp + part
    o_ref[...] = _rmsnorm(h + mlp, gf_ref[...])


def _out_mlp(x2d, ya, yb, woa, wob, gm, wup, wdn, gf, tm, ff_tile):
    rows, d = x2d.shape
    row_spec = pl.BlockSpec((tm, d), lambda i: (i, 0))
    return pl.pallas_call(
        functools.partial(_out_mlp_kernel, ff_tile=ff_tile),
        grid=(rows // tm,),
        in_specs=[row_spec, row_spec, row_spec, _const_spec(woa.shape), _const_spec(wob.shape), _const_spec(gm.shape),
                  _const_spec(wup.shape), _const_spec(wdn.shape), _const_spec(gf.shape)],
        out_specs=row_spec,
        out_shape=jax.ShapeDtypeStruct((rows, d), F32),
        compiler_params=_params(("parallel",)),
        name="out_mlp",
    )(x2d, ya, yb, woa, wob, gm, wup, wdn, gf)


def _row_tile(rows, want):
    tm = min(rows, want)
    assert rows % tm == 0
    return tm


def kernel(x, meta_tokens, g_mix, w_in, conv_w, conv_b, dt_bias, a_log, d_ssd, g_ssd, lam_re, lam_im, log_step,
           b_re, b_im, c_re, c_im, d_s5, w_glu, b_glu, g_s5, w_out, g_mlp, w_up, w_down, g_final):
    bsz, seq, d_model = x.shape
    assert w_in.shape[0] == 1, "single-layer block: meta-token outputs are only consumed as state"
    assert seq % SSD_CHUNK == 0
    n_heads = dt_bias.shape[-1]
    d_ssd_w = n_heads * SSD_HEAD_DIM
    d_xbc = conv_w.shape[-1]
    n_grp = d_s5.shape[-2]
    o_dt = d_ssd_w + d_xbc
    o_u = o_dt + n_heads
    row = lambda v: v.reshape(1, -1).astype(F32)

    wi = w_in[0]
    wz = wi[:, :d_ssd_w].astype(BF16)
    wxbc = wi[:, d_ssd_w:o_dt].astype(BF16)
    wdt = jnp.pad(wi[:, o_dt:o_u], ((0, 0), (0, LANES - n_heads))).astype(BF16)
    wu = wi[:, o_u:].astype(BF16)
    pad_heads = lambda v: jnp.pad(row(v), ((0, 0), (0, LANES - n_heads)))
    convw, convb, gmix = conv_w[0].astype(F32), row(conv_b[0]), row(g_mix[0])

    x2d = x.reshape(bsz * seq, d_model)
    meta_chunk = jnp.concatenate([jnp.zeros((SSD_CHUNK - N_META, d_model), x.dtype), meta_tokens.astype(x.dtype)], axis=0)
    tm = _row_tile(seq, 512)
    no_hist = jnp.zeros((SUBLANES, d_xbc), F32)
    _, xa_m, bc_m, dt_m, f_m, tail_m = _in_proj(meta_chunk, no_hist, gmix, wz, wxbc, wdt, wu, convw, convb, SSD_CHUNK, 1)
    z_r, xa_r, bc_r, dt_r, f_r, _ = _in_proj(x2d, tail_m, gmix, wz, wxbc, wdt, wu, convw, convb, tm, seq // tm)

    seq3 = lambda a: a.reshape(bsz, seq, a.shape[-1])
    expand = (jnp.arange(d_ssd_w)[None, :] // SSD_HEAD_DIM == jnp.arange(LANES)[:, None]).astype(BF16)
    y_ssd = _ssd(seq3(xa_r), xa_m, seq3(bc_r), bc_m, seq3(z_r), seq3(dt_r), dt_m, pad_heads(dt_bias[0]),
                 pad_heads(a_log[0]), row(jnp.repeat(d_ssd[0], SSD_HEAD_DIM)), row(g_ssd[0]), expand)

    t, gs, p = S5_BLOCK, S5_PREP_GROUPS, S5_STATE
    lanes = lambda v: v.astype(F32).reshape(n_grp // gs, 1, gs * p)
    per_grp = lambda v: (v.astype(F32).reshape(n_grp // gs, gs, S5_WIDTH, p).transpose(0, 2, 1, 3)
                         .reshape(n_grp // gs, S5_WIDTH, gs * p))
    mt, wc, ec, dp_re, dp_im = _s5_prep(lanes(lam_re[0]), lanes(lam_im[0]), lanes(jnp.repeat(log_step[0], p)),
                                        per_grp(jnp.swapaxes(b_re[0], 1, 2)), per_grp(jnp.swapaxes(b_im[0], 1, 2)),
                                        per_grp(c_re[0]), per_grp(c_im[0]))
    dsk = jnp.tile(d_s5[0].astype(F32), (1, t)).reshape(n_grp, 1, S5_FLAT)
    yg = _s5_core(f_r, f_m, mt, wc, ec, dp_re, dp_im, dsk, bsz)
    y_s5 = _s5_glu(yg, w_glu[0].astype(BF16), row(b_glu[0]), row(g_s5[0]), tm)

    wo = w_out[0].astype(BF16)
    out = _out_mlp(x2d, y_ssd.reshape(bsz * seq, d_ssd_w), y_s5, wo[:d_ssd_w], wo[d_ssd_w:], row(g_mlp[0]),
                   w_up[0].astype(BF16), w_down[0].astype(BF16), row(g_final), tm, 1024)
    return out.reshape(bsz, seq, d_model).astype(x.dtype)
```

```python
import functools

import jax
import jax.numpy as jnp
from jax import lax
from jax.experimental import pallas as pl
from jax.experimental.pallas import tpu as pltpu

F32 = jnp.float32
BF16 = jnp.bfloat16

NORM_EPS = 1e-5
N_META = 16
SSD_CHUNK = 256
SSD_HEAD_DIM = 64
SSD_GROUPS = 2
SSD_STATE = 128
SSD_CONV = 4
S5_WIDTH = 16
S5_STATE = 64
S5_BLOCK = 16
S5_FLAT = S5_BLOCK * S5_WIDTH
S5_PREP_GROUPS = 8
PLANE = SSD_CHUNK // S5_BLOCK
LANES = 128
SUBLANES = 8
LANE_BLOCKS = LANES // S5_WIDTH
COL_TILE = 256
VMEM_LIMIT_BYTES = 56 * 1024 * 1024

assert PLANE == S5_BLOCK == 2 * SUBLANES


def _rmsnorm(x, g):
    return x * lax.rsqrt(jnp.mean(x * x, axis=-1, keepdims=True) + NORM_EPS) * g


def _silu(x):
    return x * jax.nn.sigmoid(x)


def _gelu(x):
    return 0.5 * x * (1.0 + lax.erf(x * (2.0 ** -0.5)))


def _dot(a, b):
    return jnp.dot(a, b, preferred_element_type=F32)


def _dot_nt(a, b, **kw):
    return lax.dot_general(a, b, (((1,), (1,)), ((), ())), preferred_element_type=F32, **kw)


def _dot_tn(a, b):
    return lax.dot_general(a, b, (((0,), (0,)), ((), ())), preferred_element_type=F32)


def _dot_split(a, m01, parts):
    acc = None
    rem = a
    for _ in range(parts):
        piece = rem.astype(BF16)
        rem = rem - piece.astype(F32)
        term = _dot(piece, m01)
        acc = term if acc is None else acc + term
    return acc


def _params(sem):
    return pltpu.CompilerParams(dimension_semantics=sem, vmem_limit_bytes=VMEM_LIMIT_BYTES)


def _const_spec(shape):
    nd = len(shape)
    return pl.BlockSpec(shape, lambda *_: (0,) * nd)


def _token_of_row(r):
    return (r % PLANE) * S5_BLOCK + r // PLANE


def _transpose8(arrays, pos, axis, unit):
    a = list(arrays)
    size = 8 * unit
    d = 4
    while d:
        keep_lo = (pos & (d * unit)) == 0
        for i in range(8):
            if i & d:
                continue
            lo, hi = a[i], a[i + d]
            a[i] = jnp.where(keep_lo, lo, pltpu.roll(hi, d * unit, axis=axis))
            a[i + d] = jnp.where(keep_lo, pltpu.roll(lo, size - d * unit, axis=axis), hi)
        d //= 2
    return a


def _transpose_lane_blocks(planes, lane):
    return _transpose8(planes, lane, 1, S5_WIDTH)


def _swap_row_order(v):
    sub = lax.broadcasted_iota(jnp.int32, (SUBLANES, v.shape[1]), 0)
    chunks = []
    for chunk in range(v.shape[0] // SSD_CHUNK):
        out = [[None, None] for _ in range(S5_BLOCK)]
        for ah in range(2):
            for bh in range(2):
                first = [chunk * SSD_CHUNK + (SUBLANES * ah + al) * PLANE + SUBLANES * bh for al in range(SUBLANES)]
                src = [v[r:r + SUBLANES] for r in first]
                dst = _transpose8(src, sub, 0, 1)
                for bl in range(SUBLANES):
                    out[SUBLANES * bh + bl][ah] = dst[bl]
        chunks.append(jnp.concatenate([piece for pair in out for piece in pair], axis=0))
    return jnp.concatenate(chunks, axis=0) if len(chunks) > 1 else chunks[0]


def _in_proj_kernel(x_ref, hist_ref, g_ref, wz_ref, wxbc_ref, wdt_ref, wu_ref, convw_ref, convb_ref,
                    z_ref, xa_ref, bc_ref, dt_ref, f_ref, tail_ref, hist_scr, *, tiles_per_seq):
    tm = z_ref.shape[0]
    d_x = xa_ref.shape[1]
    n_hist = (SSD_CONV - 1) * PLANE
    n_chunks = tm // SSD_CHUNK

    @pl.when(lax.rem(pl.program_id(0), tiles_per_seq) == 0)
    def _():
        hist_scr[...] = hist_ref[...]

    n = _rmsnorm(_swap_row_order(x_ref[...]), g_ref[...]).astype(BF16)

    lane = lax.broadcasted_iota(jnp.int32, (PLANE, LANES), 1)
    for ct in range(wu_ref.shape[1] // COL_TILE):
        u = _dot(n, wu_ref[:, ct * COL_TILE:(ct + 1) * COL_TILE])
        for chunk in range(n_chunks):
            for o in range(COL_TILE // LANES):
                oct_ = ct * (COL_TILE // LANES) + o
                for half in range(S5_BLOCK // LANE_BLOCKS):
                    tok = []
                    for s in range(half * LANE_BLOCKS, (half + 1) * LANE_BLOCKS):
                        lo = chunk * SSD_CHUNK + s * PLANE
                        tok.append(u[lo:lo + PLANE, o * LANES:(o + 1) * LANES])
                    grp = _transpose_lane_blocks(tok, lane)
                    for gi in range(LANE_BLOCKS):
                        f_ref[oct_ * LANE_BLOCKS + gi, chunk * PLANE:(chunk + 1) * PLANE,
                              half * LANES:(half + 1) * LANES] = grp[gi].astype(BF16)

    row0 = lax.broadcasted_iota(jnp.int32, (PLANE, 1), 0) == 0
    for ct in range(wxbc_ref.shape[1] // COL_TILE):
        cols = slice(ct * COL_TILE, (ct + 1) * COL_TILE)
        pre = _dot(n, wxbc_ref[:, cols])
        w = [convw_ref[k:k + 1, cols] for k in range(SSD_CONV)]
        prev = hist_scr[:, cols]
        for chunk in range(n_chunks):
            cur = pre[chunk * SSD_CHUNK:(chunk + 1) * SSD_CHUNK]
            last = cur[SSD_CHUNK - n_hist:]
            wrapped = [jnp.where(row0, pltpu.roll(prev[m * PLANE:(m + 1) * PLANE], 1, axis=0),
                                 pltpu.roll(last[m * PLANE:(m + 1) * PLANE], 1, axis=0)) for m in range(SSD_CONV - 1)]
            acc = convb_ref[:, cols] + w[SSD_CONV - 1] * cur
            for k in range(1, SSD_CONV):
                shifted = jnp.concatenate(wrapped[SSD_CONV - 1 - k:] + [cur[:SSD_CHUNK - k * PLANE]], axis=0)
                acc = acc + w[SSD_CONV - 1 - k] * shifted
            act = _silu(acc)
            rows = slice(chunk * SSD_CHUNK, (chunk + 1) * SSD_CHUNK)
            if ct * COL_TILE < d_x:
                xa_ref[rows, cols] = act
            else:
                bc_ref[rows, ct * COL_TILE - d_x:(ct + 1) * COL_TILE - d_x] = act.astype(BF16)
            prev = last
        hist_scr[:, cols] = prev
        tail_ref[:, cols] = prev

    for ct in range(wz_ref.shape[1] // COL_TILE):
        cols = slice(ct * COL_TILE, (ct + 1) * COL_TILE)
        z_ref[:, cols] = _dot(n, wz_ref[:, cols])
    dt_ref[...] = _dot(n, wdt_ref[...])


def _in_proj(x_rows, hist, g, wz, wxbc, wdt, wu, convw, convb, tm, tiles_per_seq):
    rows, d = x_rows.shape
    d_z, d_xbc, d_u = wz.shape[1], wxbc.shape[1], wu.shape[1]
    d_bc = d_xbc - d_z
    n_grp = d_u // S5_WIDTH
    nb = tm // S5_BLOCK
    row_spec = lambda w: pl.BlockSpec((tm, w), lambda i: (i, 0))
    return pl.pallas_call(
        functools.partial(_in_proj_kernel, tiles_per_seq=tiles_per_seq),
        grid=(rows // tm,),
        in_specs=[row_spec(d)] + [_const_spec(a.shape) for a in (hist, g, wz, wxbc, wdt, wu, convw, convb)],
        out_specs=[row_spec(d_z), row_spec(d_z), row_spec(d_bc), row_spec(LANES),
                   pl.BlockSpec((n_grp, nb, S5_FLAT), lambda i: (0, i, 0)), _const_spec(hist.shape)],
        out_shape=[jax.ShapeDtypeStruct((rows, d_z), F32), jax.ShapeDtypeStruct((rows, d_z), F32),
                   jax.ShapeDtypeStruct((rows, d_bc), BF16), jax.ShapeDtypeStruct((rows, LANES), F32),
                   jax.ShapeDtypeStruct((n_grp, rows // S5_BLOCK, S5_FLAT), BF16),
                   jax.ShapeDtypeStruct(hist.shape, F32)],
        scratch_shapes=[pltpu.VMEM(hist.shape, F32)],
        compiler_params=_params(("arbitrary",)),
        name="in_proj",
    )(x_rows, hist, g, wz, wxbc, wdt, wu, convw, convb)


def _ssd_kernel(xa_ref, bc_ref, z_ref, dt_ref, s0_ref, dtb_ref, alog_ref, dskip_ref, g_ref, expand_ref,
                y_ref, s_out_ref, st_ref, yd_ref, *, padded_rows):
    q = SSD_CHUNK
    d_ssd = z_ref.shape[-1]
    n_heads = d_ssd // SSD_HEAD_DIM
    heads_per_group = n_heads // SSD_GROUPS
    gw = heads_per_group * SSD_HEAD_DIM

    @pl.when(pl.program_id(1) == 0)
    def _():
        st_ref[...] = s0_ref[...]

    x_in = xa_ref[0]
    b_in = bc_ref[0, :, :SSD_GROUPS * SSD_STATE]
    c_in = bc_ref[0, :, SSD_GROUPS * SSD_STATE:]

    dt = jax.nn.softplus(dt_ref[0] + dtb_ref[...])
    if padded_rows:
        t_row = _token_of_row(lax.broadcasted_iota(jnp.int32, dt.shape, 0))
        dt = jnp.where(t_row < padded_rows, 0.0, dt)
    d_a = dt * (-jnp.exp(alog_ref[...]))
    t_r = _token_of_row(lax.broadcasted_iota(jnp.int32, (q, q), 0))
    t_c = _token_of_row(lax.broadcasted_iota(jnp.int32, (q, q), 1))
    causal = t_r >= t_c
    a_cs_t = _dot_split(d_a.T, jnp.where(t_c >= t_r, 1.0, 0.0).astype(BF16), 3)
    a_cs = a_cs_t.T

    expand = expand_ref[...]
    dt_x = _dot_split(dt, expand, 2)
    a_x = _dot_split(a_cs, expand, 3)
    a_last = a_x[q - 1:q, :]
    xdt = x_in * dt_x
    xdt_b = xdt.astype(BF16)
    xw_b = (xdt * jnp.exp(a_last - a_x)).astype(BF16)
    ea_x = jnp.exp(a_x)

    for g in range(SSD_GROUPS):
        cg = c_in[:, g * SSD_STATE:(g + 1) * SSD_STATE]
        bg = b_in[:, g * SSD_STATE:(g + 1) * SSD_STATE]
        cb = _dot_nt(cg, bg)
        for r in range(heads_per_group):
            h = g * heads_per_group + r
            seg = a_cs[:, h:h + 1] - a_cs_t[h:h + 1, :]
            att = (cb * jnp.exp(jnp.where(causal, seg, -jnp.inf))).astype(BF16)
            lo = h * SSD_HEAD_DIM
            yd_ref[:, lo:lo + SSD_HEAD_DIM] = _dot(att, xdt_b[:, lo:lo + SSD_HEAD_DIM])
        s_prev = st_ref[g]
        y_off = _dot(cg, s_prev.astype(BF16)) * ea_x[:, g * gw:(g + 1) * gw]
        yd_ref[:, g * gw:(g + 1) * gw] += y_off
        s_new = _dot_tn(bg, xw_b[:, g * gw:(g + 1) * gw])
        st_ref[g] = s_prev * jnp.exp(a_last[:, g * gw:(g + 1) * gw]) + s_new

    s_out_ref[...] = st_ref[...]
    y = yd_ref[...] + x_in * dskip_ref[...]
    y = y * _silu(z_ref[0])
    y_ref[0] = _rmsnorm(y, g_ref[...]).astype(y_ref.dtype)


def _ssd(xa, bc, z, dt, s0, dtb, alog, dskip, g, expand, padded_rows):
    bsz, seq, d_ssd = xa.shape
    q = SSD_CHUNK
    chunk = lambda w: pl.BlockSpec((1, q, w), lambda b, c: (b, c, 0))
    return pl.pallas_call(
        functools.partial(_ssd_kernel, padded_rows=padded_rows),
        grid=(bsz, seq // q),
        in_specs=[chunk(d_ssd), chunk(bc.shape[-1]), chunk(d_ssd), chunk(LANES)]
        + [_const_spec(a.shape) for a in (s0, dtb, alog, dskip, g, expand)],
        out_specs=[chunk(d_ssd), _const_spec(s0.shape)],
        out_shape=[jax.ShapeDtypeStruct((bsz, seq, d_ssd), BF16), jax.ShapeDtypeStruct(s0.shape, F32)],
        scratch_shapes=[pltpu.VMEM(s0.shape, F32), pltpu.VMEM((q, d_ssd), F32)],
        compiler_params=_params(("arbitrary", "arbitrary")),
        name="ssd",
    )(xa, bc, z, dt, s0, dtb, alog, dskip, g, expand)


def _lam_pow(lr, li, step, k):
    kf = k.astype(F32)
    mag = jnp.exp(kf * (lr * step))
    ang = kf * (li * step)
    return mag * jnp.cos(ang), mag * jnp.sin(ang)


def _s5_prep_kernel(lr_ref, li_ref, ls_ref, bt_re_ref, bt_im_ref, c_re_ref, c_im_ref,
                    mt_ref, wc_ref, ec_ref, dp_re_ref, dp_im_ref):
    t, hw, p = S5_BLOCK, S5_WIDTH, S5_STATE
    lr, li = lr_ref[0], li_ref[0]
    step = jnp.exp(ls_ref[0])
    k_up = lax.broadcasted_iota(jnp.int32, (3 * SUBLANES, 1), 0)
    pw_re, pw_im = _lam_pow(lr, li, step, k_up)
    k_dn = (t - 1) - lax.broadcasted_iota(jnp.int32, (t, 1), 0)
    pr_re, pr_im = _lam_pow(lr, li, step, k_dn)
    k_blk = t * lax.broadcasted_iota(jnp.int32, (2 * SUBLANES, 1), 0)
    dp_re, dp_im = _lam_pow(lr, li, step, k_blk)

    ab_re, ab_im = pw_re[1:2, :], pw_im[1:2, :]
    den = lr * lr + li * li
    coef_re = ((ab_re - 1.0) * lr + ab_im * li) / den
    coef_im = (ab_im * lr - (ab_re - 1.0) * li) / den
    bt_re, bt_im = bt_re_ref[0], bt_im_ref[0]
    bb_re = coef_re * bt_re - coef_im * bt_im
    bb_im = coef_re * bt_im + coef_im * bt_re
    c_re, c_im = c_re_ref[0], c_im_ref[0]

    def rep(m):
        return jnp.broadcast_to(m[:, None, :], (t, hw, p)).reshape(t * hw, p)

    def tile(m):
        return jnp.concatenate([m] * t, axis=0)

    hi = lax.Precision.HIGHEST
    lane = lax.broadcasted_iota(jnp.int32, (hw, t * hw), 1)
    for a in range(S5_PREP_GROUPS):
        sl = slice(a * p, (a + 1) * p)
        ct_re, ct_im = tile(c_re[:, sl]), tile(c_im[:, sl])
        l0_re, l0_im = rep(pw_re[0:t, sl]), rep(pw_im[0:t, sl])
        l1_re, l1_im = rep(pw_re[1:t + 1, sl]), rep(pw_im[1:t + 1, sl])
        x0_re = l0_re * ct_re - l0_im * ct_im
        x0_im = l0_re * ct_im + l0_im * ct_re
        kt = _dot_nt(bb_re[:, sl], x0_re, precision=hi) - _dot_nt(bb_im[:, sl], x0_im, precision=hi)
        for s in range(t):
            blk = kt if s == 0 else jnp.where(lane >= s * hw, pltpu.roll(kt, s * hw, axis=1), 0.0)
            mt_ref[a, s * hw:(s + 1) * hw, :] = blk.astype(mt_ref.dtype)
        x1_re = l1_re * ct_re - l1_im * ct_im
        x1_im = l1_re * ct_im + l1_im * ct_re
        ec_ref[a] = jnp.concatenate([x1_re, -x1_im], axis=1).T.astype(ec_ref.dtype)
        lw_re, lw_im = rep(pr_re[:, sl]), rep(pr_im[:, sl])
        bt_re_a, bt_im_a = tile(bb_re[:, sl]), tile(bb_im[:, sl])
        wc_ref[a] = jnp.concatenate([lw_re * bt_re_a - lw_im * bt_im_a, lw_re * bt_im_a + lw_im * bt_re_a],
                                    axis=1).astype(wc_ref.dtype)
        dp_re_ref[a] = jnp.concatenate([dp_re[:, sl]] * 2, axis=1)
        dp_im_ref[a] = jnp.concatenate([dp_im[:, sl]] * 2, axis=1)


def _s5_prep(lr, li, ls, bt_re, bt_im, c_re, c_im):
    steps = lr.shape[0]
    gs, p, fl = S5_PREP_GROUPS, S5_STATE, S5_FLAT
    n_grp = steps * gs
    in_spec = lambda a: pl.BlockSpec((1,) + a.shape[1:], lambda i: (i, 0, 0))
    out_spec = lambda *s: pl.BlockSpec((gs,) + s, lambda i: (i, 0, 0))
    out = lambda *s, dt=BF16: jax.ShapeDtypeStruct((n_grp,) + s, dt)
    args = (lr, li, ls, bt_re, bt_im, c_re, c_im)
    return pl.pallas_call(
        _s5_prep_kernel,
        grid=(steps,),
        in_specs=[in_spec(a) for a in args],
        out_specs=[out_spec(fl, fl), out_spec(fl, 2 * p), out_spec(2 * p, fl), out_spec(2 * SUBLANES, 2 * p),
                   out_spec(2 * SUBLANES, 2 * p)],
        out_shape=[out(fl, fl), out(fl, 2 * p), out(2 * p, fl), out(2 * SUBLANES, 2 * p, dt=F32),
                   out(2 * SUBLANES, 2 * p, dt=F32)],
        compiler_params=_params(("parallel",)),
        name="s5_prep",
    )(*args)


def _s5_core_kernel(f_r_ref, f_m_ref, mt_ref, wc_ref, ec_ref, dp_re_ref, dp_im_ref, dskip_ref,
                    y_ref, wu_re_ref, wu_im_ref, s_re_ref, s_im_ref, *, n_batch):
    p = S5_STATE
    rows_m = f_m_ref.shape[1]
    rows_r = f_r_ref.shape[1] // n_batch
    per_batch = rows_m + rows_r
    n_tiles = per_batch // SUBLANES
    lane_lo = lax.broadcasted_iota(jnp.int32, (1, 2 * p), 1) < p

    def pack(a, b):
        return jnp.where(lane_lo, a, pltpu.roll(b, p, axis=1)), jnp.where(lane_lo, pltpu.roll(a, p, axis=1), b)

    wu_m = [_dot(f_m_ref[a], wc_ref[a]) for a in range(2)]
    wu_r = [_dot(f_r_ref[a], wc_ref[a]) for a in range(2)]
    m_re, m_im = pack(*wu_m)
    r_re, r_im = pack(*wu_r)
    for b in range(n_batch):
        wu_re_ref[b * per_batch:b * per_batch + rows_m, :] = m_re
        wu_im_ref[b * per_batch:b * per_batch + rows_m, :] = m_im
        wu_re_ref[b * per_batch + rows_m:(b + 1) * per_batch, :] = r_re[b * rows_r:(b + 1) * rows_r]
        wu_im_ref[b * per_batch + rows_m:(b + 1) * per_batch, :] = r_im[b * rows_r:(b + 1) * rows_r]

    dp_re = jnp.where(lane_lo, dp_re_ref[0], dp_re_ref[1])
    dp_im = jnp.where(lane_lo, dp_im_ref[0], dp_im_ref[1])
    sub = lax.broadcasted_iota(jnp.int32, (SUBLANES, 2 * p), 0)

    def cmul_add(a_re, a_im, d_re, d_im, x_re, x_im):
        return a_re + d_re * x_re - d_im * x_im, a_im + d_re * x_im + d_im * x_re

    def shift_down(x, k):
        return jnp.where(sub >= k, pltpu.roll(x, k, axis=0), 0.0)

    def tile_step(i, carry):
        new = []
        for b in range(n_batch):
            c_re, c_im = carry[2 * b], carry[2 * b + 1]
            base = pl.multiple_of(b * per_batch + i * SUBLANES, SUBLANES)
            v_re = wu_re_ref[pl.ds(base, SUBLANES), :]
            v_im = wu_im_ref[pl.ds(base, SUBLANES), :]
            for lvl in range(3):
                k = 1 << lvl
                v_re, v_im = cmul_add(v_re, v_im, dp_re[k:k + 1, :], dp_im[k:k + 1, :],
                                      shift_down(v_re, k), shift_down(v_im, k))
            e_re, e_im = cmul_add(shift_down(v_re, 1), shift_down(v_im, 1), dp_re[0:SUBLANES, :],
                                  dp_im[0:SUBLANES, :], c_re, c_im)
            s_re_ref[pl.ds(base, SUBLANES), :] = e_re
            s_im_ref[pl.ds(base, SUBLANES), :] = e_im
            n_re, n_im = cmul_add(v_re[SUBLANES - 1:, :], v_im[SUBLANES - 1:, :], dp_re[SUBLANES:SUBLANES + 1, :],
                                  dp_im[SUBLANES:SUBLANES + 1, :], c_re, c_im)
            new += [n_re, n_im]
        return tuple(new)

    zero = jnp.zeros((1, 2 * p), F32)
    lax.fori_loop(0, n_tiles, tile_step, (zero,) * (2 * n_batch))

    for b in range(n_batch):
        lo = b * per_batch + rows_m
        s_a, s_b = pack(s_re_ref[lo:lo + rows_r, :], s_im_ref[lo:lo + rows_r, :])
        for a, s_ab in enumerate((s_a, s_b)):
            f = f_r_ref[a, b * rows_r:(b + 1) * rows_r, :]
            y = _dot(f, mt_ref[a]) + _dot(s_ab.astype(BF16), ec_ref[a]) + f.astype(F32) * dskip_ref[a]
            y_ref[a, b * rows_r:(b + 1) * rows_r, :] = _gelu(y).astype(y_ref.dtype)


def _s5_core(f_r, f_m, mt, wc, ec, dp_re, dp_im, dskip, n_batch):
    n_grp, rows_r, fl = f_r.shape
    rows_m = f_m.shape[1]
    p2 = 2 * S5_STATE
    rows = n_batch * rows_m + rows_r
    spec = lambda *s: pl.BlockSpec((2,) + s, lambda i: (i, 0, 0))
    return pl.pallas_call(
        functools.partial(_s5_core_kernel, n_batch=n_batch),
        grid=(n_grp // 2,),
        in_specs=[spec(rows_r, fl), spec(rows_m, fl), spec(fl, fl), spec(fl, p2), spec(p2, fl),
                  spec(2 * SUBLANES, p2), spec(2 * SUBLANES, p2), spec(1, fl)],
        out_specs=spec(rows_r, fl),
        out_shape=jax.ShapeDtypeStruct((n_grp, rows_r, fl), BF16),
        scratch_shapes=[pltpu.VMEM((rows, p2), F32)] * 4,
        compiler_params=_params(("parallel",)),
        name="s5_core",
    )(f_r, f_m, mt, wc, ec, dp_re, dp_im, dskip)


def _s5_glu_kernel(yg_ref, w_ref, b_ref, g_ref, o_ref):
    d = o_ref.shape[-1]
    tm = o_ref.shape[0]
    lane = lax.broadcasted_iota(jnp.int32, (PLANE, LANES), 1)
    chunks = []
    for chunk in range(tm // SSD_CHUNK):
        planes = [[None] * (d // LANES) for _ in range(S5_BLOCK)]
        for oct_ in range(d // LANES):
            for half in range(S5_BLOCK // LANE_BLOCKS):
                grp = [yg_ref[oct_ * LANE_BLOCKS + gi, chunk * PLANE:(chunk + 1) * PLANE,
                              half * LANES:(half + 1) * LANES].astype(F32) for gi in range(LANE_BLOCKS)]
                tok = _transpose_lane_blocks(grp, lane)
                for s8 in range(LANE_BLOCKS):
                    planes[half * LANE_BLOCKS + s8][oct_] = tok[s8].astype(BF16)
        chunks.append(jnp.concatenate([jnp.concatenate(pieces, axis=1) for pieces in planes], axis=0))
    y = jnp.concatenate(chunks, axis=0) if len(chunks) > 1 else chunks[0]
    v = _dot(y, w_ref[...]) + b_ref[...]
    o = v[:, :d] * jax.nn.sigmoid(v[:, d:])
    o_ref[...] = _rmsnorm(o, g_ref[...]).astype(o_ref.dtype)


def _s5_glu(yg, w, b, g, tm):
    n_grp, blocks, fl = yg.shape
    rows, d = blocks * S5_BLOCK, n_grp * S5_WIDTH
    nb = tm // S5_BLOCK
    return pl.pallas_call(
        _s5_glu_kernel,
        grid=(rows // tm,),
        in_specs=[pl.BlockSpec((n_grp, nb, fl), lambda i: (0, i, 0)), _const_spec(w.shape), _const_spec(b.shape),
                  _const_spec(g.shape)],
        out_specs=pl.BlockSpec((tm, d), lambda i: (i, 0)),
        out_shape=jax.ShapeDtypeStruct((rows, d), BF16),
        compiler_params=_params(("parallel",)),
        name="s5_glu",
    )(yg, w, b, g)


def _out_mlp_kernel(x_ref, ya_ref, yb_ref, woa_ref, wob_ref, gm_ref, wup_ref, wdn_ref, gf_ref, o_ref, *, ff_tile):
    h = x_ref[...] + _swap_row_order(_dot(ya_ref[...], woa_ref[...]) + _dot(yb_ref[...], wob_ref[...]))
    n = _rmsnorm(h, gm_ref[...]).astype(BF16)
    d_ff = wup_ref.shape[1]
    mlp = None
    for k in range(d_ff // ff_tile):
        m = _dot(n, wup_ref[:, k * ff_tile:(k + 1) * ff_tile])
        m = jnp.square(jnp.maximum(m, 0.0)).astype(BF16)
        part = _dot(m, wdn_ref[k * ff_tile:(k + 1) * ff_tile, :])
        mlp = part if mlp is None else mlp + part
    o_ref[...] = _rmsnorm(h + mlp, gf_ref[...])


def _out_mlp(x_rows, ya, yb, woa, wob, gm, wup, wdn, gf, tm, ff_tile):
    rows, d = ya.shape
    row_spec = pl.BlockSpec((tm, d), lambda i: (i, 0))
    return pl.pallas_call(
        functools.partial(_out_mlp_kernel, ff_tile=ff_tile),
        grid=(rows // tm,),
        in_specs=[row_spec, row_spec, row_spec, _const_spec(woa.shape), _const_spec(wob.shape), _const_spec(gm.shape),
                  _const_spec(wup.shape), _const_spec(wdn.shape), _const_spec(gf.shape)],
        out_specs=row_spec,
        out_shape=jax.ShapeDtypeStruct((rows, d), F32),
        compiler_params=_params(("parallel",)),
        name="out_mlp",
    )(x_rows, ya, yb, woa, wob, gm, wup, wdn, gf)


def _row_tile(rows, want):
    tm = min(rows, want)
    assert rows % tm == 0
    return tm


def kernel(x, meta_tokens, g_mix, w_in, conv_w, conv_b, dt_bias, a_log, d_ssd, g_ssd, lam_re, lam_im, log_step,
           b_re, b_im, c_re, c_im, d_s5, w_glu, b_glu, g_s5, w_out, g_mlp, w_up, w_down, g_final):
    bsz, seq, d_model = x.shape
    assert w_in.shape[0] == 1, "single-layer block: meta-token outputs are only consumed as state"
    assert seq % SSD_CHUNK == 0 and d_model % LANES == 0
    n_heads = dt_bias.shape[-1]
    d_ssd_w = n_heads * SSD_HEAD_DIM
    d_xbc = conv_w.shape[-1]
    n_grp = d_s5.shape[-2]
    o_dt = d_ssd_w + d_xbc
    o_u = o_dt + n_heads
    row = lambda v: v.reshape(1, -1).astype(F32)

    wi = w_in[0]
    wz = wi[:, :d_ssd_w].astype(BF16)
    wxbc = wi[:, d_ssd_w:o_dt].astype(BF16)
    wdt = jnp.pad(wi[:, o_dt:o_u], ((0, 0), (0, LANES - n_heads))).astype(BF16)
    wu = wi[:, o_u:].astype(BF16)
    pad_heads = lambda v: jnp.pad(row(v), ((0, 0), (0, LANES - n_heads)))
    convw, convb, gmix = conv_w[0].astype(F32), row(conv_b[0]), row(g_mix[0])

    x_rows = x.reshape(bsz * seq, d_model)
    meta_chunk = jnp.concatenate([jnp.zeros((SSD_CHUNK - N_META, d_model), x.dtype), meta_tokens.astype(x.dtype)], axis=0)
    tm = _row_tile(seq, 512)
    no_hist = jnp.zeros(((SSD_CONV - 1) * PLANE, d_xbc), F32)
    z_m, xa_m, bc_m, dt_m, f_m, tail_m = _in_proj(meta_chunk, no_hist, gmix, wz, wxbc, wdt, wu,
                                                  convw, convb, SSD_CHUNK, 1)
    z_r, xa_r, bc_r, dt_r, f_r, _ = _in_proj(x_rows, tail_m, gmix, wz, wxbc, wdt, wu, convw, convb, tm, seq // tm)

    seq3 = lambda a, b: a.reshape(b, -1, a.shape[-1])
    expand = (jnp.arange(d_ssd_w)[None, :] // SSD_HEAD_DIM == jnp.arange(LANES)[:, None]).astype(BF16)
    ssd_args = (pad_heads(dt_bias[0]), pad_heads(a_log[0]), row(jnp.repeat(d_ssd[0], SSD_HEAD_DIM)), row(g_ssd[0]), expand)
    no_state = jnp.zeros((SSD_GROUPS, SSD_STATE, d_ssd_w // SSD_GROUPS), F32)
    _, s_meta = _ssd(seq3(xa_m, 1), seq3(bc_m, 1), seq3(z_m, 1), seq3(dt_m, 1), no_state, *ssd_args,
                     padded_rows=SSD_CHUNK - N_META)
    y_ssd, _ = _ssd(seq3(xa_r, bsz), seq3(bc_r, bsz), seq3(z_r, bsz), seq3(dt_r, bsz), s_meta, *ssd_args, padded_rows=0)

    t, gs, p = S5_BLOCK, S5_PREP_GROUPS, S5_STATE
    lanes = lambda v: v.astype(F32).reshape(n_grp // gs, 1, gs * p)
    per_grp = lambda v: (v.astype(F32).reshape(n_grp // gs, gs, S5_WIDTH, p).transpose(0, 2, 1, 3)
                         .reshape(n_grp // gs, S5_WIDTH, gs * p))
    mt, wc, ec, dp_re, dp_im = _s5_prep(lanes(lam_re[0]), lanes(lam_im[0]), lanes(jnp.repeat(log_step[0], p)),
                                        per_grp(jnp.swapaxes(b_re[0], 1, 2)), per_grp(jnp.swapaxes(b_im[0], 1, 2)),
                                        per_grp(c_re[0]), per_grp(c_im[0]))
    dsk = jnp.tile(d_s5[0].astype(F32), (1, t)).reshape(n_grp, 1, S5_FLAT)
    yg = _s5_core(f_r, f_m, mt, wc, ec, dp_re, dp_im, dsk, bsz)
    y_s5 = _s5_glu(yg, w_glu[0].astype(BF16), row(b_glu[0]), row(g_s5[0]), tm)

    wo = w_out[0].astype(BF16)
    out = _out_mlp(x_rows, y_ssd.reshape(bsz * seq, d_ssd_w), y_s5, wo[:d_ssd_w], wo[d_ssd_w:], row(g_mlp[0]),
                   w_up[0].astype(BF16), w_down[0].astype(BF16), row(g_final), tm, 1024)
    return out.reshape(bsz, seq, d_model).astype(x.dtype)
```

```python
import functools

import jax
import jax.numpy as jnp
from jax import lax
from jax.experimental import pallas as pl
from jax.experimental.pallas import tpu as pltpu

F32 = jnp.float32
BF16 = jnp.bfloat16

NORM_EPS = 1e-5
N_META = 16
SSD_CHUNK = 256
SSD_HEAD_DIM = 64
SSD_GROUPS = 2
SSD_STATE = 128
SSD_CONV = 4
S5_WIDTH = 16
S5_STATE = 64
S5_BLOCK = 16
S5_FLAT = S5_BLOCK * S5_WIDTH
S5_PREP_GROUPS = 8
PLANE = SSD_CHUNK // S5_BLOCK
LANES = 128
SUBLANES = 8
LANE_BLOCKS = LANES // S5_WIDTH
COL_TILE = 256
VMEM_LIMIT_BYTES = 56 * 1024 * 1024

assert PLANE == S5_BLOCK == 2 * SUBLANES


def _rmsnorm(x, g):
    return x * lax.rsqrt(jnp.mean(x * x, axis=-1, keepdims=True) + NORM_EPS) * g


def _silu(x):
    return x * jax.nn.sigmoid(x)


def _gelu(x):
    return 0.5 * x * (1.0 + lax.erf(x * (2.0 ** -0.5)))


def _dot(a, b):
    return jnp.dot(a, b, preferred_element_type=F32)


def _dot_nt(a, b, **kw):
    return lax.dot_general(a, b, (((1,), (1,)), ((), ())), preferred_element_type=F32, **kw)


def _dot_tn(a, b):
    return lax.dot_general(a, b, (((0,), (0,)), ((), ())), preferred_element_type=F32)


def _dot_split(a, m01, parts):
    acc = None
    rem = a
    for _ in range(parts):
        piece = rem.astype(BF16)
        rem = rem - piece.astype(F32)
        term = _dot(piece, m01)
        acc = term if acc is None else acc + term
    return acc


def _params(sem):
    return pltpu.CompilerParams(dimension_semantics=sem, vmem_limit_bytes=VMEM_LIMIT_BYTES)


def _const_spec(shape, single_buffer=False):
    nd = len(shape)
    mode = pl.Buffered(1) if single_buffer else None
    return pl.BlockSpec(shape, lambda *_: (0,) * nd, pipeline_mode=mode)


def _token_of_row(r):
    return (r % PLANE) * S5_BLOCK + r // PLANE


def _transpose8(arrays, pos, axis, unit):
    a = list(arrays)
    size = 8 * unit
    d = 4
    while d:
        keep_lo = (pos & (d * unit)) == 0
        for i in range(8):
            if i & d:
                continue
            lo, hi = a[i], a[i + d]
            a[i] = jnp.where(keep_lo, lo, pltpu.roll(hi, d * unit, axis=axis))
            a[i + d] = jnp.where(keep_lo, pltpu.roll(lo, size - d * unit, axis=axis), hi)
        d //= 2
    return a


def _transpose_lane_blocks(planes, lane):
    return _transpose8(planes, lane, 1, S5_WIDTH)


def _swap_row_order(v):
    sub = lax.broadcasted_iota(jnp.int32, (SUBLANES, v.shape[1]), 0)
    chunks = []
    for chunk in range(v.shape[0] // SSD_CHUNK):
        out = [[None, None] for _ in range(S5_BLOCK)]
        for ah in range(2):
            for bh in range(2):
                first = [chunk * SSD_CHUNK + (SUBLANES * ah + al) * PLANE + SUBLANES * bh for al in range(SUBLANES)]
                src = [v[r:r + SUBLANES] for r in first]
                dst = _transpose8(src, sub, 0, 1)
                for bl in range(SUBLANES):
                    out[SUBLANES * bh + bl][ah] = dst[bl]
        chunks.append(jnp.concatenate([piece for pair in out for piece in pair], axis=0))
    return jnp.concatenate(chunks, axis=0) if len(chunks) > 1 else chunks[0]


def _in_proj_kernel(x_ref, hist_ref, g_ref, wz_ref, wxbc_ref, wdt_ref, wu_ref, convw_ref, convb_ref,
                    z_ref, xa_ref, bc_ref, dt_ref, f_ref, tail_ref, hist_scr, *, tiles_per_seq):
    tm = z_ref.shape[0]
    d_x = xa_ref.shape[1]
    n_hist = (SSD_CONV - 1) * PLANE
    n_chunks = tm // SSD_CHUNK

    @pl.when(lax.rem(pl.program_id(0), tiles_per_seq) == 0)
    def _():
        hist_scr[...] = hist_ref[...]

    n = _rmsnorm(_swap_row_order(x_ref[...]), g_ref[...]).astype(BF16)

    lane = lax.broadcasted_iota(jnp.int32, (PLANE, LANES), 1)
    for ct in range(wu_ref.shape[1] // COL_TILE):
        u = _dot(n, wu_ref[:, ct * COL_TILE:(ct + 1) * COL_TILE])
        for chunk in range(n_chunks):
            for o in range(COL_TILE // LANES):
                oct_ = ct * (COL_TILE // LANES) + o
                for half in range(S5_BLOCK // LANE_BLOCKS):
                    tok = []
                    for s in range(half * LANE_BLOCKS, (half + 1) * LANE_BLOCKS):
                        lo = chunk * SSD_CHUNK + s * PLANE
                        tok.append(u[lo:lo + PLANE, o * LANES:(o + 1) * LANES])
                    grp = _transpose_lane_blocks(tok, lane)
                    for gi in range(LANE_BLOCKS):
                        f_ref[oct_ * LANE_BLOCKS + gi, chunk * PLANE:(chunk + 1) * PLANE,
                              half * LANES:(half + 1) * LANES] = grp[gi].astype(BF16)

    row0 = lax.broadcasted_iota(jnp.int32, (PLANE, 1), 0) == 0
    for ct in range(wxbc_ref.shape[1] // COL_TILE):
        cols = slice(ct * COL_TILE, (ct + 1) * COL_TILE)
        pre = _dot(n, wxbc_ref[:, cols])
        w = [convw_ref[k:k + 1, cols] for k in range(SSD_CONV)]
        prev = hist_scr[:, cols]
        for chunk in range(n_chunks):
            cur = pre[chunk * SSD_CHUNK:(chunk + 1) * SSD_CHUNK]
            last = cur[SSD_CHUNK - n_hist:]
            wrapped = [jnp.where(row0, pltpu.roll(prev[m * PLANE:(m + 1) * PLANE], 1, axis=0),
                                 pltpu.roll(last[m * PLANE:(m + 1) * PLANE], 1, axis=0)) for m in range(SSD_CONV - 1)]
            acc = convb_ref[:, cols] + w[SSD_CONV - 1] * cur
            for k in range(1, SSD_CONV):
                shifted = jnp.concatenate(wrapped[SSD_CONV - 1 - k:] + [cur[:SSD_CHUNK - k * PLANE]], axis=0)
                acc = acc + w[SSD_CONV - 1 - k] * shifted
            act = _silu(acc)
            rows = slice(chunk * SSD_CHUNK, (chunk + 1) * SSD_CHUNK)
            if ct * COL_TILE < d_x:
                xa_ref[rows, cols] = act
            else:
                bc_ref[rows, ct * COL_TILE - d_x:(ct + 1) * COL_TILE - d_x] = act.astype(BF16)
            prev = last
        hist_scr[:, cols] = prev
        tail_ref[:, cols] = prev

    for ct in range(wz_ref.shape[1] // COL_TILE):
        cols = slice(ct * COL_TILE, (ct + 1) * COL_TILE)
        z_ref[:, cols] = _dot(n, wz_ref[:, cols])
    dt_ref[...] = _dot(n, wdt_ref[...])


def _in_proj(x_rows, hist, g, wz, wxbc, wdt, wu, convw, convb, tm, tiles_per_seq):
    rows, d = x_rows.shape
    d_z, d_xbc, d_u = wz.shape[1], wxbc.shape[1], wu.shape[1]
    d_bc = d_xbc - d_z
    n_grp = d_u // S5_WIDTH
    nb = tm // S5_BLOCK
    row_spec = lambda w: pl.BlockSpec((tm, w), lambda i: (i, 0))
    return pl.pallas_call(
        functools.partial(_in_proj_kernel, tiles_per_seq=tiles_per_seq),
        grid=(rows // tm,),
        in_specs=[row_spec(d)] + [_const_spec(a.shape) for a in (hist, g, wz, wxbc, wdt, wu, convw, convb)],
        out_specs=[row_spec(d_z), row_spec(d_z), row_spec(d_bc), row_spec(LANES),
                   pl.BlockSpec((n_grp, nb, S5_FLAT), lambda i: (0, i, 0)), _const_spec(hist.shape)],
        out_shape=[jax.ShapeDtypeStruct((rows, d_z), F32), jax.ShapeDtypeStruct((rows, d_z), F32),
                   jax.ShapeDtypeStruct((rows, d_bc), BF16), jax.ShapeDtypeStruct((rows, LANES), F32),
                   jax.ShapeDtypeStruct((n_grp, rows // S5_BLOCK, S5_FLAT), BF16),
                   jax.ShapeDtypeStruct(hist.shape, F32)],
        scratch_shapes=[pltpu.VMEM(hist.shape, F32)],
        compiler_params=_params(("arbitrary",)),
        name="in_proj",
    )(x_rows, hist, g, wz, wxbc, wdt, wu, convw, convb)


def _ssd_chunk_stages(x_in, bc, z, dt_raw, dtb, alog, dskip, g_norm, expand, st_ref, yd_ref, padded_rows, emit):
    q = SSD_CHUNK
    d_ssd = x_in.shape[-1]
    n_heads = d_ssd // SSD_HEAD_DIM
    heads_per_group = n_heads // SSD_GROUPS
    gw = heads_per_group * SSD_HEAD_DIM
    b_in = bc[:, :SSD_GROUPS * SSD_STATE]
    c_in = bc[:, SSD_GROUPS * SSD_STATE:]

    dt = jax.nn.softplus(dt_raw + dtb)
    if padded_rows:
        t_row = _token_of_row(lax.broadcasted_iota(jnp.int32, dt.shape, 0))
        dt = jnp.where(t_row < padded_rows, 0.0, dt)
    d_a = dt * (-jnp.exp(alog))
    t_r = _token_of_row(lax.broadcasted_iota(jnp.int32, (q, q), 0))
    t_c = _token_of_row(lax.broadcasted_iota(jnp.int32, (q, q), 1))
    causal = t_r >= t_c
    a_cs_t = _dot_split(d_a.T, jnp.where(t_c >= t_r, 1.0, 0.0).astype(BF16), 3)
    a_cs = a_cs_t.T
    yield

    dt_x = _dot_split(dt, expand, 2)
    a_x = _dot_split(a_cs, expand, 3)
    a_last = a_x[q - 1:q, :]
    xdt = x_in * dt_x
    xdt_b = xdt.astype(BF16)
    xw_b = (xdt * jnp.exp(a_last - a_x)).astype(BF16)
    ea_x = jnp.exp(a_x)
    yield

    for g in range(SSD_GROUPS):
        cg = c_in[:, g * SSD_STATE:(g + 1) * SSD_STATE]
        bg = b_in[:, g * SSD_STATE:(g + 1) * SSD_STATE]
        cb = _dot_nt(cg, bg)
        for r in range(heads_per_group):
            h = g * heads_per_group + r
            seg = a_cs[:, h:h + 1] - a_cs_t[h:h + 1, :]
            att = (cb * jnp.exp(jnp.where(causal, seg, -jnp.inf))).astype(BF16)
            lo = h * SSD_HEAD_DIM
            yd_ref[:, lo:lo + SSD_HEAD_DIM] = _dot(att, xdt_b[:, lo:lo + SSD_HEAD_DIM])
            if r % 2:
                yield
        s_prev = st_ref[g]
        y_off = _dot(cg, s_prev.astype(BF16)) * ea_x[:, g * gw:(g + 1) * gw]
        yd_ref[:, g * gw:(g + 1) * gw] += y_off
        s_new = _dot_tn(bg, xw_b[:, g * gw:(g + 1) * gw])
        st_ref[g] = s_prev * jnp.exp(a_last[:, g * gw:(g + 1) * gw]) + s_new
        yield

    y = yd_ref[...] + x_in * dskip
    y = y * _silu(z)
    emit(_rmsnorm(y, g_norm).astype(BF16))
    yield


def _run_interleaved(first, second, ratio):
    live = [first, second]
    while live:
        for gen, n in ((first, ratio), (second, 1)):
            for _ in range(n):
                if gen in live and next(gen, StopIteration) is StopIteration:
                    live.remove(gen)


def _ssd_state_kernel(xa_ref, bc_ref, z_ref, dt_ref, dtb_ref, alog_ref, dskip_ref, g_ref, expand_ref,
                      s_out_ref, st_ref, yd_ref, *, padded_rows):
    st_ref[...] = jnp.zeros(st_ref.shape, F32)
    for _ in _ssd_chunk_stages(xa_ref[...], bc_ref[...], z_ref[...], dt_ref[...], dtb_ref[...], alog_ref[...],
                               dskip_ref[...], g_ref[...], expand_ref[...], st_ref, yd_ref, padded_rows, lambda y: None):
        pass
    s_out_ref[...] = st_ref[...]


def _ssd_state(xa, bc, z, dt, dtb, alog, dskip, g, expand, padded_rows):
    q, d_ssd = xa.shape
    state = (SSD_GROUPS, SSD_STATE, d_ssd // SSD_GROUPS)
    args = (xa, bc, z, dt, dtb, alog, dskip, g, expand)
    return pl.pallas_call(
        functools.partial(_ssd_state_kernel, padded_rows=padded_rows),
        grid=(1,),
        in_specs=[_const_spec(a.shape) for a in args],
        out_specs=_const_spec(state),
        out_shape=jax.ShapeDtypeStruct(state, F32),
        scratch_shapes=[pltpu.VMEM(state, F32), pltpu.VMEM((q, d_ssd), F32)],
        compiler_params=_params(("arbitrary",)),
        name="ssd_meta",
    )(*args)


def _lam_pow(lr, li, step, k):
    kf = k.astype(F32)
    mag = jnp.exp(kf * (lr * step))
    ang = kf * (li * step)
    return mag * jnp.cos(ang), mag * jnp.sin(ang)


def _s5_prep_kernel(lr_ref, li_ref, ls_ref, bt_re_ref, bt_im_ref, c_re_ref, c_im_ref,
                    mt_ref, wc_ref, ec_ref, dp_re_ref, dp_im_ref):
    t, hw, p = S5_BLOCK, S5_WIDTH, S5_STATE
    lr, li = lr_ref[0], li_ref[0]
    step = jnp.exp(ls_ref[0])
    k_up = lax.broadcasted_iota(jnp.int32, (3 * SUBLANES, 1), 0)
    pw_re, pw_im = _lam_pow(lr, li, step, k_up)
    k_dn = (t - 1) - lax.broadcasted_iota(jnp.int32, (t, 1), 0)
    pr_re, pr_im = _lam_pow(lr, li, step, k_dn)
    k_blk = t * lax.broadcasted_iota(jnp.int32, (2 * SUBLANES, 1), 0)
    dp_re, dp_im = _lam_pow(lr, li, step, k_blk)

    ab_re, ab_im = pw_re[1:2, :], pw_im[1:2, :]
    den = lr * lr + li * li
    coef_re = ((ab_re - 1.0) * lr + ab_im * li) / den
    coef_im = (ab_im * lr - (ab_re - 1.0) * li) / den
    bt_re, bt_im = bt_re_ref[0], bt_im_ref[0]
    bb_re = coef_re * bt_re - coef_im * bt_im
    bb_im = coef_re * bt_im + coef_im * bt_re
    c_re, c_im = c_re_ref[0], c_im_ref[0]

    def rep(m):
        return jnp.broadcast_to(m[:, None, :], (t, hw, p)).reshape(t * hw, p)

    def tile(m):
        return jnp.concatenate([m] * t, axis=0)

    hi = lax.Precision.HIGHEST
    lane = lax.broadcasted_iota(jnp.int32, (hw, t * hw), 1)
    for a in range(S5_PREP_GROUPS):
        sl = slice(a * p, (a + 1) * p)
        ct_re, ct_im = tile(c_re[:, sl]), tile(c_im[:, sl])
        l0_re, l0_im = rep(pw_re[0:t, sl]), rep(pw_im[0:t, sl])
        l1_re, l1_im = rep(pw_re[1:t + 1, sl]), rep(pw_im[1:t + 1, sl])
        x0_re = l0_re * ct_re - l0_im * ct_im
        x0_im = l0_re * ct_im + l0_im * ct_re
        kt = _dot_nt(bb_re[:, sl], x0_re, precision=hi) - _dot_nt(bb_im[:, sl], x0_im, precision=hi)
        for s in range(t):
            blk = kt if s == 0 else jnp.where(lane >= s * hw, pltpu.roll(kt, s * hw, axis=1), 0.0)
            mt_ref[a, s * hw:(s + 1) * hw, :] = blk.astype(mt_ref.dtype)
        x1_re = l1_re * ct_re - l1_im * ct_im
        x1_im = l1_re * ct_im + l1_im * ct_re
        ec_ref[a] = jnp.concatenate([x1_re, -x1_im], axis=1).T.astype(ec_ref.dtype)
        lw_re, lw_im = rep(pr_re[:, sl]), rep(pr_im[:, sl])
        bt_re_a, bt_im_a = tile(bb_re[:, sl]), tile(bb_im[:, sl])
        wc_ref[a] = jnp.concatenate([lw_re * bt_re_a - lw_im * bt_im_a, lw_re * bt_im_a + lw_im * bt_re_a],
                                    axis=1).astype(wc_ref.dtype)
        dp_re_ref[a] = jnp.concatenate([dp_re[:, sl]] * 2, axis=1)
        dp_im_ref[a] = jnp.concatenate([dp_im[:, sl]] * 2, axis=1)


def _s5_prep(lr, li, ls, bt_re, bt_im, c_re, c_im):
    steps = lr.shape[0]
    gs, p, fl = S5_PREP_GROUPS, S5_STATE, S5_FLAT
    n_grp = steps * gs
    in_spec = lambda a: pl.BlockSpec((1,) + a.shape[1:], lambda i: (i, 0, 0))
    out_spec = lambda *s: pl.BlockSpec((gs,) + s, lambda i: (i, 0, 0))
    out = lambda *s, dt=BF16: jax.ShapeDtypeStruct((n_grp,) + s, dt)
    args = (lr, li, ls, bt_re, bt_im, c_re, c_im)
    return pl.pallas_call(
        _s5_prep_kernel,
        grid=(steps,),
        in_specs=[in_spec(a) for a in args],
        out_specs=[out_spec(fl, fl), out_spec(fl, 2 * p), out_spec(2 * p, fl), out_spec(2 * SUBLANES, 2 * p),
                   out_spec(2 * SUBLANES, 2 * p)],
        out_shape=[out(fl, fl), out(fl, 2 * p), out(2 * p, fl), out(2 * SUBLANES, 2 * p, dt=F32),
                   out(2 * SUBLANES, 2 * p, dt=F32)],
        compiler_params=_params(("parallel",)),
        name="s5_prep",
    )(*args)


def _s5_core_kernel(f_r_ref, f_m_ref, mt_ref, wc_ref, ec_ref, dp_re_ref, dp_im_ref, dskip_ref,
                    y_ref, wu_re_ref, wu_im_ref, s_re_ref, s_im_ref, *, n_batch):
    p = S5_STATE
    rows_m = f_m_ref.shape[1]
    rows_r = f_r_ref.shape[1] // n_batch
    per_batch = rows_m + rows_r
    n_tiles = per_batch // SUBLANES
    lane_lo = lax.broadcasted_iota(jnp.int32, (1, 2 * p), 1) < p

    def pack(a, b):
        return jnp.where(lane_lo, a, pltpu.roll(b, p, axis=1)), jnp.where(lane_lo, pltpu.roll(a, p, axis=1), b)

    wu_m = [_dot(f_m_ref[a], wc_ref[a]) for a in range(2)]
    wu_r = [_dot(f_r_ref[a], wc_ref[a]) for a in range(2)]
    m_re, m_im = pack(*wu_m)
    r_re, r_im = pack(*wu_r)
    for b in range(n_batch):
        wu_re_ref[b * per_batch:b * per_batch + rows_m, :] = m_re
        wu_im_ref[b * per_batch:b * per_batch + rows_m, :] = m_im
        wu_re_ref[b * per_batch + rows_m:(b + 1) * per_batch, :] = r_re[b * rows_r:(b + 1) * rows_r]
        wu_im_ref[b * per_batch + rows_m:(b + 1) * per_batch, :] = r_im[b * rows_r:(b + 1) * rows_r]

    dp_re = jnp.where(lane_lo, dp_re_ref[0], dp_re_ref[1])
    dp_im = jnp.where(lane_lo, dp_im_ref[0], dp_im_ref[1])
    sub = lax.broadcasted_iota(jnp.int32, (SUBLANES, 2 * p), 0)

    def cmul_add(a_re, a_im, d_re, d_im, x_re, x_im):
        return a_re + d_re * x_re - d_im * x_im, a_im + d_re * x_im + d_im * x_re

    def shift_down(x, k):
        return jnp.where(sub >= k, pltpu.roll(x, k, axis=0), 0.0)

    def tile_step(i, carry):
        new = []
        for b in range(n_batch):
            c_re, c_im = carry[2 * b], carry[2 * b + 1]
            base = pl.multiple_of(b * per_batch + i * SUBLANES, SUBLANES)
            v_re = wu_re_ref[pl.ds(base, SUBLANES), :]
            v_im = wu_im_ref[pl.ds(base, SUBLANES), :]
            for lvl in range(3):
                k = 1 << lvl
                v_re, v_im = cmul_add(v_re, v_im, dp_re[k:k + 1, :], dp_im[k:k + 1, :],
                                      shift_down(v_re, k), shift_down(v_im, k))
            e_re, e_im = cmul_add(shift_down(v_re, 1), shift_down(v_im, 1), dp_re[0:SUBLANES, :],
                                  dp_im[0:SUBLANES, :], c_re, c_im)
            s_re_ref[pl.ds(base, SUBLANES), :] = e_re
            s_im_ref[pl.ds(base, SUBLANES), :] = e_im
            n_re, n_im = cmul_add(v_re[SUBLANES - 1:, :], v_im[SUBLANES - 1:, :], dp_re[SUBLANES:SUBLANES + 1, :],
                                  dp_im[SUBLANES:SUBLANES + 1, :], c_re, c_im)
            new += [n_re, n_im]
        return tuple(new)

    zero = jnp.zeros((1, 2 * p), F32)
    lax.fori_loop(0, n_tiles, tile_step, (zero,) * (2 * n_batch))

    for b in range(n_batch):
        lo = b * per_batch + rows_m
        s_a, s_b = pack(s_re_ref[lo:lo + rows_r, :], s_im_ref[lo:lo + rows_r, :])
        for a, s_ab in enumerate((s_a, s_b)):
            f = f_r_ref[a, b * rows_r:(b + 1) * rows_r, :]
            y = _dot(f, mt_ref[a]) + _dot(s_ab.astype(BF16), ec_ref[a]) + f.astype(F32) * dskip_ref[a]
            y_ref[a, b * rows_r:(b + 1) * rows_r, :] = _gelu(y).astype(y_ref.dtype)


def _s5_core(f_r, f_m, mt, wc, ec, dp_re, dp_im, dskip, n_batch):
    n_grp, rows_r, fl = f_r.shape
    rows_m = f_m.shape[1]
    p2 = 2 * S5_STATE
    rows = n_batch * rows_m + rows_r
    spec = lambda *s: pl.BlockSpec((2,) + s, lambda i: (i, 0, 0))
    return pl.pallas_call(
        functools.partial(_s5_core_kernel, n_batch=n_batch),
        grid=(n_grp // 2,),
        in_specs=[spec(rows_r, fl), spec(rows_m, fl), spec(fl, fl), spec(fl, p2), spec(p2, fl),
                  spec(2 * SUBLANES, p2), spec(2 * SUBLANES, p2), spec(1, fl)],
        out_specs=spec(rows_r, fl),
        out_shape=jax.ShapeDtypeStruct((n_grp, rows_r, fl), BF16),
        scratch_shapes=[pltpu.VMEM((rows, p2), F32)] * 4,
        compiler_params=_params(("parallel",)),
        name="s5_core",
    )(f_r, f_m, mt, wc, ec, dp_re, dp_im, dskip)


def _s5_glu_kernel(yg_ref, w_ref, b_ref, g_ref, o_ref):
    d = o_ref.shape[-1]
    tm = o_ref.shape[0]
    lane = lax.broadcasted_iota(jnp.int32, (PLANE, LANES), 1)
    chunks = []
    for chunk in range(tm // SSD_CHUNK):
        planes = [[None] * (d // LANES) for _ in range(S5_BLOCK)]
        for oct_ in range(d // LANES):
            for half in range(S5_BLOCK // LANE_BLOCKS):
                grp = [yg_ref[oct_ * LANE_BLOCKS + gi, chunk * PLANE:(chunk + 1) * PLANE,
                              half * LANES:(half + 1) * LANES].astype(F32) for gi in range(LANE_BLOCKS)]
                tok = _transpose_lane_blocks(grp, lane)
                for s8 in range(LANE_BLOCKS):
                    planes[half * LANE_BLOCKS + s8][oct_] = tok[s8].astype(BF16)
        chunks.append(jnp.concatenate([jnp.concatenate(pieces, axis=1) for pieces in planes], axis=0))
    y = jnp.concatenate(chunks, axis=0) if len(chunks) > 1 else chunks[0]
    v = _dot(y, w_ref[...]) + b_ref[...]
    o = v[:, :d] * jax.nn.sigmoid(v[:, d:])
    o_ref[...] = _rmsnorm(o, g_ref[...]).astype(o_ref.dtype)


def _s5_glu(yg, w, b, g, tm):
    n_grp, blocks, fl = yg.shape
    rows, d = blocks * S5_BLOCK, n_grp * S5_WIDTH
    nb = tm // S5_BLOCK
    return pl.pallas_call(
        _s5_glu_kernel,
        grid=(rows // tm,),
        in_specs=[pl.BlockSpec((n_grp, nb, fl), lambda i: (0, i, 0)), _const_spec(w.shape), _const_spec(b.shape),
                  _const_spec(g.shape)],
        out_specs=pl.BlockSpec((tm, d), lambda i: (i, 0)),
        out_shape=jax.ShapeDtypeStruct((rows, d), BF16),
        compiler_params=_params(("parallel",)),
        name="s5_glu",
    )(yg, w, b, g)


def _ssd_out_mlp_kernel(xa_ref, bc_ref, z_ref, dt_ref, s0_ref, dtb_ref, alog_ref, dskip_ref, gs_ref, expand_ref,
                        x_ref, yb_ref, woa_ref, wob_ref, gm_ref, wup_ref, wdn_ref, gf_ref,
                        o_ref, st_ref, yd_ref, ya_scr, *, tiles_per_seq, n_tiles, ff_tile):
    k = pl.program_id(0)
    tile = jnp.minimum(k, n_tiles - 1)
    tm = x_ref.shape[0]

    @pl.when(k == 0)
    def _():
        ya_scr[...] = jnp.zeros(ya_scr.shape, ya_scr.dtype)

    @pl.when(lax.rem(tile, tiles_per_seq) == 0)
    def _():
        st_ref[...] = s0_ref[...]

    def mlp_stages():
        attn = _dot(ya_scr[lax.rem(k + 1, 2)], woa_ref[...])
        yield
        attn = attn + _dot(yb_ref[...], wob_ref[...])
        yield
        h = x_ref[...] + _swap_row_order(attn)
        n = _rmsnorm(h, gm_ref[...]).astype(BF16)
        yield
        mlp = None
        for j in range(wup_ref.shape[1] // ff_tile):
            m = _dot(n, wup_ref[:, j * ff_tile:(j + 1) * ff_tile])
            m = jnp.square(jnp.maximum(m, 0.0)).astype(BF16)
            yield
            part = _dot(m, wdn_ref[j * ff_tile:(j + 1) * ff_tile, :])
            mlp = part if mlp is None else mlp + part
            yield
        o_ref[...] = _rmsnorm(h + mlp, gf_ref[...])
        yield

    def ssd_stages():
        slot = lax.rem(k, 2)
        for c in range(tm // SSD_CHUNK):
            rows = slice(c * SSD_CHUNK, (c + 1) * SSD_CHUNK)

            def emit(y, rows=rows):
                ya_scr[slot, rows, :] = y

            yield from _ssd_chunk_stages(xa_ref[rows, :], bc_ref[rows, :], z_ref[rows, :], dt_ref[rows, :],
                                         dtb_ref[...], alog_ref[...], dskip_ref[...], gs_ref[...], expand_ref[...],
                                         st_ref, yd_ref, 0, emit)

    _run_interleaved(ssd_stages(), mlp_stages(), 2)


def _ssd_out_mlp(xa, bc, z, dt, s0, dtb, alog, dskip, gs, expand, x_rows, yb, woa, wob, gm, wup, wdn, gf,
                 tm, tiles_per_seq, ff_tile):
    rows, d = x_rows.shape
    n_tiles = rows // tm
    d_ssd = xa.shape[-1]
    cur = lambda w: pl.BlockSpec((tm, w), lambda k: (jnp.minimum(k, n_tiles - 1), 0))
    prev = lambda w: pl.BlockSpec((tm, w), lambda k: (jnp.maximum(k - 1, 0), 0))
    consts = (s0, dtb, alog, dskip, gs, expand)
    weights = (woa, wob, gm, wup, wdn, gf)
    return pl.pallas_call(
        functools.partial(_ssd_out_mlp_kernel, tiles_per_seq=tiles_per_seq, n_tiles=n_tiles, ff_tile=ff_tile),
        grid=(n_tiles + 1,),
        in_specs=[cur(d_ssd), cur(bc.shape[-1]), cur(d_ssd), cur(LANES)] + [_const_spec(a.shape) for a in consts]
        + [prev(d), prev(d)] + [_const_spec(a.shape, single_buffer=True) for a in weights],
        out_specs=prev(d),
        out_shape=jax.ShapeDtypeStruct((rows, d), F32),
        scratch_shapes=[pltpu.VMEM(s0.shape, F32), pltpu.VMEM((SSD_CHUNK, d_ssd), F32), pltpu.VMEM((2, tm, d_ssd), BF16)],
        compiler_params=_params(("arbitrary",)),
        name="ssd_out_mlp",
    )(xa, bc, z, dt, *consts, x_rows, yb, *weights)


def _row_tile(rows, want):
    tm = min(rows, want)
    assert rows % tm == 0
    return tm


def kernel(x, meta_tokens, g_mix, w_in, conv_w, conv_b, dt_bias, a_log, d_ssd, g_ssd, lam_re, lam_im, log_step,
           b_re, b_im, c_re, c_im, d_s5, w_glu, b_glu, g_s5, w_out, g_mlp, w_up, w_down, g_final):
    bsz, seq, d_model = x.shape
    assert w_in.shape[0] == 1, "single-layer block: meta-token outputs are only consumed as state"
    assert seq % SSD_CHUNK == 0 and d_model % LANES == 0
    n_heads = dt_bias.shape[-1]
    d_ssd_w = n_heads * SSD_HEAD_DIM
    d_xbc = conv_w.shape[-1]
    n_grp = d_s5.shape[-2]
    o_dt = d_ssd_w + d_xbc
    o_u = o_dt + n_heads
    row = lambda v: v.reshape(1, -1).astype(F32)

    wi = w_in[0]
    wz = wi[:, :d_ssd_w].astype(BF16)
    wxbc = wi[:, d_ssd_w:o_dt].astype(BF16)
    wdt = jnp.pad(wi[:, o_dt:o_u], ((0, 0), (0, LANES - n_heads))).astype(BF16)
    wu = wi[:, o_u:].astype(BF16)
    pad_heads = lambda v: jnp.pad(row(v), ((0, 0), (0, LANES - n_heads)))
    convw, convb, gmix = conv_w[0].astype(F32), row(conv_b[0]), row(g_mix[0])

    x_rows = x.reshape(bsz * seq, d_model)
    meta_chunk = jnp.concatenate([jnp.zeros((SSD_CHUNK - N_META, d_model), x.dtype), meta_tokens.astype(x.dtype)], axis=0)
    tm = _row_tile(seq, 512)
    no_hist = jnp.zeros(((SSD_CONV - 1) * PLANE, d_xbc), F32)
    z_m, xa_m, bc_m, dt_m, f_m, tail_m = _in_proj(meta_chunk, no_hist, gmix, wz, wxbc, wdt, wu,
                                                  convw, convb, SSD_CHUNK, 1)
    z_r, xa_r, bc_r, dt_r, f_r, _ = _in_proj(x_rows, tail_m, gmix, wz, wxbc, wdt, wu, convw, convb, tm, seq // tm)

    expand = (jnp.arange(d_ssd_w)[None, :] // SSD_HEAD_DIM == jnp.arange(LANES)[:, None]).astype(BF16)
    ssd_args = (pad_heads(dt_bias[0]), pad_heads(a_log[0]), row(jnp.repeat(d_ssd[0], SSD_HEAD_DIM)), row(g_ssd[0]), expand)
    s_meta = _ssd_state(xa_m, bc_m, z_m, dt_m, *ssd_args, padded_rows=SSD_CHUNK - N_META)

    t, gs, p = S5_BLOCK, S5_PREP_GROUPS, S5_STATE
    lanes = lambda v: v.astype(F32).reshape(n_grp // gs, 1, gs * p)
    per_grp = lambda v: (v.astype(F32).reshape(n_grp // gs, gs, S5_WIDTH, p).transpose(0, 2, 1, 3)
                         .reshape(n_grp // gs, S5_WIDTH, gs * p))
    mt, wc, ec, dp_re, dp_im = _s5_prep(lanes(lam_re[0]), lanes(lam_im[0]), lanes(jnp.repeat(log_step[0], p)),
                                        per_grp(jnp.swapaxes(b_re[0], 1, 2)), per_grp(jnp.swapaxes(b_im[0], 1, 2)),
                                        per_grp(c_re[0]), per_grp(c_im[0]))
    dsk = jnp.tile(d_s5[0].astype(F32), (1, t)).reshape(n_grp, 1, S5_FLAT)
    yg = _s5_core(f_r, f_m, mt, wc, ec, dp_re, dp_im, dsk, bsz)
    y_s5 = _s5_glu(yg, w_glu[0].astype(BF16), row(b_glu[0]), row(g_s5[0]), tm)

    wo = w_out[0].astype(BF16)
    out = _ssd_out_mlp(xa_r, bc_r, z_r, dt_r, s_meta, *ssd_args, x_rows, y_s5, wo[:d_ssd_w], wo[d_ssd_w:], row(g_mlp[0]),
                       w_up[0].astype(BF16), w_down[0].astype(BF16), row(g_final), tm, seq // tm, 1024)
    return out.reshape(bsz, seq, d_model).astype(x.dtype)
```

```python
import functools

import jax
import jax.numpy as jnp
from jax import lax
from jax.experimental import pallas as pl
from jax.experimental.pallas import tpu as pltpu

F32 = jnp.float32
BF16 = jnp.bfloat16

NORM_EPS = 1e-5
N_META = 16
SSD_CHUNK = 256
SSD_HEAD_DIM = 64
SSD_GROUPS = 2
SSD_STATE = 128
SSD_CONV = 4
S5_WIDTH = 16
S5_STATE = 64
S5_BLOCK = 16
S5_FLAT = S5_BLOCK * S5_WIDTH
S5_PREP_GROUPS = 8
PLANE = SSD_CHUNK // S5_BLOCK
LANES = 128
SUBLANES = 8
LANE_BLOCKS = LANES // S5_WIDTH
COL_TILE = 256
VMEM_LIMIT_BYTES = 58 * 1024 * 1024

assert PLANE == S5_BLOCK == 2 * SUBLANES
assert LANES == 2 * SSD_HEAD_DIM


def _rmsnorm(x, g):
    return x * lax.rsqrt(jnp.mean(x * x, axis=-1, keepdims=True) + NORM_EPS) * g


def _silu(x):
    return x * jax.nn.sigmoid(x)


def _gelu(x):
    return 0.5 * x * (1.0 + lax.erf(x * (2.0 ** -0.5)))


def _dot(a, b):
    return jnp.dot(a, b, preferred_element_type=F32)


def _dot_nt(a, b, **kw):
    return lax.dot_general(a, b, (((1,), (1,)), ((), ())), preferred_element_type=F32, **kw)


def _dot_tn(a, b):
    return lax.dot_general(a, b, (((0,), (0,)), ((), ())), preferred_element_type=F32)


def _dot_split(a, m01, parts):
    acc = None
    rem = a
    for _ in range(parts):
        piece = rem.astype(BF16)
        rem = rem - piece.astype(F32)
        term = _dot(piece, m01)
        acc = term if acc is None else acc + term
    return acc


def _params(sem):
    return pltpu.CompilerParams(dimension_semantics=sem, vmem_limit_bytes=VMEM_LIMIT_BYTES)


def _const_spec(shape, single_buffer=False):
    nd = len(shape)
    mode = pl.Buffered(1) if single_buffer else None
    return pl.BlockSpec(shape, lambda *_: (0,) * nd, pipeline_mode=mode)


def _token_of_row(r):
    return (r % PLANE) * S5_BLOCK + r // PLANE


def _transpose8(arrays, pos, axis, unit):
    a = list(arrays)
    size = 8 * unit
    d = 4
    while d:
        keep_lo = (pos & (d * unit)) == 0
        for i in range(8):
            if i & d:
                continue
            lo, hi = a[i], a[i + d]
            a[i] = jnp.where(keep_lo, lo, pltpu.roll(hi, d * unit, axis=axis))
            a[i + d] = jnp.where(keep_lo, pltpu.roll(lo, size - d * unit, axis=axis), hi)
        d //= 2
    return a


def _transpose_lane_blocks(planes, lane):
    return _transpose8(planes, lane, 1, S5_WIDTH)


def _swap_row_order(v):
    sub = lax.broadcasted_iota(jnp.int32, (SUBLANES, v.shape[1]), 0)
    chunks = []
    for chunk in range(v.shape[0] // SSD_CHUNK):
        out = [[None, None] for _ in range(S5_BLOCK)]
        for ah in range(2):
            for bh in range(2):
                first = [chunk * SSD_CHUNK + (SUBLANES * ah + al) * PLANE + SUBLANES * bh for al in range(SUBLANES)]
                src = [v[r:r + SUBLANES] for r in first]
                dst = _transpose8(src, sub, 0, 1)
                for bl in range(SUBLANES):
                    out[SUBLANES * bh + bl][ah] = dst[bl]
        chunks.append(jnp.concatenate([piece for pair in out for piece in pair], axis=0))
    return jnp.concatenate(chunks, axis=0) if len(chunks) > 1 else chunks[0]


def _in_proj_kernel(x_ref, hist_ref, g_ref, wz_ref, wxbc_ref, wdt_ref, wu_ref, convw_ref, convb_ref,
                    z_ref, xa_ref, bc_ref, dt_ref, f_ref, tail_ref, hist_scr, *, tiles_per_seq):
    tm = z_ref.shape[0]
    d_x = xa_ref.shape[1]
    n_hist = (SSD_CONV - 1) * PLANE
    n_chunks = tm // SSD_CHUNK

    @pl.when(lax.rem(pl.program_id(0), tiles_per_seq) == 0)
    def _():
        hist_scr[...] = hist_ref[...]

    n = _rmsnorm(_swap_row_order(x_ref[...]), g_ref[...]).astype(BF16)

    lane = lax.broadcasted_iota(jnp.int32, (PLANE, LANES), 1)
    for ct in range(wu_ref.shape[1] // COL_TILE):
        u = _dot(n, wu_ref[:, ct * COL_TILE:(ct + 1) * COL_TILE])
        for chunk in range(n_chunks):
            for o in range(COL_TILE // LANES):
                oct_ = ct * (COL_TILE // LANES) + o
                for half in range(S5_BLOCK // LANE_BLOCKS):
                    tok = []
                    for s in range(half * LANE_BLOCKS, (half + 1) * LANE_BLOCKS):
                        lo = chunk * SSD_CHUNK + s * PLANE
                        tok.append(u[lo:lo + PLANE, o * LANES:(o + 1) * LANES])
                    grp = _transpose_lane_blocks(tok, lane)
                    for gi in range(LANE_BLOCKS):
                        f_ref[oct_ * LANE_BLOCKS + gi, chunk * PLANE:(chunk + 1) * PLANE,
                              half * LANES:(half + 1) * LANES] = grp[gi].astype(BF16)

    row0 = lax.broadcasted_iota(jnp.int32, (PLANE, 1), 0) == 0
    for ct in range(wxbc_ref.shape[1] // COL_TILE):
        cols = slice(ct * COL_TILE, (ct + 1) * COL_TILE)
        pre = _dot(n, wxbc_ref[:, cols])
        w = [convw_ref[k:k + 1, cols] for k in range(SSD_CONV)]
        prev = hist_scr[:, cols]
        for chunk in range(n_chunks):
            cur = pre[chunk * SSD_CHUNK:(chunk + 1) * SSD_CHUNK]
            last = cur[SSD_CHUNK - n_hist:]
            wrapped = [jnp.where(row0, pltpu.roll(prev[m * PLANE:(m + 1) * PLANE], 1, axis=0),
                                 pltpu.roll(last[m * PLANE:(m + 1) * PLANE], 1, axis=0)) for m in range(SSD_CONV - 1)]
            acc = convb_ref[:, cols] + w[SSD_CONV - 1] * cur
            for k in range(1, SSD_CONV):
                shifted = jnp.concatenate(wrapped[SSD_CONV - 1 - k:] + [cur[:SSD_CHUNK - k * PLANE]], axis=0)
                acc = acc + w[SSD_CONV - 1 - k] * shifted
            act = _silu(acc)
            rows = slice(chunk * SSD_CHUNK, (chunk + 1) * SSD_CHUNK)
            if ct * COL_TILE < d_x:
                xa_ref[rows, cols] = act
            else:
                bc_ref[rows, ct * COL_TILE - d_x:(ct + 1) * COL_TILE - d_x] = act.astype(BF16)
            prev = last
        hist_scr[:, cols] = prev
        tail_ref[:, cols] = prev

    for ct in range(wz_ref.shape[1] // COL_TILE):
        cols = slice(ct * COL_TILE, (ct + 1) * COL_TILE)
        z_ref[:, cols] = _dot(n, wz_ref[:, cols])
    dt_ref[...] = _dot(n, wdt_ref[...])


def _in_proj(x_rows, hist, g, wz, wxbc, wdt, wu, convw, convb, tm, tiles_per_seq):
    rows, d = x_rows.shape
    d_z, d_xbc, d_u = wz.shape[1], wxbc.shape[1], wu.shape[1]
    d_bc = d_xbc - d_z
    n_grp = d_u // S5_WIDTH
    nb = tm // S5_BLOCK
    row_spec = lambda w: pl.BlockSpec((tm, w), lambda i: (i, 0))
    return pl.pallas_call(
        functools.partial(_in_proj_kernel, tiles_per_seq=tiles_per_seq),
        grid=(rows // tm,),
        in_specs=[row_spec(d)]
        + [_const_spec(a.shape, single_buffer=True) for a in (hist, g, wz, wxbc, wdt, wu, convw, convb)],
        out_specs=[row_spec(d_z), row_spec(d_z), row_spec(d_bc), row_spec(LANES),
                   pl.BlockSpec((n_grp, nb, S5_FLAT), lambda i: (0, i, 0)), _const_spec(hist.shape)],
        out_shape=[jax.ShapeDtypeStruct((rows, d_z), F32), jax.ShapeDtypeStruct((rows, d_z), F32),
                   jax.ShapeDtypeStruct((rows, d_bc), BF16), jax.ShapeDtypeStruct((rows, LANES), F32),
                   jax.ShapeDtypeStruct((n_grp, rows // S5_BLOCK, S5_FLAT), BF16),
                   jax.ShapeDtypeStruct(hist.shape, F32)],
        scratch_shapes=[pltpu.VMEM(hist.shape, F32)],
        compiler_params=_params(("arbitrary",)),
        name="in_proj",
    )(x_rows, hist, g, wz, wxbc, wdt, wu, convw, convb)


def _ssd_chunk(xa_ref, bc_ref, z_ref, dt_ref, rows, dtb, alog, dskip_ref, g_ref, expand_ref, st_ref, yd_ref,
               padded_rows):
    q = SSD_CHUNK
    d_ssd = xa_ref.shape[-1]
    gw = d_ssd // SSD_GROUPS
    b_in = bc_ref[rows, :SSD_GROUPS * SSD_STATE]
    c_in = bc_ref[rows, SSD_GROUPS * SSD_STATE:]

    dt = jax.nn.softplus(dt_ref[rows, :] + dtb)
    if padded_rows:
        t_row = _token_of_row(lax.broadcasted_iota(jnp.int32, dt.shape, 0))
        dt = jnp.where(t_row < padded_rows, 0.0, dt)
    d_a = dt * (-jnp.exp(alog))
    t_r = _token_of_row(lax.broadcasted_iota(jnp.int32, (q, q), 0))
    t_c = _token_of_row(lax.broadcasted_iota(jnp.int32, (q, q), 1))
    causal = t_r >= t_c
    a_cs_t = _dot_split(d_a.T, jnp.where(t_c >= t_r, 1.0, 0.0).astype(BF16), 3)
    a_cs = a_cs_t.T

    first_head = lax.broadcasted_iota(jnp.int32, (q, LANES), 1) < SSD_HEAD_DIM
    cb = None
    ssq = jnp.zeros((q, 1), F32)
    for t in range(d_ssd // COL_TILE):
        cols = slice(t * COL_TILE, (t + 1) * COL_TILE)
        g = (t * COL_TILE) // gw
        gcols = slice(t * COL_TILE - g * gw, (t + 1) * COL_TILE - g * gw)
        cg = c_in[:, g * SSD_STATE:(g + 1) * SSD_STATE]
        bg = b_in[:, g * SSD_STATE:(g + 1) * SSD_STATE]
        if (t * COL_TILE) % gw == 0:
            cb = _dot_nt(cg, bg)
        expand = expand_ref[:, cols]
        dt_x = _dot_split(dt, expand, 2)
        a_x = _dot_split(a_cs, expand, 2)
        a_last = a_x[q - 1:q, :]
        x_in = xa_ref[rows, cols]
        xdt = x_in * dt_x
        xdt_b = xdt.astype(BF16)
        xw_b = (xdt * jnp.exp(a_last - a_x)).astype(BF16)
        s_prev = st_ref[g, :, gcols]
        y_off = _dot(cg, s_prev.astype(BF16)) * jnp.exp(a_x)
        pairs = []
        for pair in range(COL_TILE // LANES):
            lo = pair * LANES
            att = []
            for h in (2 * (t * (COL_TILE // LANES) + pair), 2 * (t * (COL_TILE // LANES) + pair) + 1):
                seg = a_cs[:, h:h + 1] - a_cs_t[h:h + 1, :]
                att.append((cb * jnp.exp(jnp.where(causal, seg, -jnp.inf))).astype(BF16))
            x_pair = xdt_b[:, lo:lo + LANES]
            zero = jnp.zeros_like(x_pair)
            rhs = jnp.concatenate([jnp.where(first_head, x_pair, zero), jnp.where(first_head, zero, x_pair)], axis=0)
            pairs.append(_dot(jnp.concatenate(att, axis=1), rhs) + y_off[:, lo:lo + LANES])
        y = jnp.concatenate(pairs, axis=1) + x_in * dskip_ref[:, cols]
        y = y * _silu(z_ref[rows, cols])
        yd_ref[:, cols] = y
        ssq = ssq + jnp.sum(y * y, axis=-1, keepdims=True)
        st_ref[g, :, gcols] = s_prev * jnp.exp(a_last) + _dot_tn(bg, xw_b)

    scale = lax.rsqrt(ssq * (1.0 / d_ssd) + NORM_EPS)
    return (yd_ref[...] * scale * g_ref[...]).astype(BF16)


def _ssd_kernel(xa_ref, bc_ref, z_ref, dt_ref, s0_ref, dtb_ref, alog_ref, dskip_ref, g_ref, expand_ref,
                y_ref, s_out_ref, st_ref, yd_ref, *, padded_rows):
    @pl.when(pl.program_id(1) == 0)
    def _():
        st_ref[...] = s0_ref[...]

    for c in range(xa_ref.shape[1] // SSD_CHUNK):
        rows = slice(c * SSD_CHUNK, (c + 1) * SSD_CHUNK)
        y_ref[0, rows, :] = _ssd_chunk(xa_ref.at[0], bc_ref.at[0], z_ref.at[0], dt_ref.at[0], rows, dtb_ref[...],
                                       alog_ref[...], dskip_ref, g_ref, expand_ref, st_ref, yd_ref, padded_rows)
    s_out_ref[...] = st_ref[...]


def _ssd(xa, bc, z, dt, s0, dtb, alog, dskip, g, expand, padded_rows, chunks_per_step):
    bsz, seq, d_ssd = xa.shape
    rows = SSD_CHUNK * chunks_per_step
    chunk = lambda w: pl.BlockSpec((1, rows, w), lambda b, c: (b, c, 0))
    return pl.pallas_call(
        functools.partial(_ssd_kernel, padded_rows=padded_rows),
        grid=(bsz, seq // rows),
        in_specs=[chunk(d_ssd), chunk(bc.shape[-1]), chunk(d_ssd), chunk(LANES)]
        + [_const_spec(a.shape) for a in (s0, dtb, alog, dskip, g, expand)],
        out_specs=[chunk(d_ssd), _const_spec(s0.shape)],
        out_shape=[jax.ShapeDtypeStruct((bsz, seq, d_ssd), BF16), jax.ShapeDtypeStruct(s0.shape, F32)],
        scratch_shapes=[pltpu.VMEM(s0.shape, F32), pltpu.VMEM((SSD_CHUNK, d_ssd), F32)],
        compiler_params=_params(("arbitrary", "arbitrary")),
        name="ssd",
    )(xa, bc, z, dt, s0, dtb, alog, dskip, g, expand)


def _lam_pow(lr, li, step, k):
    kf = k.astype(F32)
    mag = jnp.exp(kf * (lr * step))
    ang = kf * (li * step)
    return mag * jnp.cos(ang), mag * jnp.sin(ang)


def _s5_prep_kernel(lr_ref, li_ref, ls_ref, bt_re_ref, bt_im_ref, c_re_ref, c_im_ref,
                    mt_ref, wc_ref, ec_ref, dp_re_ref, dp_im_ref):
    t, hw, p = S5_BLOCK, S5_WIDTH, S5_STATE
    lr, li = lr_ref[0], li_ref[0]
    step = jnp.exp(ls_ref[0])
    k_up = lax.broadcasted_iota(jnp.int32, (3 * SUBLANES, 1), 0)
    pw_re, pw_im = _lam_pow(lr, li, step, k_up)
    k_dn = (t - 1) - lax.broadcasted_iota(jnp.int32, (t, 1), 0)
    pr_re, pr_im = _lam_pow(lr, li, step, k_dn)
    k_blk = t * lax.broadcasted_iota(jnp.int32, (2 * SUBLANES, 1), 0)
    dp_re, dp_im = _lam_pow(lr, li, step, k_blk)

    ab_re, ab_im = pw_re[1:2, :], pw_im[1:2, :]
    den = lr * lr + li * li
    coef_re = ((ab_re - 1.0) * lr + ab_im * li) / den
    coef_im = (ab_im * lr - (ab_re - 1.0) * li) / den
    bt_re, bt_im = bt_re_ref[0], bt_im_ref[0]
    bb_re = coef_re * bt_re - coef_im * bt_im
    bb_im = coef_re * bt_im + coef_im * bt_re
    c_re, c_im = c_re_ref[0], c_im_ref[0]

    def rep(m):
        return jnp.broadcast_to(m[:, None, :], (t, hw, p)).reshape(t * hw, p)

    def tile(m):
        return jnp.concatenate([m] * t, axis=0)

    hi = lax.Precision.HIGHEST
    lane = lax.broadcasted_iota(jnp.int32, (hw, t * hw), 1)
    for a in range(S5_PREP_GROUPS):
        sl = slice(a * p, (a + 1) * p)
        ct_re, ct_im = tile(c_re[:, sl]), tile(c_im[:, sl])
        l0_re, l0_im = rep(pw_re[0:t, sl]), rep(pw_im[0:t, sl])
        l1_re, l1_im = rep(pw_re[1:t + 1, sl]), rep(pw_im[1:t + 1, sl])
        x0_re = l0_re * ct_re - l0_im * ct_im
        x0_im = l0_re * ct_im + l0_im * ct_re
        kt = _dot_nt(bb_re[:, sl], x0_re, precision=hi) - _dot_nt(bb_im[:, sl], x0_im, precision=hi)
        for s in range(t):
            blk = kt if s == 0 else jnp.where(lane >= s * hw, pltpu.roll(kt, s * hw, axis=1), 0.0)
            mt_ref[a, s * hw:(s + 1) * hw, :] = blk.astype(mt_ref.dtype)
        x1_re = l1_re * ct_re - l1_im * ct_im
        x1_im = l1_re * ct_im + l1_im * ct_re
        ec_ref[a] = jnp.concatenate([x1_re, -x1_im], axis=1).T.astype(ec_ref.dtype)
        lw_re, lw_im = rep(pr_re[:, sl]), rep(pr_im[:, sl])
        bt_re_a, bt_im_a = tile(bb_re[:, sl]), tile(bb_im[:, sl])
        wc_ref[a] = jnp.concatenate([lw_re * bt_re_a - lw_im * bt_im_a, lw_re * bt_im_a + lw_im * bt_re_a],
                                    axis=1).astype(wc_ref.dtype)
        dp_re_ref[a] = jnp.concatenate([dp_re[:, sl]] * 2, axis=1)
        dp_im_ref[a] = jnp.concatenate([dp_im[:, sl]] * 2, axis=1)


def _s5_prep(lr, li, ls, bt_re, bt_im, c_re, c_im):
    steps = lr.shape[0]
    gs, p, fl = S5_PREP_GROUPS, S5_STATE, S5_FLAT
    n_grp = steps * gs
    in_spec = lambda a: pl.BlockSpec((1,) + a.shape[1:], lambda i: (i, 0, 0))
    out_spec = lambda *s: pl.BlockSpec((gs,) + s, lambda i: (i, 0, 0))
    out = lambda *s, dt=BF16: jax.ShapeDtypeStruct((n_grp,) + s, dt)
    args = (lr, li, ls, bt_re, bt_im, c_re, c_im)
    return pl.pallas_call(
        _s5_prep_kernel,
        grid=(steps,),
        in_specs=[in_spec(a) for a in args],
        out_specs=[out_spec(fl, fl), out_spec(fl, 2 * p), out_spec(2 * p, fl), out_spec(2 * SUBLANES, 2 * p),
                   out_spec(2 * SUBLANES, 2 * p)],
        out_shape=[out(fl, fl), out(fl, 2 * p), out(2 * p, fl), out(2 * SUBLANES, 2 * p, dt=F32),
                   out(2 * SUBLANES, 2 * p, dt=F32)],
        compiler_params=_params(("parallel",)),
        name="s5_prep",
    )(*args)


def _s5_core_kernel(f_r_ref, f_m_ref, mt_ref, wc_ref, ec_ref, dp_re_ref, dp_im_ref, dskip_ref,
                    y_ref, wu_re_ref, wu_im_ref, s_re_ref, s_im_ref, *, n_batch):
    p = S5_STATE
    rows_m = f_m_ref.shape[1]
    rows_r = f_r_ref.shape[1] // n_batch
    per_batch = rows_m + rows_r
    n_tiles = per_batch // SUBLANES
    lane_lo = lax.broadcasted_iota(jnp.int32, (1, 2 * p), 1) < p

    def pack(a, b):
        return jnp.where(lane_lo, a, pltpu.roll(b, p, axis=1)), jnp.where(lane_lo, pltpu.roll(a, p, axis=1), b)

    wu_m = [_dot(f_m_ref[a], wc_ref[a]) for a in range(2)]
    wu_r = [_dot(f_r_ref[a], wc_ref[a]) for a in range(2)]
    m_re, m_im = pack(*wu_m)
    r_re, r_im = pack(*wu_r)
    for b in range(n_batch):
        wu_re_ref[b * per_batch:b * per_batch + rows_m, :] = m_re
        wu_im_ref[b * per_batch:b * per_batch + rows_m, :] = m_im
        wu_re_ref[b * per_batch + rows_m:(b + 1) * per_batch, :] = r_re[b * rows_r:(b + 1) * rows_r]
        wu_im_ref[b * per_batch + rows_m:(b + 1) * per_batch, :] = r_im[b * rows_r:(b + 1) * rows_r]

    dp_re = jnp.where(lane_lo, dp_re_ref[0], dp_re_ref[1])
    dp_im = jnp.where(lane_lo, dp_im_ref[0], dp_im_ref[1])
    sub = lax.broadcasted_iota(jnp.int32, (SUBLANES, 2 * p), 0)

    def cmul_add(a_re, a_im, d_re, d_im, x_re, x_im):
        return a_re + d_re * x_re - d_im * x_im, a_im + d_re * x_im + d_im * x_re

    def shift_down(x, k):
        return jnp.where(sub >= k, pltpu.roll(x, k, axis=0), 0.0)

    def tile_step(i, carry):
        new = []
        for b in range(n_batch):
            c_re, c_im = carry[2 * b], carry[2 * b + 1]
            base = pl.multiple_of(b * per_batch + i * SUBLANES, SUBLANES)
            v_re = wu_re_ref[pl.ds(base, SUBLANES), :]
            v_im = wu_im_ref[pl.ds(base, SUBLANES), :]
            for lvl in range(3):
                k = 1 << lvl
                v_re, v_im = cmul_add(v_re, v_im, dp_re[k:k + 1, :], dp_im[k:k + 1, :],
                                      shift_down(v_re, k), shift_down(v_im, k))
            e_re, e_im = cmul_add(shift_down(v_re, 1), shift_down(v_im, 1), dp_re[0:SUBLANES, :],
                                  dp_im[0:SUBLANES, :], c_re, c_im)
            s_re_ref[pl.ds(base, SUBLANES), :] = e_re
            s_im_ref[pl.ds(base, SUBLANES), :] = e_im
            n_re, n_im = cmul_add(v_re[SUBLANES - 1:, :], v_im[SUBLANES - 1:, :], dp_re[SUBLANES:SUBLANES + 1, :],
                                  dp_im[SUBLANES:SUBLANES + 1, :], c_re, c_im)
            new += [n_re, n_im]
        return tuple(new)

    zero = jnp.zeros((1, 2 * p), F32)
    lax.fori_loop(0, n_tiles, tile_step, (zero,) * (2 * n_batch))

    for b in range(n_batch):
        lo = b * per_batch + rows_m
        s_a, s_b = pack(s_re_ref[lo:lo + rows_r, :], s_im_ref[lo:lo + rows_r, :])
        for a, s_ab in enumerate((s_a, s_b)):
            f = f_r_ref[a, b * rows_r:(b + 1) * rows_r, :]
            y = _dot(f, mt_ref[a]) + _dot(s_ab.astype(BF16), ec_ref[a]) + f.astype(F32) * dskip_ref[a]
            y_ref[a, b * rows_r:(b + 1) * rows_r, :] = _gelu(y).astype(y_ref.dtype)


def _s5_core(f_r, f_m, mt, wc, ec, dp_re, dp_im, dskip, n_batch):
    n_grp, rows_r, fl = f_r.shape
    rows_m = f_m.shape[1]
    p2 = 2 * S5_STATE
    rows = n_batch * rows_m + rows_r
    spec = lambda *s: pl.BlockSpec((2,) + s, lambda i: (i, 0, 0))
    return pl.pallas_call(
        functools.partial(_s5_core_kernel, n_batch=n_batch),
        grid=(n_grp // 2,),
        in_specs=[spec(rows_r, fl), spec(rows_m, fl), spec(fl, fl), spec(fl, p2), spec(p2, fl),
                  spec(2 * SUBLANES, p2), spec(2 * SUBLANES, p2), spec(1, fl)],
        out_specs=spec(rows_r, fl),
        out_shape=jax.ShapeDtypeStruct((n_grp, rows_r, fl), BF16),
        scratch_shapes=[pltpu.VMEM((rows, p2), F32)] * 4,
        compiler_params=_params(("parallel",)),
        name="s5_core",
    )(f_r, f_m, mt, wc, ec, dp_re, dp_im, dskip)


def _s5_glu_kernel(yg_ref, w_ref, b_ref, g_ref, o_ref):
    d = o_ref.shape[-1]
    tm = o_ref.shape[0]
    lane = lax.broadcasted_iota(jnp.int32, (PLANE, LANES), 1)
    chunks = []
    for chunk in range(tm // SSD_CHUNK):
        planes = [[None] * (d // LANES) for _ in range(S5_BLOCK)]
        for oct_ in range(d // LANES):
            for half in range(S5_BLOCK // LANE_BLOCKS):
                grp = [yg_ref[oct_ * LANE_BLOCKS + gi, chunk * PLANE:(chunk + 1) * PLANE,
                              half * LANES:(half + 1) * LANES].astype(F32) for gi in range(LANE_BLOCKS)]
                tok = _transpose_lane_blocks(grp, lane)
                for s8 in range(LANE_BLOCKS):
                    planes[half * LANE_BLOCKS + s8][oct_] = tok[s8].astype(BF16)
        chunks.append(jnp.concatenate([jnp.concatenate(pieces, axis=1) for pieces in planes], axis=0))
    y = jnp.concatenate(chunks, axis=0) if len(chunks) > 1 else chunks[0]
    v = _dot(y, w_ref[...]) + b_ref[...]
    o = v[:, :d] * jax.nn.sigmoid(v[:, d:])
    o_ref[...] = _rmsnorm(o, g_ref[...]).astype(o_ref.dtype)


def _s5_glu(yg, w, b, g, tm):
    n_grp, blocks, fl = yg.shape
    rows, d = blocks * S5_BLOCK, n_grp * S5_WIDTH
    nb = tm // S5_BLOCK
    return pl.pallas_call(
        _s5_glu_kernel,
        grid=(rows // tm,),
        in_specs=[pl.BlockSpec((n_grp, nb, fl), lambda i: (0, i, 0))]
        + [_const_spec(a.shape, single_buffer=True) for a in (w, b, g)],
        out_specs=pl.BlockSpec((tm, d), lambda i: (i, 0)),
        out_shape=jax.ShapeDtypeStruct((rows, d), BF16),
        compiler_params=_params(("parallel",)),
        name="s5_glu",
    )(yg, w, b, g)


def _out_mlp_kernel(x_ref, ya_ref, yb_ref, woa_ref, wob_ref, gm_ref, wup_ref, wdn_ref, gf_ref, o_ref, *, ff_tile):
    h = x_ref[...] + _swap_row_order(_dot(ya_ref[...], woa_ref[...]) + _dot(yb_ref[...], wob_ref[...]))
    n = _rmsnorm(h, gm_ref[...]).astype(BF16)
    d_ff = wup_ref.shape[1]
    mlp = None
    for k in range(d_ff // ff_tile):
        m = _dot(n, wup_ref[:, k * ff_tile:(k + 1) * ff_tile])
        m = jnp.square(jnp.maximum(m, 0.0)).astype(BF16)
        part = _dot(m, wdn_ref[k * ff_tile:(k + 1) * ff_tile, :])
        mlp = part if mlp is None else mlp + part
    o_ref[...] = _rmsnorm(h + mlp, gf_ref[...])


def _out_mlp(x_rows, ya, yb, woa, wob, gm, wup, wdn, gf, tm, ff_tile):
    rows, d = ya.shape
    row_spec = pl.BlockSpec((tm, d), lambda i: (i, 0))
    return pl.pallas_call(
        functools.partial(_out_mlp_kernel, ff_tile=ff_tile),
        grid=(rows // tm,),
        in_specs=[row_spec, row_spec, row_spec]
        + [_const_spec(a.shape, single_buffer=True) for a in (woa, wob, gm, wup, wdn, gf)],
        out_specs=row_spec,
        out_shape=jax.ShapeDtypeStruct((rows, d), F32),
        compiler_params=_params(("parallel",)),
        name="out_mlp",
    )(x_rows, ya, yb, woa, wob, gm, wup, wdn, gf)


def _row_tile(rows, want):
    tm = min(rows, want)
    assert rows % tm == 0
    return tm


def kernel(x, meta_tokens, g_mix, w_in, conv_w, conv_b, dt_bias, a_log, d_ssd, g_ssd, lam_re, lam_im, log_step,
           b_re, b_im, c_re, c_im, d_s5, w_glu, b_glu, g_s5, w_out, g_mlp, w_up, w_down, g_final):
    bsz, seq, d_model = x.shape
    assert w_in.shape[0] == 1, "single-layer block: meta-token outputs are only consumed as state"
    assert seq % SSD_CHUNK == 0 and d_model % LANES == 0
    n_heads = dt_bias.shape[-1]
    d_ssd_w = n_heads * SSD_HEAD_DIM
    d_xbc = conv_w.shape[-1]
    n_grp = d_s5.shape[-2]
    o_dt = d_ssd_w + d_xbc
    o_u = o_dt + n_heads
    row = lambda v: v.reshape(1, -1).astype(F32)

    wi = w_in[0]
    wz = wi[:, :d_ssd_w].astype(BF16)
    wxbc = wi[:, d_ssd_w:o_dt].astype(BF16)
    wdt = jnp.pad(wi[:, o_dt:o_u], ((0, 0), (0, LANES - n_heads))).astype(BF16)
    wu = wi[:, o_u:].astype(BF16)
    pad_heads = lambda v: jnp.pad(row(v), ((0, 0), (0, LANES - n_heads)))
    convw, convb, gmix = conv_w[0].astype(F32), row(conv_b[0]), row(g_mix[0])

    x_rows = x.reshape(bsz * seq, d_model)
    meta_chunk = jnp.concatenate([jnp.zeros((SSD_CHUNK - N_META, d_model), x.dtype), meta_tokens.astype(x.dtype)], axis=0)
    tm = _row_tile(seq, 1024)
    no_hist = jnp.zeros(((SSD_CONV - 1) * PLANE, d_xbc), F32)
    z_m, xa_m, bc_m, dt_m, f_m, tail_m = _in_proj(meta_chunk, no_hist, gmix, wz, wxbc, wdt, wu,
                                                  convw, convb, SSD_CHUNK, 1)
    z_r, xa_r, bc_r, dt_r, f_r, _ = _in_proj(x_rows, tail_m, gmix, wz, wxbc, wdt, wu, convw, convb, tm, seq // tm)

    seq3 = lambda a, b: a.reshape(b, -1, a.shape[-1])
    expand = (jnp.arange(d_ssd_w)[None, :] // SSD_HEAD_DIM == jnp.arange(LANES)[:, None]).astype(BF16)
    ssd_args = (pad_heads(dt_bias[0]), pad_heads(a_log[0]), row(jnp.repeat(d_ssd[0], SSD_HEAD_DIM)), row(g_ssd[0]), expand)
    no_state = jnp.zeros((SSD_GROUPS, SSD_STATE, d_ssd_w // SSD_GROUPS), F32)
    _, s_meta = _ssd(seq3(xa_m, 1), seq3(bc_m, 1), seq3(z_m, 1), seq3(dt_m, 1), no_state, *ssd_args,
                     padded_rows=SSD_CHUNK - N_META, chunks_per_step=1)
    y_ssd, _ = _ssd(seq3(xa_r, bsz), seq3(bc_r, bsz), seq3(z_r, bsz), seq3(dt_r, bsz), s_meta, *ssd_args, padded_rows=0,
                    chunks_per_step=2)

    t, gs, p = S5_BLOCK, S5_PREP_GROUPS, S5_STATE
    lanes = lambda v: v.astype(F32).reshape(n_grp // gs, 1, gs * p)
    per_grp = lambda v: (v.astype(F32).reshape(n_grp // gs, gs, S5_WIDTH, p).transpose(0, 2, 1, 3)
                         .reshape(n_grp // gs, S5_WIDTH, gs * p))
    mt, wc, ec, dp_re, dp_im = _s5_prep(lanes(lam_re[0]), lanes(lam_im[0]), lanes(jnp.repeat(log_step[0], p)),
                                        per_grp(jnp.swapaxes(b_re[0], 1, 2)), per_grp(jnp.swapaxes(b_im[0], 1, 2)),
                                        per_grp(c_re[0]), per_grp(c_im[0]))
    dsk = jnp.tile(d_s5[0].astype(F32), (1, t)).reshape(n_grp, 1, S5_FLAT)
    yg = _s5_core(f_r, f_m, mt, wc, ec, dp_re, dp_im, dsk, bsz)
    y_s5 = _s5_glu(yg, w_glu[0].astype(BF16), row(b_glu[0]), row(g_s5[0]), tm)

    wo = w_out[0].astype(BF16)
    out = _out_mlp(x_rows, y_ssd.reshape(bsz * seq, d_ssd_w), y_s5, wo[:d_ssd_w], wo[d_ssd_w:], row(g_mlp[0]),
                   w_up[0].astype(BF16), w_down[0].astype(BF16), row(g_final), tm, 1024)
    return out.reshape(bsz, seq, d_model).astype(x.dtype)
```

```python
import functools

import jax
import jax.numpy as jnp
from jax import lax
from jax.experimental import pallas as pl
from jax.experimental.pallas import tpu as pltpu

F32 = jnp.float32
BF16 = jnp.bfloat16

NORM_EPS = 1e-5
N_META = 16
SSD_CHUNK = 256
SSD_SUB = 128
SSD_HEAD_DIM = 64
SSD_GROUPS = 2
SSD_STATE = 128
SSD_CONV = 4
S5_WIDTH = 16
S5_STATE = 64
S5_BLOCK = 16
S5_FLAT = S5_BLOCK * S5_WIDTH
S5_PREP_GROUPS = 8
PLANE = SSD_CHUNK // S5_BLOCK
LANES = 128
SUBLANES = 8
LANE_BLOCKS = LANES // S5_WIDTH
COL_TILE = 256
VMEM_LIMIT_BYTES = 58 * 1024 * 1024

assert PLANE == S5_BLOCK == 2 * SUBLANES
assert LANES == 2 * SSD_HEAD_DIM


def _rmsnorm(x, g):
    return x * lax.rsqrt(jnp.mean(x * x, axis=-1, keepdims=True) + NORM_EPS) * g


def _silu(x):
    return x * jax.nn.sigmoid(x)


def _gelu(x):
    return 0.5 * x * (1.0 + lax.erf(x * (2.0 ** -0.5)))


def _dot(a, b):
    return jnp.dot(a, b, preferred_element_type=F32)


def _dot_nt(a, b, **kw):
    return lax.dot_general(a, b, (((1,), (1,)), ((), ())), preferred_element_type=F32, **kw)


def _dot_tn(a, b):
    return lax.dot_general(a, b, (((0,), (0,)), ((), ())), preferred_element_type=F32)


def _dot_split(a, m01, parts):
    acc = None
    rem = a
    for _ in range(parts):
        piece = rem.astype(BF16)
        rem = rem - piece.astype(F32)
        term = _dot(piece, m01)
        acc = term if acc is None else acc + term
    return acc


def _params(sem):
    return pltpu.CompilerParams(dimension_semantics=sem, vmem_limit_bytes=VMEM_LIMIT_BYTES)


def _const_spec(shape, single_buffer=False):
    nd = len(shape)
    mode = pl.Buffered(1) if single_buffer else None
    return pl.BlockSpec(shape, lambda *_: (0,) * nd, pipeline_mode=mode)


def _token_of_row(r):
    return (r % PLANE) * S5_BLOCK + r // PLANE


def _transpose8(arrays, pos, axis, unit):
    a = list(arrays)
    size = 8 * unit
    d = 4
    while d:
        keep_lo = (pos & (d * unit)) == 0
        for i in range(8):
            if i & d:
                continue
            lo, hi = a[i], a[i + d]
            a[i] = jnp.where(keep_lo, lo, pltpu.roll(hi, d * unit, axis=axis))
            a[i + d] = jnp.where(keep_lo, pltpu.roll(lo, size - d * unit, axis=axis), hi)
        d //= 2
    return a


def _transpose_lane_blocks(planes, lane):
    return _transpose8(planes, lane, 1, S5_WIDTH)


def _swap_row_order(v):
    sub = lax.broadcasted_iota(jnp.int32, (SUBLANES, v.shape[1]), 0)
    chunks = []
    for chunk in range(v.shape[0] // SSD_CHUNK):
        out = [[None, None] for _ in range(S5_BLOCK)]
        for ah in range(2):
            for bh in range(2):
                first = [chunk * SSD_CHUNK + (SUBLANES * ah + al) * PLANE + SUBLANES * bh for al in range(SUBLANES)]
                src = [v[r:r + SUBLANES] for r in first]
                dst = _transpose8(src, sub, 0, 1)
                for bl in range(SUBLANES):
                    out[SUBLANES * bh + bl][ah] = dst[bl]
        chunks.append(jnp.concatenate([piece for pair in out for piece in pair], axis=0))
    return jnp.concatenate(chunks, axis=0) if len(chunks) > 1 else chunks[0]


def _in_proj_kernel(x_ref, hist_ref, g_ref, wz_ref, wxbc_ref, wdt_ref, wu_ref, convw_ref, convb_ref,
                    z_ref, xa_ref, bc_ref, dt_ref, f_ref, tail_ref, hist_scr, *, tiles_per_seq):
    tm = z_ref.shape[0]
    d_x = xa_ref.shape[1]
    n_hist = (SSD_CONV - 1) * PLANE
    n_chunks = tm // SSD_CHUNK

    @pl.when(lax.rem(pl.program_id(0), tiles_per_seq) == 0)
    def _():
        hist_scr[...] = hist_ref[...]

    n = _rmsnorm(_swap_row_order(x_ref[...]), g_ref[...]).astype(BF16)

    lane = lax.broadcasted_iota(jnp.int32, (PLANE, LANES), 1)
    for ct in range(wu_ref.shape[1] // COL_TILE):
        u = _dot(n, wu_ref[:, ct * COL_TILE:(ct + 1) * COL_TILE])
        for chunk in range(n_chunks):
            for o in range(COL_TILE // LANES):
                oct_ = ct * (COL_TILE // LANES) + o
                for half in range(S5_BLOCK // LANE_BLOCKS):
                    tok = []
                    for s in range(half * LANE_BLOCKS, (half + 1) * LANE_BLOCKS):
                        lo = chunk * SSD_CHUNK + s * PLANE
                        tok.append(u[lo:lo + PLANE, o * LANES:(o + 1) * LANES])
                    grp = _transpose_lane_blocks(tok, lane)
                    for gi in range(LANE_BLOCKS):
                        f_ref[oct_ * LANE_BLOCKS + gi, chunk * PLANE:(chunk + 1) * PLANE,
                              half * LANES:(half + 1) * LANES] = grp[gi].astype(BF16)

    row0 = lax.broadcasted_iota(jnp.int32, (PLANE, 1), 0) == 0
    for ct in range(wxbc_ref.shape[1] // COL_TILE):
        cols = slice(ct * COL_TILE, (ct + 1) * COL_TILE)
        pre = _dot(n, wxbc_ref[:, cols])
        w = [convw_ref[k:k + 1, cols] for k in range(SSD_CONV)]
        prev = hist_scr[:, cols]
        for chunk in range(n_chunks):
            cur = pre[chunk * SSD_CHUNK:(chunk + 1) * SSD_CHUNK]
            last = cur[SSD_CHUNK - n_hist:]
            wrapped = [jnp.where(row0, pltpu.roll(prev[m * PLANE:(m + 1) * PLANE], 1, axis=0),
                                 pltpu.roll(last[m * PLANE:(m + 1) * PLANE], 1, axis=0)) for m in range(SSD_CONV - 1)]
            acc = convb_ref[:, cols] + w[SSD_CONV - 1] * cur
            for k in range(1, SSD_CONV):
                shifted = jnp.concatenate(wrapped[SSD_CONV - 1 - k:] + [cur[:SSD_CHUNK - k * PLANE]], axis=0)
                acc = acc + w[SSD_CONV - 1 - k] * shifted
            act = _silu(acc)
            rows = slice(chunk * SSD_CHUNK, (chunk + 1) * SSD_CHUNK)
            if ct * COL_TILE < d_x:
                xa_ref[rows, cols] = act
            else:
                bc_ref[rows, ct * COL_TILE - d_x:(ct + 1) * COL_TILE - d_x] = act.astype(BF16)
            prev = last
        hist_scr[:, cols] = prev
        tail_ref[:, cols] = prev

    for ct in range(wz_ref.shape[1] // COL_TILE):
        cols = slice(ct * COL_TILE, (ct + 1) * COL_TILE)
        z_ref[:, cols] = _dot(n, wz_ref[:, cols])
    dt_ref[...] = _dot(n, wdt_ref[...])


def _in_proj(x_rows, hist, g, wz, wxbc, wdt, wu, convw, convb, tm, tiles_per_seq):
    rows, d = x_rows.shape
    d_z, d_xbc, d_u = wz.shape[1], wxbc.shape[1], wu.shape[1]
    d_bc = d_xbc - d_z
    n_grp = d_u // S5_WIDTH
    nb = tm // S5_BLOCK
    row_spec = lambda w: pl.BlockSpec((tm, w), lambda i: (i, 0))
    return pl.pallas_call(
        functools.partial(_in_proj_kernel, tiles_per_seq=tiles_per_seq),
        grid=(rows // tm,),
        in_specs=[row_spec(d)]
        + [_const_spec(a.shape, single_buffer=True) for a in (hist, g, wz, wxbc, wdt, wu, convw, convb)],
        out_specs=[row_spec(d_z), row_spec(d_z), row_spec(d_bc), row_spec(LANES),
                   pl.BlockSpec((n_grp, nb, S5_FLAT), lambda i: (0, i, 0)), _const_spec(hist.shape)],
        out_shape=[jax.ShapeDtypeStruct((rows, d_z), F32), jax.ShapeDtypeStruct((rows, d_z), F32),
                   jax.ShapeDtypeStruct((rows, d_bc), BF16), jax.ShapeDtypeStruct((rows, LANES), F32),
                   jax.ShapeDtypeStruct((n_grp, rows // S5_BLOCK, S5_FLAT), BF16),
                   jax.ShapeDtypeStruct(hist.shape, F32)],
        scratch_shapes=[pltpu.VMEM(hist.shape, F32)],
        compiler_params=_params(("arbitrary",)),
        name="in_proj",
    )(x_rows, hist, g, wz, wxbc, wdt, wu, convw, convb)


def _ssd_chunk(xa_ref, bc_ref, z_ref, dt_ref, rows, dtb, alog, dskip_ref, g_ref, expand_ref, st_ref, yd_ref,
               padded_rows):
    q = SSD_SUB
    d_ssd = xa_ref.shape[-1]
    gw = d_ssd // SSD_GROUPS
    sub_planes = q // S5_BLOCK

    def half_of(v, half):
        return jnp.concatenate([v[s * PLANE + half * sub_planes:s * PLANE + (half + 1) * sub_planes]
                                for s in range(S5_BLOCK)], axis=0)

    def half_ref(ref, half, cols):
        first = rows.start + half * sub_planes
        return jnp.concatenate([ref[first + s * PLANE:first + s * PLANE + sub_planes, cols]
                                for s in range(S5_BLOCK)], axis=0)

    def token_of_row(r):
        return (r % sub_planes) * S5_BLOCK + r // sub_planes

    bc = bc_ref[rows, :].astype(F32)
    t_r = token_of_row(lax.broadcasted_iota(jnp.int32, (q, q), 0))
    t_c = token_of_row(lax.broadcasted_iota(jnp.int32, (q, q), 1))
    causal = t_r >= t_c
    tri = jnp.where(t_c >= t_r, 1.0, 0.0).astype(BF16)
    first_head = lax.broadcasted_iota(jnp.int32, (q, LANES), 1) < SSD_HEAD_DIM
    n_half = SSD_CHUNK // q
    n_tile = d_ssd // COL_TILE

    def prepare(half):
        bc_h = half_of(bc, half).astype(BF16)
        dt = jax.nn.softplus(half_ref(dt_ref, half, slice(None)) + dtb)
        if padded_rows:
            t_row = half * q + token_of_row(lax.broadcasted_iota(jnp.int32, dt.shape, 0))
            dt = jnp.where(t_row < padded_rows, 0.0, dt)
        d_a = dt * (-jnp.exp(alog))
        a_cs_t = _dot_split(d_a.T, tri, 3)
        return bc_h[:, :SSD_GROUPS * SSD_STATE], bc_h[:, SSD_GROUPS * SSD_STATE:], dt, a_cs_t.T, a_cs_t

    def expand_tile(half, t, dt, a_cs):
        cols = slice(t * COL_TILE, (t + 1) * COL_TILE)
        expand = expand_ref[:, cols]
        dt_x = _dot_split(dt, expand, 2)
        a_x = _dot_split(a_cs, expand, 2)
        a_last = a_x[q - 1:q, :]
        x_in = half_ref(xa_ref, half, cols)
        xdt = x_in * dt_x
        xw_b = (xdt * jnp.exp(a_last - a_x)).astype(BF16)
        return x_in, xdt.astype(BF16), xw_b, jnp.exp(a_x), jnp.exp(a_last)

    halves = [prepare(half) for half in range(n_half)]
    steps = [(half, t) for half in range(n_half) for t in range(n_tile)]
    expanded = expand_tile(*steps[0], halves[0][2], halves[0][3])
    ssq = [jnp.zeros((q, 1), F32) for _ in range(n_half)]
    cb = None
    for i, (half, t) in enumerate(steps):
        b_in, c_in, _, a_cs, a_cs_t = halves[half]
        x_in, xdt_b, xw_b, ea_x, ea_last = expanded
        if i + 1 < len(steps):
            nh, nt = steps[i + 1]
            expanded = expand_tile(nh, nt, halves[nh][2], halves[nh][3])
        cols = slice(t * COL_TILE, (t + 1) * COL_TILE)
        g = (t * COL_TILE) // gw
        gcols = slice(t * COL_TILE - g * gw, (t + 1) * COL_TILE - g * gw)
        cg = c_in[:, g * SSD_STATE:(g + 1) * SSD_STATE]
        bg = b_in[:, g * SSD_STATE:(g + 1) * SSD_STATE]
        if (t * COL_TILE) % gw == 0:
            cb = _dot_nt(cg, bg)
        s_prev = st_ref[g, :, gcols]
        y_off = _dot(cg, s_prev.astype(BF16)) * ea_x
        pairs = []
        for pair in range(COL_TILE // LANES):
            lo = pair * LANES
            att = []
            for h in (2 * (t * (COL_TILE // LANES) + pair), 2 * (t * (COL_TILE // LANES) + pair) + 1):
                seg = a_cs[:, h:h + 1] - a_cs_t[h:h + 1, :]
                att.append((cb * jnp.exp(jnp.where(causal, seg, -jnp.inf))).astype(BF16))
            x_pair = xdt_b[:, lo:lo + LANES]
            zero = jnp.zeros_like(x_pair)
            rhs = jnp.concatenate([jnp.where(first_head, x_pair, zero), jnp.where(first_head, zero, x_pair)], axis=0)
            pairs.append(_dot(jnp.concatenate(att, axis=1), rhs) + y_off[:, lo:lo + LANES])
        y = jnp.concatenate(pairs, axis=1) + x_in * dskip_ref[:, cols]
        y = y * _silu(half_ref(z_ref, half, cols))
        for s in range(S5_BLOCK):
            yd_ref[s * PLANE + half * sub_planes:s * PLANE + (half + 1) * sub_planes, cols] = (
                y[s * sub_planes:(s + 1) * sub_planes])
        ssq[half] = ssq[half] + jnp.sum(y * y, axis=-1, keepdims=True)
        st_ref[g, :, gcols] = s_prev * ea_last + _dot_tn(bg, xw_b)

    ssq = jnp.concatenate([ssq[half][s * sub_planes:(s + 1) * sub_planes]
                           for s in range(S5_BLOCK) for half in range(n_half)], axis=0)
    scale = lax.rsqrt(ssq * (1.0 / d_ssd) + NORM_EPS)
    return (yd_ref[...] * scale * g_ref[...]).astype(BF16)


def _ssd_kernel(xa_ref, bc_ref, z_ref, dt_ref, s0_ref, dtb_ref, alog_ref, dskip_ref, g_ref, expand_ref,
                y_ref, s_out_ref, st_ref, yd_ref, *, padded_rows):
    @pl.when(pl.program_id(1) == 0)
    def _():
        st_ref[...] = s0_ref[...]

    for c in range(xa_ref.shape[1] // SSD_CHUNK):
        rows = slice(c * SSD_CHUNK, (c + 1) * SSD_CHUNK)
        y_ref[0, rows, :] = _ssd_chunk(xa_ref.at[0], bc_ref.at[0], z_ref.at[0], dt_ref.at[0], rows, dtb_ref[...],
                                       alog_ref[...], dskip_ref, g_ref, expand_ref, st_ref, yd_ref, padded_rows)
    s_out_ref[...] = st_ref[...]


def _ssd(xa, bc, z, dt, s0, dtb, alog, dskip, g, expand, padded_rows, chunks_per_step):
    bsz, seq, d_ssd = xa.shape
    rows = SSD_CHUNK * chunks_per_step
    chunk = lambda w: pl.BlockSpec((1, rows, w), lambda b, c: (b, c, 0))
    return pl.pallas_call(
        functools.partial(_ssd_kernel, padded_rows=padded_rows),
        grid=(bsz, seq // rows),
        in_specs=[chunk(d_ssd), chunk(bc.shape[-1]), chunk(d_ssd), chunk(LANES)]
        + [_const_spec(a.shape) for a in (s0, dtb, alog, dskip, g, expand)],
        out_specs=[chunk(d_ssd), _const_spec(s0.shape)],
        out_shape=[jax.ShapeDtypeStruct((bsz, seq, d_ssd), BF16), jax.ShapeDtypeStruct(s0.shape, F32)],
        scratch_shapes=[pltpu.VMEM(s0.shape, F32), pltpu.VMEM((SSD_CHUNK, d_ssd), F32)],
        compiler_params=_params(("arbitrary", "arbitrary")),
        name="ssd",
    )(xa, bc, z, dt, s0, dtb, alog, dskip, g, expand)


def _lam_pow(lr, li, step, k):
    kf = k.astype(F32)
    mag = jnp.exp(kf * (lr * step))
    ang = kf * (li * step)
    return mag * jnp.cos(ang), mag * jnp.sin(ang)


def _s5_prep_kernel(lr_ref, li_ref, ls_ref, bt_re_ref, bt_im_ref, c_re_ref, c_im_ref,
                    mt_ref, wc_ref, ec_ref, dp_re_ref, dp_im_ref):
    t, hw, p = S5_BLOCK, S5_WIDTH, S5_STATE
    lr, li = lr_ref[0], li_ref[0]
    step = jnp.exp(ls_ref[0])
    k_up = lax.broadcasted_iota(jnp.int32, (3 * SUBLANES, 1), 0)
    pw_re, pw_im = _lam_pow(lr, li, step, k_up)
    k_dn = (t - 1) - lax.broadcasted_iota(jnp.int32, (t, 1), 0)
    pr_re, pr_im = _lam_pow(lr, li, step, k_dn)
    k_blk = t * lax.broadcasted_iota(jnp.int32, (2 * SUBLANES, 1), 0)
    dp_re, dp_im = _lam_pow(lr, li, step, k_blk)

    ab_re, ab_im = pw_re[1:2, :], pw_im[1:2, :]
    den = lr * lr + li * li
    coef_re = ((ab_re - 1.0) * lr + ab_im * li) / den
    coef_im = (ab_im * lr - (ab_re - 1.0) * li) / den
    bt_re, bt_im = bt_re_ref[0], bt_im_ref[0]
    bb_re = coef_re * bt_re - coef_im * bt_im
    bb_im = coef_re * bt_im + coef_im * bt_re
    c_re, c_im = c_re_ref[0], c_im_ref[0]

    def rep(m):
        return jnp.broadcast_to(m[:, None, :], (t, hw, p)).reshape(t * hw, p)

    def tile(m):
        return jnp.concatenate([m] * t, axis=0)

    hi = lax.Precision.HIGHEST
    lane = lax.broadcasted_iota(jnp.int32, (hw, t * hw), 1)
    for a in range(S5_PREP_GROUPS):
        sl = slice(a * p, (a + 1) * p)
        ct_re, ct_im = tile(c_re[:, sl]), tile(c_im[:, sl])
        l0_re, l0_im = rep(pw_re[0:t, sl]), rep(pw_im[0:t, sl])
        l1_re, l1_im = rep(pw_re[1:t + 1, sl]), rep(pw_im[1:t + 1, sl])
        x0_re = l0_re * ct_re - l0_im * ct_im
        x0_im = l0_re * ct_im + l0_im * ct_re
        kt = _dot_nt(bb_re[:, sl], x0_re, precision=hi) - _dot_nt(bb_im[:, sl], x0_im, precision=hi)
        for s in range(t):
            blk = kt if s == 0 else jnp.where(lane >= s * hw, pltpu.roll(kt, s * hw, axis=1), 0.0)
            mt_ref[a, s * hw:(s + 1) * hw, :] = blk.astype(mt_ref.dtype)
        x1_re = l1_re * ct_re - l1_im * ct_im
        x1_im = l1_re * ct_im + l1_im * ct_re
        ec_ref[a] = jnp.concatenate([x1_re, -x1_im], axis=1).T.astype(ec_ref.dtype)
        lw_re, lw_im = rep(pr_re[:, sl]), rep(pr_im[:, sl])
        bt_re_a, bt_im_a = tile(bb_re[:, sl]), tile(bb_im[:, sl])
        wc_ref[a] = jnp.concatenate([lw_re * bt_re_a - lw_im * bt_im_a, lw_re * bt_im_a + lw_im * bt_re_a],
                                    axis=1).astype(wc_ref.dtype)
        dp_re_ref[a] = jnp.concatenate([dp_re[:, sl]] * 2, axis=1)
        dp_im_ref[a] = jnp.concatenate([dp_im[:, sl]] * 2, axis=1)


def _s5_prep(lr, li, ls, bt_re, bt_im, c_re, c_im):
    steps = lr.shape[0]
    gs, p, fl = S5_PREP_GROUPS, S5_STATE, S5_FLAT
    n_grp = steps * gs
    in_spec = lambda a: pl.BlockSpec((1,) + a.shape[1:], lambda i: (i, 0, 0))
    out_spec = lambda *s: pl.BlockSpec((gs,) + s, lambda i: (i, 0, 0))
    out = lambda *s, dt=BF16: jax.ShapeDtypeStruct((n_grp,) + s, dt)
    args = (lr, li, ls, bt_re, bt_im, c_re, c_im)
    return pl.pallas_call(
        _s5_prep_kernel,
        grid=(steps,),
        in_specs=[in_spec(a) for a in args],
        out_specs=[out_spec(fl, fl), out_spec(fl, 2 * p), out_spec(2 * p, fl), out_spec(2 * SUBLANES, 2 * p),
                   out_spec(2 * SUBLANES, 2 * p)],
        out_shape=[out(fl, fl), out(fl, 2 * p), out(2 * p, fl), out(2 * SUBLANES, 2 * p, dt=F32),
                   out(2 * SUBLANES, 2 * p, dt=F32)],
        compiler_params=_params(("parallel",)),
        name="s5_prep",
    )(*args)


def _s5_core_kernel(f_r_ref, f_m_ref, mt_ref, wc_ref, ec_ref, dp_re_ref, dp_im_ref, dskip_ref,
                    y_ref, wu_re_ref, wu_im_ref, s_re_ref, s_im_ref, *, n_batch):
    p = S5_STATE
    rows_m = f_m_ref.shape[1]
    rows_r = f_r_ref.shape[1] // n_batch
    per_batch = rows_m + rows_r
    n_tiles = per_batch // SUBLANES
    lane_lo = lax.broadcasted_iota(jnp.int32, (1, 2 * p), 1) < p

    def pack(a, b):
        return jnp.where(lane_lo, a, pltpu.roll(b, p, axis=1)), jnp.where(lane_lo, pltpu.roll(a, p, axis=1), b)

    wu_m = [_dot(f_m_ref[a], wc_ref[a]) for a in range(2)]
    wu_r = [_dot(f_r_ref[a], wc_ref[a]) for a in range(2)]
    m_re, m_im = pack(*wu_m)
    r_re, r_im = pack(*wu_r)
    for b in range(n_batch):
        wu_re_ref[b * per_batch:b * per_batch + rows_m, :] = m_re
        wu_im_ref[b * per_batch:b * per_batch + rows_m, :] = m_im
        wu_re_ref[b * per_batch + rows_m:(b + 1) * per_batch, :] = r_re[b * rows_r:(b + 1) * rows_r]
        wu_im_ref[b * per_batch + rows_m:(b + 1) * per_batch, :] = r_im[b * rows_r:(b + 1) * rows_r]

    dp_re = jnp.where(lane_lo, dp_re_ref[0], dp_re_ref[1])
    dp_im = jnp.where(lane_lo, dp_im_ref[0], dp_im_ref[1])
    sub = lax.broadcasted_iota(jnp.int32, (SUBLANES, 2 * p), 0)

    def cmul_add(a_re, a_im, d_re, d_im, x_re, x_im):
        return a_re + d_re * x_re - d_im * x_im, a_im + d_re * x_im + d_im * x_re

    def shift_down(x, k):
        return jnp.where(sub >= k, pltpu.roll(x, k, axis=0), 0.0)

    def tile_step(i, carry):
        new = []
        for b in range(n_batch):
            c_re, c_im = carry[2 * b], carry[2 * b + 1]
            base = pl.multiple_of(b * per_batch + i * SUBLANES, SUBLANES)
            v_re = wu_re_ref[pl.ds(base, SUBLANES), :]
            v_im = wu_im_ref[pl.ds(base, SUBLANES), :]
            for lvl in range(3):
                k = 1 << lvl
                v_re, v_im = cmul_add(v_re, v_im, dp_re[k:k + 1, :], dp_im[k:k + 1, :],
                                      shift_down(v_re, k), shift_down(v_im, k))
            e_re, e_im = cmul_add(shift_down(v_re, 1), shift_down(v_im, 1), dp_re[0:SUBLANES, :],
                                  dp_im[0:SUBLANES, :], c_re, c_im)
            s_re_ref[pl.ds(base, SUBLANES), :] = e_re
            s_im_ref[pl.ds(base, SUBLANES), :] = e_im
            n_re, n_im = cmul_add(v_re[SUBLANES - 1:, :], v_im[SUBLANES - 1:, :], dp_re[SUBLANES:SUBLANES + 1, :],
                                  dp_im[SUBLANES:SUBLANES + 1, :], c_re, c_im)
            new += [n_re, n_im]
        return tuple(new)

    zero = jnp.zeros((1, 2 * p), F32)
    lax.fori_loop(0, n_tiles, tile_step, (zero,) * (2 * n_batch))

    for b in range(n_batch):
        lo = b * per_batch + rows_m
        s_a, s_b = pack(s_re_ref[lo:lo + rows_r, :], s_im_ref[lo:lo + rows_r, :])
        for a, s_ab in enumerate((s_a, s_b)):
            f = f_r_ref[a, b * rows_r:(b + 1) * rows_r, :]
            y = _dot(f, mt_ref[a]) + _dot(s_ab.astype(BF16), ec_ref[a]) + f.astype(F32) * dskip_ref[a]
            y_ref[a, b * rows_r:(b + 1) * rows_r, :] = _gelu(y).astype(y_ref.dtype)


def _s5_core(f_r, f_m, mt, wc, ec, dp_re, dp_im, dskip, n_batch):
    n_grp, rows_r, fl = f_r.shape
    rows_m = f_m.shape[1]
    p2 = 2 * S5_STATE
    rows = n_batch * rows_m + rows_r
    spec = lambda *s: pl.BlockSpec((2,) + s, lambda i: (i, 0, 0))
    return pl.pallas_call(
        functools.partial(_s5_core_kernel, n_batch=n_batch),
        grid=(n_grp // 2,),
        in_specs=[spec(rows_r, fl), spec(rows_m, fl), spec(fl, fl), spec(fl, p2), spec(p2, fl),
                  spec(2 * SUBLANES, p2), spec(2 * SUBLANES, p2), spec(1, fl)],
        out_specs=spec(rows_r, fl),
        out_shape=jax.ShapeDtypeStruct((n_grp, rows_r, fl), BF16),
        scratch_shapes=[pltpu.VMEM((rows, p2), F32)] * 4,
        compiler_params=_params(("parallel",)),
        name="s5_core",
    )(f_r, f_m, mt, wc, ec, dp_re, dp_im, dskip)


def _s5_glu_kernel(yg_ref, w_ref, b_ref, g_ref, o_ref):
    d = o_ref.shape[-1]
    tm = o_ref.shape[0]
    lane = lax.broadcasted_iota(jnp.int32, (PLANE, LANES), 1)
    chunks = []
    for chunk in range(tm // SSD_CHUNK):
        planes = [[None] * (d // LANES) for _ in range(S5_BLOCK)]
        for oct_ in range(d // LANES):
            for half in range(S5_BLOCK // LANE_BLOCKS):
                grp = [yg_ref[oct_ * LANE_BLOCKS + gi, chunk * PLANE:(chunk + 1) * PLANE,
                              half * LANES:(half + 1) * LANES].astype(F32) for gi in range(LANE_BLOCKS)]
                tok = _transpose_lane_blocks(grp, lane)
                for s8 in range(LANE_BLOCKS):
                    planes[half * LANE_BLOCKS + s8][oct_] = tok[s8].astype(BF16)
        chunks.append(jnp.concatenate([jnp.concatenate(pieces, axis=1) for pieces in planes], axis=0))
    y = jnp.concatenate(chunks, axis=0) if len(chunks) > 1 else chunks[0]
    v = _dot(y, w_ref[...]) + b_ref[...]
    o = v[:, :d] * jax.nn.sigmoid(v[:, d:])
    o_ref[...] = _rmsnorm(o, g_ref[...]).astype(o_ref.dtype)


def _s5_glu(yg, w, b, g, tm):
    n_grp, blocks, fl = yg.shape
    rows, d = blocks * S5_BLOCK, n_grp * S5_WIDTH
    nb = tm // S5_BLOCK
    return pl.pallas_call(
        _s5_glu_kernel,
        grid=(rows // tm,),
        in_specs=[pl.BlockSpec((n_grp, nb, fl), lambda i: (0, i, 0))]
        + [_const_spec(a.shape, single_buffer=True) for a in (w, b, g)],
        out_specs=pl.BlockSpec((tm, d), lambda i: (i, 0)),
        out_shape=jax.ShapeDtypeStruct((rows, d), BF16),
        compiler_params=_params(("parallel",)),
        name="s5_glu",
    )(yg, w, b, g)


def _out_mlp_kernel(x_ref, ya_ref, yb_ref, woa_ref, wob_ref, gm_ref, wup_ref, wdn_ref, gf_ref, o_ref, *, ff_tile):
    h = x_ref[...] + _swap_row_order(_dot(ya_ref[...], woa_ref[...]) + _dot(yb_ref[...], wob_ref[...]))
    n = _rmsnorm(h, gm_ref[...]).astype(BF16)
    d_ff = wup_ref.shape[1]
    mlp = None
    for k in range(d_ff // ff_tile):
        m = _dot(n, wup_ref[:, k * ff_tile:(k + 1) * ff_tile])
        m = jnp.square(jnp.maximum(m, 0.0)).astype(BF16)
        part = _dot(m, wdn_ref[k * ff_tile:(k + 1) * ff_tile, :])
        mlp = part if mlp is None else mlp + part
    o_ref[...] = _rmsnorm(h + mlp, gf_ref[...])


def _out_mlp(x_rows, ya, yb, woa, wob, gm, wup, wdn, gf, tm, ff_tile):
    rows, d = ya.shape
    row_spec = pl.BlockSpec((tm, d), lambda i: (i, 0))
    return pl.pallas_call(
        functools.partial(_out_mlp_kernel, ff_tile=ff_tile),
        grid=(rows // tm,),
        in_specs=[row_spec, row_spec, row_spec]
        + [_const_spec(a.shape, single_buffer=True) for a in (woa, wob, gm, wup, wdn, gf)],
        out_specs=row_spec,
        out_shape=jax.ShapeDtypeStruct((rows, d), F32),
        compiler_params=_params(("parallel",)),
        name="out_mlp",
    )(x_rows, ya, yb, woa, wob, gm, wup, wdn, gf)


def _row_tile(rows, want):
    tm = min(rows, want)
    assert rows % tm == 0
    return tm


def kernel(x, meta_tokens, g_mix, w_in, conv_w, conv_b, dt_bias, a_log, d_ssd, g_ssd, lam_re, lam_im, log_step,
           b_re, b_im, c_re, c_im, d_s5, w_glu, b_glu, g_s5, w_out, g_mlp, w_up, w_down, g_final):
    bsz, seq, d_model = x.shape
    assert w_in.shape[0] == 1, "single-layer block: meta-token outputs are only consumed as state"
    assert seq % SSD_CHUNK == 0 and d_model % LANES == 0
    n_heads = dt_bias.shape[-1]
    d_ssd_w = n_heads * SSD_HEAD_DIM
    d_xbc = conv_w.shape[-1]
    n_grp = d_s5.shape[-2]
    o_dt = d_ssd_w + d_xbc
    o_u = o_dt + n_heads
    row = lambda v: v.reshape(1, -1).astype(F32)

    wi = w_in[0]
    wz = wi[:, :d_ssd_w].astype(BF16)
    wxbc = wi[:, d_ssd_w:o_dt].astype(BF16)
    wdt = jnp.pad(wi[:, o_dt:o_u], ((0, 0), (0, LANES - n_heads))).astype(BF16)
    wu = wi[:, o_u:].astype(BF16)
    pad_heads = lambda v: jnp.pad(row(v), ((0, 0), (0, LANES - n_heads)))
    convw, convb, gmix = conv_w[0].astype(F32), row(conv_b[0]), row(g_mix[0])

    x_rows = x.reshape(bsz * seq, d_model)
    meta_chunk = jnp.concatenate([jnp.zeros((SSD_CHUNK - N_META, d_model), x.dtype), meta_tokens.astype(x.dtype)], axis=0)
    tm = _row_tile(seq, 1024)
    no_hist = jnp.zeros(((SSD_CONV - 1) * PLANE, d_xbc), F32)
    z_m, xa_m, bc_m, dt_m, f_m, tail_m = _in_proj(meta_chunk, no_hist, gmix, wz, wxbc, wdt, wu,
                                                  convw, convb, SSD_CHUNK, 1)
    z_r, xa_r, bc_r, dt_r, f_r, _ = _in_proj(x_rows, tail_m, gmix, wz, wxbc, wdt, wu, convw, convb, tm, seq // tm)

    seq3 = lambda a, b: a.reshape(b, -1, a.shape[-1])
    expand = (jnp.arange(d_ssd_w)[None, :] // SSD_HEAD_DIM == jnp.arange(LANES)[:, None]).astype(BF16)
    ssd_args = (pad_heads(dt_bias[0]), pad_heads(a_log[0]), row(jnp.repeat(d_ssd[0], SSD_HEAD_DIM)), row(g_ssd[0]), expand)
    no_state = jnp.zeros((SSD_GROUPS, SSD_STATE, d_ssd_w // SSD_GROUPS), F32)
    _, s_meta = _ssd(seq3(xa_m, 1), seq3(bc_m, 1), seq3(z_m, 1), seq3(dt_m, 1), no_state, *ssd_args,
                     padded_rows=SSD_CHUNK - N_META, chunks_per_step=1)
    y_ssd, _ = _ssd(seq3(xa_r, bsz), seq3(bc_r, bsz), seq3(z_r, bsz), seq3(dt_r, bsz), s_meta, *ssd_args, padded_rows=0,
                    chunks_per_step=2)

    t, gs, p = S5_BLOCK, S5_PREP_GROUPS, S5_STATE
    lanes = lambda v: v.astype(F32).reshape(n_grp // gs, 1, gs * p)
    per_grp = lambda v: (v.astype(F32).reshape(n_grp // gs, gs, S5_WIDTH, p).transpose(0, 2, 1, 3)
                         .reshape(n_grp // gs, S5_WIDTH, gs * p))
    mt, wc, ec, dp_re, dp_im = _s5_prep(lanes(lam_re[0]), lanes(lam_im[0]), lanes(jnp.repeat(log_step[0], p)),
                                        per_grp(jnp.swapaxes(b_re[0], 1, 2)), per_grp(jnp.swapaxes(b_im[0], 1, 2)),
                                        per_grp(c_re[0]), per_grp(c_im[0]))
    dsk = jnp.tile(d_s5[0].astype(F32), (1, t)).reshape(n_grp, 1, S5_FLAT)
    yg = _s5_core(f_r, f_m, mt, wc, ec, dp_re, dp_im, dsk, bsz)
    y_s5 = _s5_glu(yg, w_glu[0].astype(BF16), row(b_glu[0]), row(g_s5[0]), tm)

    wo = w_out[0].astype(BF16)
    out = _out_mlp(x_rows, y_ssd.reshape(bsz * seq, d_ssd_w), y_s5, wo[:d_ssd_w], wo[d_ssd_w:], row(g_mlp[0]),
                   w_up[0].astype(BF16), w_down[0].astype(BF16), row(g_final), tm, 1024)
    return out.reshape(bsz, seq, d_model).astype(x.dtype)
```

```python
import functools

import jax
import jax.numpy as jnp
from jax import lax
from jax.experimental import pallas as pl
from jax.experimental.pallas import tpu as pltpu

F32 = jnp.float32
BF16 = jnp.bfloat16

NORM_EPS = 1e-5
N_META = 16
SSD_CHUNK = 256
SSD_SUB = 128
SSD_HEAD_DIM = 64
SSD_GROUPS = 2
SSD_STATE = 128
SSD_CONV = 4
S5_WIDTH = 16
S5_STATE = 64
S5_BLOCK = 16
S5_FLAT = S5_BLOCK * S5_WIDTH
S5_PREP_GROUPS = 8
PLANE = SSD_CHUNK // S5_BLOCK
LANES = 128
SUBLANES = 8
LANE_BLOCKS = LANES // S5_WIDTH
COL_TILE = 256
VMEM_LIMIT_BYTES = 58 * 1024 * 1024

assert PLANE == S5_BLOCK == 2 * SUBLANES
assert LANES == 2 * SSD_HEAD_DIM


def _rmsnorm(x, g):
    return x * lax.rsqrt(jnp.mean(x * x, axis=-1, keepdims=True) + NORM_EPS) * g


def _silu(x):
    return x * jax.nn.sigmoid(x)


def _gelu(x):
    return 0.5 * x * (1.0 + lax.erf(x * (2.0 ** -0.5)))


def _dot(a, b):
    return jnp.dot(a, b, preferred_element_type=F32)


def _dot_nt(a, b, **kw):
    return lax.dot_general(a, b, (((1,), (1,)), ((), ())), preferred_element_type=F32, **kw)


def _dot_tn(a, b):
    return lax.dot_general(a, b, (((0,), (0,)), ((), ())), preferred_element_type=F32)


def _dot_split(a, m01, parts):
    acc = None
    rem = a
    for _ in range(parts):
        piece = rem.astype(BF16)
        rem = rem - piece.astype(F32)
        term = _dot(piece, m01)
        acc = term if acc is None else acc + term
    return acc


def _params(sem):
    return pltpu.CompilerParams(dimension_semantics=sem, vmem_limit_bytes=VMEM_LIMIT_BYTES)


def _const_spec(shape, single_buffer=False):
    nd = len(shape)
    mode = pl.Buffered(1) if single_buffer else None
    return pl.BlockSpec(shape, lambda *_: (0,) * nd, pipeline_mode=mode)


def _token_of_row(r):
    return (r % PLANE) * S5_BLOCK + r // PLANE


def _transpose8(arrays, pos, axis, unit):
    a = list(arrays)
    size = 8 * unit
    d = 4
    while d:
        keep_lo = (pos & (d * unit)) == 0
        for i in range(8):
            if i & d:
                continue
            lo, hi = a[i], a[i + d]
            a[i] = jnp.where(keep_lo, lo, pltpu.roll(hi, d * unit, axis=axis))
            a[i + d] = jnp.where(keep_lo, pltpu.roll(lo, size - d * unit, axis=axis), hi)
        d //= 2
    return a


def _transpose_lane_blocks(planes, lane):
    return _transpose8(planes, lane, 1, S5_WIDTH)


def _swap_row_order(v):
    sub = lax.broadcasted_iota(jnp.int32, (SUBLANES, v.shape[1]), 0)
    chunks = []
    for chunk in range(v.shape[0] // SSD_CHUNK):
        out = [[None, None] for _ in range(S5_BLOCK)]
        for ah in range(2):
            for bh in range(2):
                first = [chunk * SSD_CHUNK + (SUBLANES * ah + al) * PLANE + SUBLANES * bh for al in range(SUBLANES)]
                src = [v[r:r + SUBLANES] for r in first]
                dst = _transpose8(src, sub, 0, 1)
                for bl in range(SUBLANES):
                    out[SUBLANES * bh + bl][ah] = dst[bl]
        chunks.append(jnp.concatenate([piece for pair in out for piece in pair], axis=0))
    return jnp.concatenate(chunks, axis=0) if len(chunks) > 1 else chunks[0]


def _in_proj_kernel(x_ref, hist_ref, g_ref, wz_ref, wxbc_ref, wdt_ref, wu_ref, convw_ref, convb_ref,
                    z_ref, xa_ref, bc_ref, dt_ref, f_ref, tail_ref, hist_scr, *, tiles_per_seq):
    tm = z_ref.shape[0]
    d_x = xa_ref.shape[1]
    n_hist = (SSD_CONV - 1) * PLANE
    n_chunks = tm // SSD_CHUNK

    @pl.when(lax.rem(pl.program_id(0), tiles_per_seq) == 0)
    def _():
        hist_scr[...] = hist_ref[...]

    def normed(chunk):
        return _rmsnorm(_swap_row_order(x_ref[chunk * SSD_CHUNK:(chunk + 1) * SSD_CHUNK, :]), g_ref[...]).astype(BF16)

    lane = lax.broadcasted_iota(jnp.int32, (PLANE, LANES), 1)
    row0 = lax.broadcasted_iota(jnp.int32, (PLANE, 1), 0) == 0
    tile = lambda ct: slice(ct * COL_TILE, (ct + 1) * COL_TILE)
    chunk_rows = lambda c: slice(c * SSD_CHUNK, (c + 1) * SSD_CHUNK)

    def finish_u(ct, chunk, u):
        for o in range(COL_TILE // LANES):
            oct_ = ct * (COL_TILE // LANES) + o
            for half in range(S5_BLOCK // LANE_BLOCKS):
                tok = [u[s * PLANE:(s + 1) * PLANE, o * LANES:(o + 1) * LANES]
                       for s in range(half * LANE_BLOCKS, (half + 1) * LANE_BLOCKS)]
                grp = _transpose_lane_blocks(tok, lane)
                for gi in range(LANE_BLOCKS):
                    f_ref[oct_ * LANE_BLOCKS + gi, chunk * PLANE:(chunk + 1) * PLANE,
                          half * LANES:(half + 1) * LANES] = grp[gi].astype(BF16)

    def finish_xbc(ct, chunk, cur):
        cols = tile(ct)
        w = [convw_ref[k:k + 1, cols] for k in range(SSD_CONV)]
        prev = hist_scr[:, cols]
        last = cur[SSD_CHUNK - n_hist:]
        wrapped = [jnp.where(row0, pltpu.roll(prev[m * PLANE:(m + 1) * PLANE], 1, axis=0),
                             pltpu.roll(last[m * PLANE:(m + 1) * PLANE], 1, axis=0)) for m in range(SSD_CONV - 1)]
        acc = convb_ref[:, cols] + w[SSD_CONV - 1] * cur
        for k in range(1, SSD_CONV):
            shifted = jnp.concatenate(wrapped[SSD_CONV - 1 - k:] + [cur[:SSD_CHUNK - k * PLANE]], axis=0)
            acc = acc + w[SSD_CONV - 1 - k] * shifted
        act = _silu(acc)
        if ct * COL_TILE < d_x:
            xa_ref[chunk_rows(chunk), cols] = act
        else:
            bc_ref[chunk_rows(chunk), ct * COL_TILE - d_x:(ct + 1) * COL_TILE - d_x] = act.astype(BF16)
        hist_scr[:, cols] = last
        tail_ref[:, cols] = last

    def finish_z(ct, chunk, z):
        z_ref[chunk_rows(chunk), tile(ct)] = z

    def finish_dt(_, chunk, dt):
        dt_ref[chunk_rows(chunk), :] = dt

    jobs = []
    for chunk in range(n_chunks):
        u_jobs = [(wu_ref, ct, chunk, finish_u) for ct in range(wu_ref.shape[1] // COL_TILE)]
        x_jobs = [(wxbc_ref, ct, chunk, finish_xbc) for ct in range(wxbc_ref.shape[1] // COL_TILE)]
        z_jobs = [(wz_ref, ct, chunk, finish_z) for ct in range(wz_ref.shape[1] // COL_TILE)]
        while u_jobs or x_jobs or z_jobs:
            jobs += x_jobs[:1] + z_jobs[:1] + u_jobs[:1]
            u_jobs, x_jobs, z_jobs = u_jobs[1:], x_jobs[1:], z_jobs[1:]
        jobs.append((wdt_ref, 0, chunk, finish_dt))

    n = {0: normed(0)}

    def matmul(job):
        w_ref, ct, chunk, _ = job
        return _dot(n[chunk], w_ref[:, tile(ct)] if w_ref.shape[1] > COL_TILE else w_ref[...])

    per_chunk = len(jobs) // n_chunks
    res = matmul(jobs[0])
    for i, job in enumerate(jobs):
        if i % per_chunk == per_chunk // 2 and job[2] + 1 < n_chunks:
            n[job[2] + 1] = normed(job[2] + 1)
        nxt = matmul(jobs[i + 1]) if i + 1 < len(jobs) else None
        job[3](job[1], job[2], res)
        res = nxt


def _in_proj(x_rows, hist, g, wz, wxbc, wdt, wu, convw, convb, tm, tiles_per_seq):
    rows, d = x_rows.shape
    d_z, d_xbc, d_u = wz.shape[1], wxbc.shape[1], wu.shape[1]
    d_bc = d_xbc - d_z
    n_grp = d_u // S5_WIDTH
    nb = tm // S5_BLOCK
    row_spec = lambda w: pl.BlockSpec((tm, w), lambda i: (i, 0))
    return pl.pallas_call(
        functools.partial(_in_proj_kernel, tiles_per_seq=tiles_per_seq),
        grid=(rows // tm,),
        in_specs=[row_spec(d)]
        + [_const_spec(a.shape, single_buffer=True) for a in (hist, g, wz, wxbc, wdt, wu, convw, convb)],
        out_specs=[row_spec(d_z), row_spec(d_z), row_spec(d_bc), row_spec(LANES),
                   pl.BlockSpec((n_grp, nb, S5_FLAT), lambda i: (0, i, 0)), _const_spec(hist.shape)],
        out_shape=[jax.ShapeDtypeStruct((rows, d_z), F32), jax.ShapeDtypeStruct((rows, d_z), F32),
                   jax.ShapeDtypeStruct((rows, d_bc), BF16), jax.ShapeDtypeStruct((rows, LANES), F32),
                   jax.ShapeDtypeStruct((n_grp, rows // S5_BLOCK, S5_FLAT), BF16),
                   jax.ShapeDtypeStruct(hist.shape, F32)],
        scratch_shapes=[pltpu.VMEM(hist.shape, F32)],
        compiler_params=_params(("arbitrary",)),
        name="in_proj",
    )(x_rows, hist, g, wz, wxbc, wdt, wu, convw, convb)


def _ssd_chunk(xa_ref, bc_ref, z_ref, dt_ref, rows, dtb, alog, dskip_ref, g_ref, expand_ref, st_ref, yd_ref,
               padded_rows):
    q = SSD_SUB
    d_ssd = xa_ref.shape[-1]
    gw = d_ssd // SSD_GROUPS
    sub_planes = q // S5_BLOCK

    def half_of(v, half):
        return jnp.concatenate([v[s * PLANE + half * sub_planes:s * PLANE + (half + 1) * sub_planes]
                                for s in range(S5_BLOCK)], axis=0)

    def half_ref(ref, half, cols):
        first = rows.start + half * sub_planes
        return jnp.concatenate([ref[first + s * PLANE:first + s * PLANE + sub_planes, cols]
                                for s in range(S5_BLOCK)], axis=0)

    def token_of_row(r):
        return (r % sub_planes) * S5_BLOCK + r // sub_planes

    bc = bc_ref[rows, :].astype(F32)
    t_r = token_of_row(lax.broadcasted_iota(jnp.int32, (q, q), 0))
    t_c = token_of_row(lax.broadcasted_iota(jnp.int32, (q, q), 1))
    causal = t_r >= t_c
    tri = jnp.where(t_c >= t_r, 1.0, 0.0).astype(BF16)
    first_head = lax.broadcasted_iota(jnp.int32, (q, LANES), 1) < SSD_HEAD_DIM
    n_half = SSD_CHUNK // q
    n_tile = d_ssd // COL_TILE

    def prepare(half):
        bc_h = half_of(bc, half).astype(BF16)
        dt = jax.nn.softplus(half_ref(dt_ref, half, slice(None)) + dtb)
        if padded_rows:
            t_row = half * q + token_of_row(lax.broadcasted_iota(jnp.int32, dt.shape, 0))
            dt = jnp.where(t_row < padded_rows, 0.0, dt)
        d_a = dt * (-jnp.exp(alog))
        a_cs_t = _dot_split(d_a.T, tri, 3)
        return bc_h[:, :SSD_GROUPS * SSD_STATE], bc_h[:, SSD_GROUPS * SSD_STATE:], dt, a_cs_t.T, a_cs_t

    def expand_tile(half, t, dt, a_cs):
        cols = slice(t * COL_TILE, (t + 1) * COL_TILE)
        expand = expand_ref[:, cols]
        dt_x = _dot_split(dt, expand, 2)
        a_x = _dot_split(a_cs, expand, 2)
        a_last = a_x[q - 1:q, :]
        x_in = half_ref(xa_ref, half, cols)
        xdt = x_in * dt_x
        xw_b = (xdt * jnp.exp(a_last - a_x)).astype(BF16)
        return x_in, xdt.astype(BF16), xw_b, jnp.exp(a_x), jnp.exp(a_last)

    halves = [prepare(half) for half in range(n_half)]
    steps = [(half, t) for half in range(n_half) for t in range(n_tile)]
    expanded = expand_tile(*steps[0], halves[0][2], halves[0][3])
    ssq = [jnp.zeros((q, 1), F32) for _ in range(n_half)]
    cb = None
    for i, (half, t) in enumerate(steps):
        b_in, c_in, _, a_cs, a_cs_t = halves[half]
        x_in, xdt_b, xw_b, ea_x, ea_last = expanded
        if i + 1 < len(steps):
            nh, nt = steps[i + 1]
            expanded = expand_tile(nh, nt, halves[nh][2], halves[nh][3])
        cols = slice(t * COL_TILE, (t + 1) * COL_TILE)
        g = (t * COL_TILE) // gw
        gcols = slice(t * COL_TILE - g * gw, (t + 1) * COL_TILE - g * gw)
        cg = c_in[:, g * SSD_STATE:(g + 1) * SSD_STATE]
        bg = b_in[:, g * SSD_STATE:(g + 1) * SSD_STATE]
        if (t * COL_TILE) % gw == 0:
            cb = _dot_nt(cg, bg)
        s_prev = st_ref[g, :, gcols]
        y_off = _dot(cg, s_prev.astype(BF16)) * ea_x
        pairs = []
        for pair in range(COL_TILE // LANES):
            lo = pair * LANES
            att = []
            for h in (2 * (t * (COL_TILE // LANES) + pair), 2 * (t * (COL_TILE // LANES) + pair) + 1):
                seg = a_cs[:, h:h + 1] - a_cs_t[h:h + 1, :]
                att.append((cb * jnp.exp(jnp.where(causal, seg, -jnp.inf))).astype(BF16))
            x_pair = xdt_b[:, lo:lo + LANES]
            zero = jnp.zeros_like(x_pair)
            rhs = jnp.concatenate([jnp.where(first_head, x_pair, zero), jnp.where(first_head, zero, x_pair)], axis=0)
            pairs.append(_dot(jnp.concatenate(att, axis=1), rhs) + y_off[:, lo:lo + LANES])
        y = jnp.concatenate(pairs, axis=1) + x_in * dskip_ref[:, cols]
        y = y * _silu(half_ref(z_ref, half, cols))
        for s in range(S5_BLOCK):
            yd_ref[s * PLANE + half * sub_planes:s * PLANE + (half + 1) * sub_planes, cols] = (
                y[s * sub_planes:(s + 1) * sub_planes])
        ssq[half] = ssq[half] + jnp.sum(y * y, axis=-1, keepdims=True)
        st_ref[g, :, gcols] = s_prev * ea_last + _dot_tn(bg, xw_b)

    ssq = jnp.concatenate([ssq[half][s * sub_planes:(s + 1) * sub_planes]
                           for s in range(S5_BLOCK) for half in range(n_half)], axis=0)
    scale = lax.rsqrt(ssq * (1.0 / d_ssd) + NORM_EPS)
    return (yd_ref[...] * scale * g_ref[...]).astype(BF16)


def _ssd_kernel(xa_ref, bc_ref, z_ref, dt_ref, s0_ref, dtb_ref, alog_ref, dskip_ref, g_ref, expand_ref,
                y_ref, s_out_ref, st_ref, yd_ref, *, padded_rows):
    @pl.when(pl.program_id(1) == 0)
    def _():
        st_ref[...] = s0_ref[...]

    for c in range(xa_ref.shape[1] // SSD_CHUNK):
        rows = slice(c * SSD_CHUNK, (c + 1) * SSD_CHUNK)
        y_ref[0, rows, :] = _ssd_chunk(xa_ref.at[0], bc_ref.at[0], z_ref.at[0], dt_ref.at[0], rows, dtb_ref[...],
                                       alog_ref[...], dskip_ref, g_ref, expand_ref, st_ref, yd_ref, padded_rows)
    s_out_ref[...] = st_ref[...]


def _ssd(xa, bc, z, dt, s0, dtb, alog, dskip, g, expand, padded_rows, chunks_per_step):
    bsz, seq, d_ssd = xa.shape
    rows = SSD_CHUNK * chunks_per_step
    chunk = lambda w: pl.BlockSpec((1, rows, w), lambda b, c: (b, c, 0))
    return pl.pallas_call(
        functools.partial(_ssd_kernel, padded_rows=padded_rows),
        grid=(bsz, seq // rows),
        in_specs=[chunk(d_ssd), chunk(bc.shape[-1]), chunk(d_ssd), chunk(LANES)]
        + [_const_spec(a.shape) for a in (s0, dtb, alog, dskip, g, expand)],
        out_specs=[chunk(d_ssd), _const_spec(s0.shape)],
        out_shape=[jax.ShapeDtypeStruct((bsz, seq, d_ssd), BF16), jax.ShapeDtypeStruct(s0.shape, F32)],
        scratch_shapes=[pltpu.VMEM(s0.shape, F32), pltpu.VMEM((SSD_CHUNK, d_ssd), F32)],
        compiler_params=_params(("arbitrary", "arbitrary")),
        name="ssd",
    )(xa, bc, z, dt, s0, dtb, alog, dskip, g, expand)


def _lam_pow(lr, li, step, k):
    kf = k.astype(F32)
    mag = jnp.exp(kf * (lr * step))
    ang = kf * (li * step)
    return mag * jnp.cos(ang), mag * jnp.sin(ang)


def _s5_prep_kernel(lr_ref, li_ref, ls_ref, bt_re_ref, bt_im_ref, c_re_ref, c_im_ref,
                    mt_ref, wc_ref, ec_ref, dp_re_ref, dp_im_ref):
    t, hw, p = S5_BLOCK, S5_WIDTH, S5_STATE
    lr, li = lr_ref[0], li_ref[0]
    step = jnp.exp(ls_ref[0])
    k_up = lax.broadcasted_iota(jnp.int32, (3 * SUBLANES, 1), 0)
    pw_re, pw_im = _lam_pow(lr, li, step, k_up)
    k_dn = (t - 1) - lax.broadcasted_iota(jnp.int32, (t, 1), 0)
    pr_re, pr_im = _lam_pow(lr, li, step, k_dn)
    k_blk = t * lax.broadcasted_iota(jnp.int32, (2 * SUBLANES, 1), 0)
    dp_re, dp_im = _lam_pow(lr, li, step, k_blk)

    ab_re, ab_im = pw_re[1:2, :], pw_im[1:2, :]
    den = lr * lr + li * li
    coef_re = ((ab_re - 1.0) * lr + ab_im * li) / den
    coef_im = (ab_im * lr - (ab_re - 1.0) * li) / den
    bt_re, bt_im = bt_re_ref[0], bt_im_ref[0]
    bb_re = coef_re * bt_re - coef_im * bt_im
    bb_im = coef_re * bt_im + coef_im * bt_re
    c_re, c_im = c_re_ref[0], c_im_ref[0]

    def rep(m):
        return jnp.broadcast_to(m[:, None, :], (t, hw, p)).reshape(t * hw, p)

    def tile(m):
        return jnp.concatenate([m] * t, axis=0)

    hi = lax.Precision.HIGHEST
    lane = lax.broadcasted_iota(jnp.int32, (hw, t * hw), 1)
    for a in range(S5_PREP_GROUPS):
        sl = slice(a * p, (a + 1) * p)
        ct_re, ct_im = tile(c_re[:, sl]), tile(c_im[:, sl])
        l0_re, l0_im = rep(pw_re[0:t, sl]), rep(pw_im[0:t, sl])
        l1_re, l1_im = rep(pw_re[1:t + 1, sl]), rep(pw_im[1:t + 1, sl])
        x0_re = l0_re * ct_re - l0_im * ct_im
        x0_im = l0_re * ct_im + l0_im * ct_re
        kt = _dot_nt(bb_re[:, sl], x0_re, precision=hi) - _dot_nt(bb_im[:, sl], x0_im, precision=hi)
        for s in range(t):
            blk = kt if s == 0 else jnp.where(lane >= s * hw, pltpu.roll(kt, s * hw, axis=1), 0.0)
            mt_ref[a, s * hw:(s + 1) * hw, :] = blk.astype(mt_ref.dtype)
        x1_re = l1_re * ct_re - l1_im * ct_im
        x1_im = l1_re * ct_im + l1_im * ct_re
        ec_ref[a] = jnp.concatenate([x1_re, -x1_im], axis=1).T.astype(ec_ref.dtype)
        lw_re, lw_im = rep(pr_re[:, sl]), rep(pr_im[:, sl])
        bt_re_a, bt_im_a = tile(bb_re[:, sl]), tile(bb_im[:, sl])
        wc_ref[a] = jnp.concatenate([lw_re * bt_re_a - lw_im * bt_im_a, lw_re * bt_im_a + lw_im * bt_re_a],
                                    axis=1).astype(wc_ref.dtype)
        dp_re_ref[a] = jnp.concatenate([dp_re[:, sl]] * 2, axis=1)
        dp_im_ref[a] = jnp.concatenate([dp_im[:, sl]] * 2, axis=1)


def _s5_prep(lr, li, ls, bt_re, bt_im, c_re, c_im):
    steps = lr.shape[0]
    gs, p, fl = S5_PREP_GROUPS, S5_STATE, S5_FLAT
    n_grp = steps * gs
    in_spec = lambda a: pl.BlockSpec((1,) + a.shape[1:], lambda i: (i, 0, 0))
    out_spec = lambda *s: pl.BlockSpec((gs,) + s, lambda i: (i, 0, 0))
    out = lambda *s, dt=BF16: jax.ShapeDtypeStruct((n_grp,) + s, dt)
    args = (lr, li, ls, bt_re, bt_im, c_re, c_im)
    return pl.pallas_call(
        _s5_prep_kernel,
        grid=(steps,),
        in_specs=[in_spec(a) for a in args],
        out_specs=[out_spec(fl, fl), out_spec(fl, 2 * p), out_spec(2 * p, fl), out_spec(2 * SUBLANES, 2 * p),
                   out_spec(2 * SUBLANES, 2 * p)],
        out_shape=[out(fl, fl), out(fl, 2 * p), out(2 * p, fl), out(2 * SUBLANES, 2 * p, dt=F32),
                   out(2 * SUBLANES, 2 * p, dt=F32)],
        compiler_params=_params(("parallel",)),
        name="s5_prep",
    )(*args)


def _s5_core_kernel(f_r_ref, f_m_ref, mt_ref, wc_ref, ec_ref, dp_re_ref, dp_im_ref, dskip_ref,
                    y_ref, wu_re_ref, wu_im_ref, s_re_ref, s_im_ref, *, n_batch):
    p = S5_STATE
    rows_m = f_m_ref.shape[1]
    rows_r = f_r_ref.shape[1] // n_batch
    per_batch = rows_m + rows_r
    n_tiles = per_batch // SUBLANES
    lane_lo = lax.broadcasted_iota(jnp.int32, (1, 2 * p), 1) < p

    def pack(a, b):
        return jnp.where(lane_lo, a, pltpu.roll(b, p, axis=1)), jnp.where(lane_lo, pltpu.roll(a, p, axis=1), b)

    wu_m = [_dot(f_m_ref[a], wc_ref[a]) for a in range(2)]
    wu_r = [_dot(f_r_ref[a], wc_ref[a]) for a in range(2)]
    m_re, m_im = pack(*wu_m)
    r_re, r_im = pack(*wu_r)
    for b in range(n_batch):
        wu_re_ref[b * per_batch:b * per_batch + rows_m, :] = m_re
        wu_im_ref[b * per_batch:b * per_batch + rows_m, :] = m_im
        wu_re_ref[b * per_batch + rows_m:(b + 1) * per_batch, :] = r_re[b * rows_r:(b + 1) * rows_r]
        wu_im_ref[b * per_batch + rows_m:(b + 1) * per_batch, :] = r_im[b * rows_r:(b + 1) * rows_r]

    dp_re = jnp.where(lane_lo, dp_re_ref[0], dp_re_ref[1])
    dp_im = jnp.where(lane_lo, dp_im_ref[0], dp_im_ref[1])
    sub = lax.broadcasted_iota(jnp.int32, (SUBLANES, 2 * p), 0)

    def cmul_add(a_re, a_im, d_re, d_im, x_re, x_im):
        return a_re + d_re * x_re - d_im * x_im, a_im + d_re * x_im + d_im * x_re

    def shift_down(x, k):
        return jnp.where(sub >= k, pltpu.roll(x, k, axis=0), 0.0)

    def tile_step(i, carry):
        new = []
        for b in range(n_batch):
            c_re, c_im = carry[2 * b], carry[2 * b + 1]
            base = pl.multiple_of(b * per_batch + i * SUBLANES, SUBLANES)
            v_re = wu_re_ref[pl.ds(base, SUBLANES), :]
            v_im = wu_im_ref[pl.ds(base, SUBLANES), :]
            for lvl in range(3):
                k = 1 << lvl
                v_re, v_im = cmul_add(v_re, v_im, dp_re[k:k + 1, :], dp_im[k:k + 1, :],
                                      shift_down(v_re, k), shift_down(v_im, k))
            e_re, e_im = cmul_add(shift_down(v_re, 1), shift_down(v_im, 1), dp_re[0:SUBLANES, :],
                                  dp_im[0:SUBLANES, :], c_re, c_im)
            s_re_ref[pl.ds(base, SUBLANES), :] = e_re
            s_im_ref[pl.ds(base, SUBLANES), :] = e_im
            n_re, n_im = cmul_add(v_re[SUBLANES - 1:, :], v_im[SUBLANES - 1:, :], dp_re[SUBLANES:SUBLANES + 1, :],
                                  dp_im[SUBLANES:SUBLANES + 1, :], c_re, c_im)
            new += [n_re, n_im]
        return tuple(new)

    zero = jnp.zeros((1, 2 * p), F32)
    lax.fori_loop(0, n_tiles, tile_step, (zero,) * (2 * n_batch))

    for b in range(n_batch):
        lo = b * per_batch + rows_m
        s_a, s_b = pack(s_re_ref[lo:lo + rows_r, :], s_im_ref[lo:lo + rows_r, :])
        for a, s_ab in enumerate((s_a, s_b)):
            f = f_r_ref[a, b * rows_r:(b + 1) * rows_r, :]
            y = _dot(f, mt_ref[a]) + _dot(s_ab.astype(BF16), ec_ref[a]) + f.astype(F32) * dskip_ref[a]
            y_ref[a, b * rows_r:(b + 1) * rows_r, :] = _gelu(y).astype(y_ref.dtype)


def _s5_core(f_r, f_m, mt, wc, ec, dp_re, dp_im, dskip, n_batch):
    n_grp, rows_r, fl = f_r.shape
    rows_m = f_m.shape[1]
    p2 = 2 * S5_STATE
    rows = n_batch * rows_m + rows_r
    spec = lambda *s: pl.BlockSpec((2,) + s, lambda i: (i, 0, 0))
    return pl.pallas_call(
        functools.partial(_s5_core_kernel, n_batch=n_batch),
        grid=(n_grp // 2,),
        in_specs=[spec(rows_r, fl), spec(rows_m, fl), spec(fl, fl), spec(fl, p2), spec(p2, fl),
                  spec(2 * SUBLANES, p2), spec(2 * SUBLANES, p2), spec(1, fl)],
        out_specs=spec(rows_r, fl),
        out_shape=jax.ShapeDtypeStruct((n_grp, rows_r, fl), BF16),
        scratch_shapes=[pltpu.VMEM((rows, p2), F32)] * 4,
        compiler_params=_params(("parallel",)),
        name="s5_core",
    )(f_r, f_m, mt, wc, ec, dp_re, dp_im, dskip)


def _s5_glu_kernel(yg_ref, w_ref, b_ref, g_ref, o_ref):
    d = o_ref.shape[-1]
    tm = o_ref.shape[0]
    lane = lax.broadcasted_iota(jnp.int32, (PLANE, LANES), 1)
    chunks = []
    for chunk in range(tm // SSD_CHUNK):
        planes = [[None] * (d // LANES) for _ in range(S5_BLOCK)]
        for oct_ in range(d // LANES):
            for half in range(S5_BLOCK // LANE_BLOCKS):
                grp = [yg_ref[oct_ * LANE_BLOCKS + gi, chunk * PLANE:(chunk + 1) * PLANE,
                              half * LANES:(half + 1) * LANES].astype(F32) for gi in range(LANE_BLOCKS)]
                tok = _transpose_lane_blocks(grp, lane)
                for s8 in range(LANE_BLOCKS):
                    planes[half * LANE_BLOCKS + s8][oct_] = tok[s8].astype(BF16)
        chunks.append(jnp.concatenate([jnp.concatenate(pieces, axis=1) for pieces in planes], axis=0))
    y = jnp.concatenate(chunks, axis=0) if len(chunks) > 1 else chunks[0]
    v = _dot(y, w_ref[...]) + b_ref[...]
    o = v[:, :d] * jax.nn.sigmoid(v[:, d:])
    o_ref[...] = _rmsnorm(o, g_ref[...]).astype(o_ref.dtype)


def _s5_glu(yg, w, b, g, tm):
    n_grp, blocks, fl = yg.shape
    rows, d = blocks * S5_BLOCK, n_grp * S5_WIDTH
    nb = tm // S5_BLOCK
    return pl.pallas_call(
        _s5_glu_kernel,
        grid=(rows // tm,),
        in_specs=[pl.BlockSpec((n_grp, nb, fl), lambda i: (0, i, 0))]
        + [_const_spec(a.shape, single_buffer=True) for a in (w, b, g)],
        out_specs=pl.BlockSpec((tm, d), lambda i: (i, 0)),
        out_shape=jax.ShapeDtypeStruct((rows, d), BF16),
        compiler_params=_params(("parallel",)),
        name="s5_glu",
    )(yg, w, b, g)


def _out_mlp_kernel(x_ref, ya_ref, yb_ref, woa_ref, wob_ref, gm_ref, wup_ref, wdn_ref, gf_ref, o_ref, *, ff_tile):
    h = x_ref[...] + _swap_row_order(_dot(ya_ref[...], woa_ref[...]) + _dot(yb_ref[...], wob_ref[...]))
    n = _rmsnorm(h, gm_ref[...]).astype(BF16)
    d_ff = wup_ref.shape[1]
    mlp = None
    for k in range(d_ff // ff_tile):
        m = _dot(n, wup_ref[:, k * ff_tile:(k + 1) * ff_tile])
        m = jnp.square(jnp.maximum(m, 0.0)).astype(BF16)
        part = _dot(m, wdn_ref[k * ff_tile:(k + 1) * ff_tile, :])
        mlp = part if mlp is None else mlp + part
    o_ref[...] = _rmsnorm(h + mlp, gf_ref[...])


def _out_mlp(x_rows, ya, yb, woa, wob, gm, wup, wdn, gf, tm, ff_tile):
    rows, d = ya.shape
    row_spec = pl.BlockSpec((tm, d), lambda i: (i, 0))
    return pl.pallas_call(
        functools.partial(_out_mlp_kernel, ff_tile=ff_tile),
        grid=(rows // tm,),
        in_specs=[row_spec, row_spec, row_spec]
        + [_const_spec(a.shape, single_buffer=True) for a in (woa, wob, gm, wup, wdn, gf)],
        out_specs=row_spec,
        out_shape=jax.ShapeDtypeStruct((rows, d), F32),
        compiler_params=_params(("parallel",)),
        name="out_mlp",
    )(x_rows, ya, yb, woa, wob, gm, wup, wdn, gf)


def _row_tile(rows, want):
    tm = min(rows, want)
    assert rows % tm == 0
    return tm


def kernel(x, meta_tokens, g_mix, w_in, conv_w, conv_b, dt_bias, a_log, d_ssd, g_ssd, lam_re, lam_im, log_step,
           b_re, b_im, c_re, c_im, d_s5, w_glu, b_glu, g_s5, w_out, g_mlp, w_up, w_down, g_final):
    bsz, seq, d_model = x.shape
    assert w_in.shape[0] == 1, "single-layer block: meta-token outputs are only consumed as state"
    assert seq % SSD_CHUNK == 0 and d_model % LANES == 0
    n_heads = dt_bias.shape[-1]
    d_ssd_w = n_heads * SSD_HEAD_DIM
    d_xbc = conv_w.shape[-1]
    n_grp = d_s5.shape[-2]
    o_dt = d_ssd_w + d_xbc
    o_u = o_dt + n_heads
    row = lambda v: v.reshape(1, -1).astype(F32)

    wi = w_in[0]
    wz = wi[:, :d_ssd_w].astype(BF16)
    wxbc = wi[:, d_ssd_w:o_dt].astype(BF16)
    wdt = jnp.pad(wi[:, o_dt:o_u], ((0, 0), (0, LANES - n_heads))).astype(BF16)
    wu = wi[:, o_u:].astype(BF16)
    pad_heads = lambda v: jnp.pad(row(v), ((0, 0), (0, LANES - n_heads)))
    convw, convb, gmix = conv_w[0].astype(F32), row(conv_b[0]), row(g_mix[0])

    x_rows = x.reshape(bsz * seq, d_model)
    meta_chunk = jnp.concatenate([jnp.zeros((SSD_CHUNK - N_META, d_model), x.dtype), meta_tokens.astype(x.dtype)], axis=0)
    tm = _row_tile(seq, 1024)
    no_hist = jnp.zeros(((SSD_CONV - 1) * PLANE, d_xbc), F32)
    z_m, xa_m, bc_m, dt_m, f_m, tail_m = _in_proj(meta_chunk, no_hist, gmix, wz, wxbc, wdt, wu,
                                                  convw, convb, SSD_CHUNK, 1)
    z_r, xa_r, bc_r, dt_r, f_r, _ = _in_proj(x_rows, tail_m, gmix, wz, wxbc, wdt, wu, convw, convb, tm, seq // tm)

    seq3 = lambda a, b: a.reshape(b, -1, a.shape[-1])
    expand = (jnp.arange(d_ssd_w)[None, :] // SSD_HEAD_DIM == jnp.arange(LANES)[:, None]).astype(BF16)
    ssd_args = (pad_heads(dt_bias[0]), pad_heads(a_log[0]), row(jnp.repeat(d_ssd[0], SSD_HEAD_DIM)), row(g_ssd[0]), expand)
    no_state = jnp.zeros((SSD_GROUPS, SSD_STATE, d_ssd_w // SSD_GROUPS), F32)
    _, s_meta = _ssd(seq3(xa_m, 1), seq3(bc_m, 1), seq3(z_m, 1), seq3(dt_m, 1), no_state, *ssd_args,
                     padded_rows=SSD_CHUNK - N_META, chunks_per_step=1)
    y_ssd, _ = _ssd(seq3(xa_r, bsz), seq3(bc_r, bsz), seq3(z_r, bsz), seq3(dt_r, bsz), s_meta, *ssd_args, padded_rows=0,
                    chunks_per_step=2)

    t, gs, p = S5_BLOCK, S5_PREP_GROUPS, S5_STATE
    lanes = lambda v: v.astype(F32).reshape(n_grp // gs, 1, gs * p)
    per_grp = lambda v: (v.astype(F32).reshape(n_grp // gs, gs, S5_WIDTH, p).transpose(0, 2, 1, 3)
                         .reshape(n_grp // gs, S5_WIDTH, gs * p))
    mt, wc, ec, dp_re, dp_im = _s5_prep(lanes(lam_re[0]), lanes(lam_im[0]), lanes(jnp.repeat(log_step[0], p)),
                                        per_grp(jnp.swapaxes(b_re[0], 1, 2)), per_grp(jnp.swapaxes(b_im[0], 1, 2)),
                                        per_grp(c_re[0]), per_grp(c_im[0]))
    dsk = jnp.tile(d_s5[0].astype(F32), (1, t)).reshape(n_grp, 1, S5_FLAT)
    yg = _s5_core(f_r, f_m, mt, wc, ec, dp_re, dp_im, dsk, bsz)
    y_s5 = _s5_glu(yg, w_glu[0].astype(BF16), row(b_glu[0]), row(g_s5[0]), tm)

    wo = w_out[0].astype(BF16)
    out = _out_mlp(x_rows, y_ssd.reshape(bsz * seq, d_ssd_w), y_s5, wo[:d_ssd_w], wo[d_ssd_w:], row(g_mlp[0]),
                   w_up[0].astype(BF16), w_down[0].astype(BF16), row(g_final), tm, 1024)
    return out.reshape(bsz, seq, d_model).astype(x.dtype)
```

```python
import functools

import jax
import jax.numpy as jnp
from jax import lax
from jax.experimental import pallas as pl
from jax.experimental.pallas import tpu as pltpu

F32 = jnp.float32
BF16 = jnp.bfloat16

NORM_EPS = 1e-5
N_META = 16
SSD_CHUNK = 256
SSD_SUB = 128
SSD_HEAD_DIM = 64
SSD_GROUPS = 2
SSD_STATE = 128
SSD_CONV = 4
S5_WIDTH = 16
S5_STATE = 64
S5_BLOCK = 16
S5_FLAT = S5_BLOCK * S5_WIDTH
S5_PREP_GROUPS = 8
PLANE = SSD_CHUNK // S5_BLOCK
LANES = 128
SUBLANES = 8
LANE_BLOCKS = LANES // S5_WIDTH
COL_TILE = 256
VMEM_LIMIT_BYTES = 58 * 1024 * 1024

assert PLANE == S5_BLOCK == 2 * SUBLANES
assert LANES == 2 * SSD_HEAD_DIM


def _rmsnorm(x, g):
    return x * lax.rsqrt(jnp.mean(x * x, axis=-1, keepdims=True) + NORM_EPS) * g


def _silu(x):
    return x * jax.nn.sigmoid(x)


def _gelu(x):
    return 0.5 * x * (1.0 + lax.erf(x * (2.0 ** -0.5)))


def _dot(a, b):
    return jnp.dot(a, b, preferred_element_type=F32)


def _dot_nt(a, b, **kw):
    return lax.dot_general(a, b, (((1,), (1,)), ((), ())), preferred_element_type=F32, **kw)


def _dot_tn(a, b):
    return lax.dot_general(a, b, (((0,), (0,)), ((), ())), preferred_element_type=F32)


def _dot_split(a, m01, parts):
    acc = None
    rem = a
    for _ in range(parts):
        piece = rem.astype(BF16)
        rem = rem - piece.astype(F32)
        term = _dot(piece, m01)
        acc = term if acc is None else acc + term
    return acc


def _params(sem):
    return pltpu.CompilerParams(dimension_semantics=sem, vmem_limit_bytes=VMEM_LIMIT_BYTES)


def _const_spec(shape, single_buffer=False):
    nd = len(shape)
    mode = pl.Buffered(1) if single_buffer else None
    return pl.BlockSpec(shape, lambda *_: (0,) * nd, pipeline_mode=mode)


def _token_of_row(r):
    return (r % PLANE) * S5_BLOCK + r // PLANE


def _transpose8(arrays, pos, axis, unit):
    a = list(arrays)
    size = 8 * unit
    d = 4
    while d:
        keep_lo = (pos & (d * unit)) == 0
        for i in range(8):
            if i & d:
                continue
            lo, hi = a[i], a[i + d]
            a[i] = jnp.where(keep_lo, lo, pltpu.roll(hi, d * unit, axis=axis))
            a[i + d] = jnp.where(keep_lo, pltpu.roll(lo, size - d * unit, axis=axis), hi)
        d //= 2
    return a


def _transpose_lane_blocks(planes, lane):
    return _transpose8(planes, lane, 1, S5_WIDTH)


def _swap_row_order(v):
    sub = lax.broadcasted_iota(jnp.int32, (SUBLANES, v.shape[1]), 0)
    chunks = []
    for chunk in range(v.shape[0] // SSD_CHUNK):
        out = [[None, None] for _ in range(S5_BLOCK)]
        for ah in range(2):
            for bh in range(2):
                first = [chunk * SSD_CHUNK + (SUBLANES * ah + al) * PLANE + SUBLANES * bh for al in range(SUBLANES)]
                src = [v[r:r + SUBLANES] for r in first]
                dst = _transpose8(src, sub, 0, 1)
                for bl in range(SUBLANES):
                    out[SUBLANES * bh + bl][ah] = dst[bl]
        chunks.append(jnp.concatenate([piece for pair in out for piece in pair], axis=0))
    return jnp.concatenate(chunks, axis=0) if len(chunks) > 1 else chunks[0]


def _in_proj_kernel(x_ref, hist_ref, g_ref, wz_ref, wxbc_ref, wdt_ref, wu_ref, convw_ref, convb_ref,
                    z_ref, xa_ref, bc_ref, dt_ref, f_ref, tail_ref, hist_scr, *, tiles_per_seq):
    tm = z_ref.shape[0]
    d_x = xa_ref.shape[1]
    n_hist = (SSD_CONV - 1) * PLANE
    n_chunks = tm // SSD_CHUNK

    @pl.when(lax.rem(pl.program_id(0), tiles_per_seq) == 0)
    def _():
        hist_scr[...] = hist_ref[...]

    n = _rmsnorm(_swap_row_order(x_ref[...]), g_ref[...]).astype(BF16)

    lane = lax.broadcasted_iota(jnp.int32, (PLANE, LANES), 1)
    for ct in range(wu_ref.shape[1] // COL_TILE):
        u = _dot(n, wu_ref[:, ct * COL_TILE:(ct + 1) * COL_TILE])
        for chunk in range(n_chunks):
            for o in range(COL_TILE // LANES):
                oct_ = ct * (COL_TILE // LANES) + o
                for half in range(S5_BLOCK // LANE_BLOCKS):
                    tok = []
                    for s in range(half * LANE_BLOCKS, (half + 1) * LANE_BLOCKS):
                        lo = chunk * SSD_CHUNK + s * PLANE
                        tok.append(u[lo:lo + PLANE, o * LANES:(o + 1) * LANES])
                    grp = _transpose_lane_blocks(tok, lane)
                    for gi in range(LANE_BLOCKS):
                        f_ref[oct_ * LANE_BLOCKS + gi, chunk * PLANE:(chunk + 1) * PLANE,
                              half * LANES:(half + 1) * LANES] = grp[gi].astype(BF16)

    row0 = lax.broadcasted_iota(jnp.int32, (PLANE, 1), 0) == 0
    for ct in range(wxbc_ref.shape[1] // COL_TILE):
        cols = slice(ct * COL_TILE, (ct + 1) * COL_TILE)
        pre = _dot(n, wxbc_ref[:, cols])
        w = [convw_ref[k:k + 1, cols] for k in range(SSD_CONV)]
        prev = hist_scr[:, cols]
        for chunk in range(n_chunks):
            cur = pre[chunk * SSD_CHUNK:(chunk + 1) * SSD_CHUNK]
            last = cur[SSD_CHUNK - n_hist:]
            wrapped = [jnp.where(row0, pltpu.roll(prev[m * PLANE:(m + 1) * PLANE], 1, axis=0),
                                 pltpu.roll(last[m * PLANE:(m + 1) * PLANE], 1, axis=0)) for m in range(SSD_CONV - 1)]
            acc = convb_ref[:, cols] + w[SSD_CONV - 1] * cur
            for k in range(1, SSD_CONV):
                shifted = jnp.concatenate(wrapped[SSD_CONV - 1 - k:] + [cur[:SSD_CHUNK - k * PLANE]], axis=0)
                acc = acc + w[SSD_CONV - 1 - k] * shifted
            act = _silu(acc)
            rows = slice(chunk * SSD_CHUNK, (chunk + 1) * SSD_CHUNK)
            if ct * COL_TILE < d_x:
                xa_ref[rows, cols] = act
            else:
                bc_ref[rows, ct * COL_TILE - d_x:(ct + 1) * COL_TILE - d_x] = act.astype(BF16)
            prev = last
        hist_scr[:, cols] = prev
        tail_ref[:, cols] = prev

    for ct in range(wz_ref.shape[1] // COL_TILE):
        cols = slice(ct * COL_TILE, (ct + 1) * COL_TILE)
        z_ref[:, cols] = _dot(n, wz_ref[:, cols])
    dt_ref[...] = _dot(n, wdt_ref[...])


def _in_proj(x_rows, hist, g, wz, wxbc, wdt, wu, convw, convb, tm, tiles_per_seq):
    rows, d = x_rows.shape
    d_z, d_xbc, d_u = wz.shape[1], wxbc.shape[1], wu.shape[1]
    d_bc = d_xbc - d_z
    n_grp = d_u // S5_WIDTH
    nb = tm // S5_BLOCK
    row_spec = lambda w: pl.BlockSpec((tm, w), lambda i: (i, 0))
    return pl.pallas_call(
        functools.partial(_in_proj_kernel, tiles_per_seq=tiles_per_seq),
        grid=(rows // tm,),
        in_specs=[row_spec(d)]
        + [_const_spec(a.shape, single_buffer=True) for a in (hist, g, wz, wxbc, wdt, wu, convw, convb)],
        out_specs=[row_spec(d_z), row_spec(d_z), row_spec(d_bc), row_spec(LANES),
                   pl.BlockSpec((n_grp, nb, S5_FLAT), lambda i: (0, i, 0)), _const_spec(hist.shape)],
        out_shape=[jax.ShapeDtypeStruct((rows, d_z), F32), jax.ShapeDtypeStruct((rows, d_z), F32),
                   jax.ShapeDtypeStruct((rows, d_bc), BF16), jax.ShapeDtypeStruct((rows, LANES), F32),
                   jax.ShapeDtypeStruct((n_grp, rows // S5_BLOCK, S5_FLAT), BF16),
                   jax.ShapeDtypeStruct(hist.shape, F32)],
        scratch_shapes=[pltpu.VMEM(hist.shape, F32)],
        compiler_params=_params(("arbitrary",)),
        name="in_proj",
    )(x_rows, hist, g, wz, wxbc, wdt, wu, convw, convb)


def _ssd_chunk(xa_ref, bc_ref, z_ref, dt_ref, rows, dtb, alog, dskip_ref, g_ref, expand_ref, st_ref, yd_ref,
               padded_rows):
    q = SSD_SUB
    d_ssd = xa_ref.shape[-1]
    gw = d_ssd // SSD_GROUPS
    sub_planes = q // S5_BLOCK

    def half_of(v, half):
        return jnp.concatenate([v[s * PLANE + half * sub_planes:s * PLANE + (half + 1) * sub_planes]
                                for s in range(S5_BLOCK)], axis=0)

    def half_ref(ref, half, cols):
        first = rows.start + half * sub_planes
        return jnp.concatenate([ref[first + s * PLANE:first + s * PLANE + sub_planes, cols]
                                for s in range(S5_BLOCK)], axis=0)

    def token_of_row(r):
        return (r % sub_planes) * S5_BLOCK + r // sub_planes

    bc = bc_ref[rows, :].astype(F32)
    t_r = token_of_row(lax.broadcasted_iota(jnp.int32, (q, q), 0))
    t_c = token_of_row(lax.broadcasted_iota(jnp.int32, (q, q), 1))
    causal = t_r >= t_c
    tri = jnp.where(t_c >= t_r, 1.0, 0.0).astype(BF16)
    first_head = lax.broadcasted_iota(jnp.int32, (q, LANES), 1) < SSD_HEAD_DIM
    n_half = SSD_CHUNK // q
    n_tile = d_ssd // COL_TILE

    def prepare(half):
        bc_h = half_of(bc, half).astype(BF16)
        dt = jax.nn.softplus(half_ref(dt_ref, half, slice(None)) + dtb)
        if padded_rows:
            t_row = half * q + token_of_row(lax.broadcasted_iota(jnp.int32, dt.shape, 0))
            dt = jnp.where(t_row < padded_rows, 0.0, dt)
        d_a = dt * (-jnp.exp(alog))
        a_cs_t = _dot_split(d_a.T, tri, 3)
        return bc_h[:, :SSD_GROUPS * SSD_STATE], bc_h[:, SSD_GROUPS * SSD_STATE:], dt, a_cs_t.T, a_cs_t

    def expand_tile(half, t, dt, a_cs):
        cols = slice(t * COL_TILE, (t + 1) * COL_TILE)
        expand = expand_ref[:, cols]
        dt_x = _dot_split(dt, expand, 2)
        a_x = _dot_split(a_cs, expand, 2)
        a_last = a_x[q - 1:q, :]
        x_in = half_ref(xa_ref, half, cols)
        xdt = x_in * dt_x
        xw_b = (xdt * jnp.exp(a_last - a_x)).astype(BF16)
        return x_in, xdt.astype(BF16), xw_b, jnp.exp(a_x), jnp.exp(a_last)

    halves = [prepare(half) for half in range(n_half)]
    steps = [(half, t) for half in range(n_half) for t in range(n_tile)]
    expanded = expand_tile(*steps[0], halves[0][2], halves[0][3])
    ssq = [jnp.zeros((q, 1), F32) for _ in range(n_half)]
    cb = None
    for i, (half, t) in enumerate(steps):
        b_in, c_in, _, a_cs, a_cs_t = halves[half]
        x_in, xdt_b, xw_b, ea_x, ea_last = expanded
        if i + 1 < len(steps):
            nh, nt = steps[i + 1]
            expanded = expand_tile(nh, nt, halves[nh][2], halves[nh][3])
        cols = slice(t * COL_TILE, (t + 1) * COL_TILE)
        g = (t * COL_TILE) // gw
        gcols = slice(t * COL_TILE - g * gw, (t + 1) * COL_TILE - g * gw)
        cg = c_in[:, g * SSD_STATE:(g + 1) * SSD_STATE]
        bg = b_in[:, g * SSD_STATE:(g + 1) * SSD_STATE]
        if (t * COL_TILE) % gw == 0:
            cb = _dot_nt(cg, bg)
        s_prev = st_ref[g, :, gcols]
        y_off = _dot(cg, s_prev.astype(BF16)) * ea_x
        pairs = []
        for pair in range(COL_TILE // LANES):
            lo = pair * LANES
            att = []
            for h in (2 * (t * (COL_TILE // LANES) + pair), 2 * (t * (COL_TILE // LANES) + pair) + 1):
                seg = a_cs[:, h:h + 1] - a_cs_t[h:h + 1, :]
                att.append((cb * jnp.exp(jnp.where(causal, seg, -jnp.inf))).astype(BF16))
            x_pair = xdt_b[:, lo:lo + LANES]
            zero = jnp.zeros_like(x_pair)
            rhs = jnp.concatenate([jnp.where(first_head, x_pair, zero), jnp.where(first_head, zero, x_pair)], axis=0)
            pairs.append(_dot(jnp.concatenate(att, axis=1), rhs) + y_off[:, lo:lo + LANES])
        y = jnp.concatenate(pairs, axis=1) + x_in * dskip_ref[:, cols]
        y = y * _silu(half_ref(z_ref, half, cols))
        for s in range(S5_BLOCK):
            yd_ref[s * PLANE + half * sub_planes:s * PLANE + (half + 1) * sub_planes, cols] = (
                y[s * sub_planes:(s + 1) * sub_planes])
        ssq[half] = ssq[half] + jnp.sum(y * y, axis=-1, keepdims=True)
        st_ref[g, :, gcols] = s_prev * ea_last + _dot_tn(bg, xw_b)

    ssq = jnp.concatenate([ssq[half][s * sub_planes:(s + 1) * sub_planes]
                           for s in range(S5_BLOCK) for half in range(n_half)], axis=0)
    scale = lax.rsqrt(ssq * (1.0 / d_ssd) + NORM_EPS)
    return (yd_ref[...] * scale * g_ref[...]).astype(BF16)


def _ssd_kernel(xa_ref, bc_ref, z_ref, dt_ref, s0_ref, dtb_ref, alog_ref, dskip_ref, g_ref, expand_ref,
                y_ref, s_out_ref, st_ref, yd_ref, *, padded_rows):
    @pl.when(pl.program_id(1) == 0)
    def _():
        st_ref[...] = s0_ref[...]

    for c in range(xa_ref.shape[1] // SSD_CHUNK):
        rows = slice(c * SSD_CHUNK, (c + 1) * SSD_CHUNK)
        y_ref[0, rows, :] = _ssd_chunk(xa_ref.at[0], bc_ref.at[0], z_ref.at[0], dt_ref.at[0], rows, dtb_ref[...],
                                       alog_ref[...], dskip_ref, g_ref, expand_ref, st_ref, yd_ref, padded_rows)
    s_out_ref[...] = st_ref[...]


def _ssd(xa, bc, z, dt, s0, dtb, alog, dskip, g, expand, padded_rows, chunks_per_step):
    bsz, seq, d_ssd = xa.shape
    rows = SSD_CHUNK * chunks_per_step
    chunk = lambda w: pl.BlockSpec((1, rows, w), lambda b, c: (b, c, 0))
    return pl.pallas_call(
        functools.partial(_ssd_kernel, padded_rows=padded_rows),
        grid=(bsz, seq // rows),
        in_specs=[chunk(d_ssd), chunk(bc.shape[-1]), chunk(d_ssd), chunk(LANES)]
        + [_const_spec(a.shape) for a in (s0, dtb, alog, dskip, g, expand)],
        out_specs=[chunk(d_ssd), _const_spec(s0.shape)],
        out_shape=[jax.ShapeDtypeStruct((bsz, seq, d_ssd), BF16), jax.ShapeDtypeStruct(s0.shape, F32)],
        scratch_shapes=[pltpu.VMEM(s0.shape, F32), pltpu.VMEM((SSD_CHUNK, d_ssd), F32)],
        compiler_params=_params(("arbitrary", "arbitrary")),
        name="ssd",
    )(xa, bc, z, dt, s0, dtb, alog, dskip, g, expand)


def _lam_pow(lr, li, step, k):
    kf = k.astype(F32)
    mag = jnp.exp(kf * (lr * step))
    ang = kf * (li * step)
    return mag * jnp.cos(ang), mag * jnp.sin(ang)


def _s5_prep_kernel(lr_ref, li_ref, ls_ref, bt_re_ref, bt_im_ref, c_re_ref, c_im_ref,
                    mt_ref, wc_ref, ec_ref, dp_re_ref, dp_im_ref):
    t, hw, p = S5_BLOCK, S5_WIDTH, S5_STATE
    lr, li = lr_ref[0], li_ref[0]
    step = jnp.exp(ls_ref[0])
    k_up = lax.broadcasted_iota(jnp.int32, (3 * SUBLANES, 1), 0)
    pw_re, pw_im = _lam_pow(lr, li, step, k_up)
    k_dn = (t - 1) - lax.broadcasted_iota(jnp.int32, (t, 1), 0)
    pr_re, pr_im = _lam_pow(lr, li, step, k_dn)
    k_blk = t * lax.broadcasted_iota(jnp.int32, (2 * SUBLANES, 1), 0)
    dp_re, dp_im = _lam_pow(lr, li, step, k_blk)

    ab_re, ab_im = pw_re[1:2, :], pw_im[1:2, :]
    den = lr * lr + li * li
    coef_re = ((ab_re - 1.0) * lr + ab_im * li) / den
    coef_im = (ab_im * lr - (ab_re - 1.0) * li) / den
    bt_re, bt_im = bt_re_ref[0], bt_im_ref[0]
    bb_re = coef_re * bt_re - coef_im * bt_im
    bb_im = coef_re * bt_im + coef_im * bt_re
    c_re, c_im = c_re_ref[0], c_im_ref[0]

    def rep(m):
        return jnp.broadcast_to(m[:, None, :], (t, hw, p)).reshape(t * hw, p)

    def tile(m):
        return jnp.concatenate([m] * t, axis=0)

    hi = lax.Precision.HIGHEST
    lane = lax.broadcasted_iota(jnp.int32, (hw, t * hw), 1)
    for a in range(S5_PREP_GROUPS):
        sl = slice(a * p, (a + 1) * p)
        ct_re, ct_im = tile(c_re[:, sl]), tile(c_im[:, sl])
        l0_re, l0_im = rep(pw_re[0:t, sl]), rep(pw_im[0:t, sl])
        l1_re, l1_im = rep(pw_re[1:t + 1, sl]), rep(pw_im[1:t + 1, sl])
        x0_re = l0_re * ct_re - l0_im * ct_im
        x0_im = l0_re * ct_im + l0_im * ct_re
        kt = _dot_nt(bb_re[:, sl], x0_re, precision=hi) - _dot_nt(bb_im[:, sl], x0_im, precision=hi)
        for s in range(t):
            blk = kt if s == 0 else jnp.where(lane >= s * hw, pltpu.roll(kt, s * hw, axis=1), 0.0)
            mt_ref[a, s * hw:(s + 1) * hw, :] = blk.astype(mt_ref.dtype)
        x1_re = l1_re * ct_re - l1_im * ct_im
        x1_im = l1_re * ct_im + l1_im * ct_re
        ec_ref[a] = jnp.concatenate([x1_re, -x1_im], axis=1).T.astype(ec_ref.dtype)
        lw_re, lw_im = rep(pr_re[:, sl]), rep(pr_im[:, sl])
        bt_re_a, bt_im_a = tile(bb_re[:, sl]), tile(bb_im[:, sl])
        wc_ref[a] = jnp.concatenate([lw_re * bt_re_a - lw_im * bt_im_a, lw_re * bt_im_a + lw_im * bt_re_a],
                                    axis=1).astype(wc_ref.dtype)
        dp_re_ref[a] = jnp.concatenate([dp_re[:, sl]] * 2, axis=1)
        dp_im_ref[a] = jnp.concatenate([dp_im[:, sl]] * 2, axis=1)


def _s5_prep(lr, li, ls, bt_re, bt_im, c_re, c_im):
    steps = lr.shape[0]
    gs, p, fl = S5_PREP_GROUPS, S5_STATE, S5_FLAT
    n_grp = steps * gs
    in_spec = lambda a: pl.BlockSpec((1,) + a.shape[1:], lambda i: (i, 0, 0))
    out_spec = lambda *s: pl.BlockSpec((gs,) + s, lambda i: (i, 0, 0))
    out = lambda *s, dt=BF16: jax.ShapeDtypeStruct((n_grp,) + s, dt)
    args = (lr, li, ls, bt_re, bt_im, c_re, c_im)
    return pl.pallas_call(
        _s5_prep_kernel,
        grid=(steps,),
        in_specs=[in_spec(a) for a in args],
        out_specs=[out_spec(fl, fl), out_spec(fl, 2 * p), out_spec(2 * p, fl), out_spec(2 * SUBLANES, 2 * p),
                   out_spec(2 * SUBLANES, 2 * p)],
        out_shape=[out(fl, fl), out(fl, 2 * p), out(2 * p, fl), out(2 * SUBLANES, 2 * p, dt=F32),
                   out(2 * SUBLANES, 2 * p, dt=F32)],
        compiler_params=_params(("parallel",)),
        name="s5_prep",
    )(*args)


def _s5_core_kernel(f_r_ref, f_m_ref, mt_ref, wc_ref, ec_ref, dp_re_ref, dp_im_ref, dskip_ref,
                    y_ref, wu_re_ref, wu_im_ref, s_re_ref, s_im_ref, *, n_batch):
    p = S5_STATE
    rows_m = f_m_ref.shape[1]
    rows_r = f_r_ref.shape[1] // n_batch
    per_batch = rows_m + rows_r
    n_tiles = per_batch // SUBLANES
    lane_lo = lax.broadcasted_iota(jnp.int32, (1, 2 * p), 1) < p

    def pack(a, b):
        return jnp.where(lane_lo, a, pltpu.roll(b, p, axis=1)), jnp.where(lane_lo, pltpu.roll(a, p, axis=1), b)

    wu_m = [_dot(f_m_ref[a], wc_ref[a]) for a in range(2)]
    wu_r = [_dot(f_r_ref[a], wc_ref[a]) for a in range(2)]
    m_re, m_im = pack(*wu_m)
    r_re, r_im = pack(*wu_r)
    for b in range(n_batch):
        wu_re_ref[b * per_batch:b * per_batch + rows_m, :] = m_re
        wu_im_ref[b * per_batch:b * per_batch + rows_m, :] = m_im
        wu_re_ref[b * per_batch + rows_m:(b + 1) * per_batch, :] = r_re[b * rows_r:(b + 1) * rows_r]
        wu_im_ref[b * per_batch + rows_m:(b + 1) * per_batch, :] = r_im[b * rows_r:(b + 1) * rows_r]

    dp_re = jnp.where(lane_lo, dp_re_ref[0], dp_re_ref[1])
    dp_im = jnp.where(lane_lo, dp_im_ref[0], dp_im_ref[1])
    sub = lax.broadcasted_iota(jnp.int32, (SUBLANES, 2 * p), 0)

    def cmul_add(a_re, a_im, d_re, d_im, x_re, x_im):
        return a_re + d_re * x_re - d_im * x_im, a_im + d_re * x_im + d_im * x_re

    def shift_down(x, k):
        return jnp.where(sub >= k, pltpu.roll(x, k, axis=0), 0.0)

    def tile_step(i, carry):
        new = []
        for b in range(n_batch):
            c_re, c_im = carry[2 * b], carry[2 * b + 1]
            base = pl.multiple_of(b * per_batch + i * SUBLANES, SUBLANES)
            v_re = wu_re_ref[pl.ds(base, SUBLANES), :]
            v_im = wu_im_ref[pl.ds(base, SUBLANES), :]
            for lvl in range(3):
                k = 1 << lvl
                v_re, v_im = cmul_add(v_re, v_im, dp_re[k:k + 1, :], dp_im[k:k + 1, :],
                                      shift_down(v_re, k), shift_down(v_im, k))
            e_re, e_im = cmul_add(shift_down(v_re, 1), shift_down(v_im, 1), dp_re[0:SUBLANES, :],
                                  dp_im[0:SUBLANES, :], c_re, c_im)
            s_re_ref[pl.ds(base, SUBLANES), :] = e_re
            s_im_ref[pl.ds(base, SUBLANES), :] = e_im
            n_re, n_im = cmul_add(v_re[SUBLANES - 1:, :], v_im[SUBLANES - 1:, :], dp_re[SUBLANES:SUBLANES + 1, :],
                                  dp_im[SUBLANES:SUBLANES + 1, :], c_re, c_im)
            new += [n_re, n_im]
        return tuple(new)

    zero = jnp.zeros((1, 2 * p), F32)
    lax.fori_loop(0, n_tiles, tile_step, (zero,) * (2 * n_batch), unroll=True)

    for b in range(n_batch):
        lo = b * per_batch + rows_m
        s_a, s_b = pack(s_re_ref[lo:lo + rows_r, :], s_im_ref[lo:lo + rows_r, :])
        for a, s_ab in enumerate((s_a, s_b)):
            f = f_r_ref[a, b * rows_r:(b + 1) * rows_r, :]
            y = _dot(f, mt_ref[a]) + _dot(s_ab.astype(BF16), ec_ref[a]) + f.astype(F32) * dskip_ref[a]
            y_ref[a, b * rows_r:(b + 1) * rows_r, :] = _gelu(y).astype(y_ref.dtype)


def _s5_core(f_r, f_m, mt, wc, ec, dp_re, dp_im, dskip, n_batch):
    n_grp, rows_r, fl = f_r.shape
    rows_m = f_m.shape[1]
    p2 = 2 * S5_STATE
    rows = n_batch * rows_m + rows_r
    spec = lambda *s: pl.BlockSpec((2,) + s, lambda i: (i, 0, 0))
    return pl.pallas_call(
        functools.partial(_s5_core_kernel, n_batch=n_batch),
        grid=(n_grp // 2,),
        in_specs=[spec(rows_r, fl), spec(rows_m, fl), spec(fl, fl), spec(fl, p2), spec(p2, fl),
                  spec(2 * SUBLANES, p2), spec(2 * SUBLANES, p2), spec(1, fl)],
        out_specs=spec(rows_r, fl),
        out_shape=jax.ShapeDtypeStruct((n_grp, rows_r, fl), BF16),
        scratch_shapes=[pltpu.VMEM((rows, p2), F32)] * 4,
        compiler_params=_params(("parallel",)),
        name="s5_core",
    )(f_r, f_m, mt, wc, ec, dp_re, dp_im, dskip)


def _s5_glu_kernel(yg_ref, w_ref, b_ref, g_ref, o_ref):
    d = o_ref.shape[-1]
    tm = o_ref.shape[0]
    lane = lax.broadcasted_iota(jnp.int32, (PLANE, LANES), 1)
    chunks = []
    for chunk in range(tm // SSD_CHUNK):
        planes = [[None] * (d // LANES) for _ in range(S5_BLOCK)]
        for oct_ in range(d // LANES):
            for half in range(S5_BLOCK // LANE_BLOCKS):
                grp = [yg_ref[oct_ * LANE_BLOCKS + gi, chunk * PLANE:(chunk + 1) * PLANE,
                              half * LANES:(half + 1) * LANES].astype(F32) for gi in range(LANE_BLOCKS)]
                tok = _transpose_lane_blocks(grp, lane)
                for s8 in range(LANE_BLOCKS):
                    planes[half * LANE_BLOCKS + s8][oct_] = tok[s8].astype(BF16)
        chunks.append(jnp.concatenate([jnp.concatenate(pieces, axis=1) for pieces in planes], axis=0))
    y = jnp.concatenate(chunks, axis=0) if len(chunks) > 1 else chunks[0]
    v = _dot(y, w_ref[...]) + b_ref[...]
    o = v[:, :d] * jax.nn.sigmoid(v[:, d:])
    o_ref[...] = _rmsnorm(o, g_ref[...]).astype(o_ref.dtype)


def _s5_glu(yg, w, b, g, tm):
    n_grp, blocks, fl = yg.shape
    rows, d = blocks * S5_BLOCK, n_grp * S5_WIDTH
    nb = tm // S5_BLOCK
    return pl.pallas_call(
        _s5_glu_kernel,
        grid=(rows // tm,),
        in_specs=[pl.BlockSpec((n_grp, nb, fl), lambda i: (0, i, 0))]
        + [_const_spec(a.shape, single_buffer=True) for a in (w, b, g)],
        out_specs=pl.BlockSpec((tm, d), lambda i: (i, 0)),
        out_shape=jax.ShapeDtypeStruct((rows, d), BF16),
        compiler_params=_params(("parallel",)),
        name="s5_glu",
    )(yg, w, b, g)


def _out_mlp_kernel(x_ref, ya_ref, yb_ref, woa_ref, wob_ref, gm_ref, wup_ref, wdn_ref, gf_ref, o_ref, *, ff_tile):
    h = x_ref[...] + _swap_row_order(_dot(ya_ref[...], woa_ref[...]) + _dot(yb_ref[...], wob_ref[...]))
    n = _rmsnorm(h, gm_ref[...]).astype(BF16)
    d_ff = wup_ref.shape[1]
    mlp = None
    for k in range(d_ff // ff_tile):
        m = _dot(n, wup_ref[:, k * ff_tile:(k + 1) * ff_tile])
        m = jnp.square(jnp.maximum(m, 0.0)).astype(BF16)
        part = _dot(m, wdn_ref[k * ff_tile:(k + 1) * ff_tile, :])
        mlp = part if mlp is None else mlp + part
    o_ref[...] = _rmsnorm(h + mlp, gf_ref[...])


def _out_mlp(x_rows, ya, yb, woa, wob, gm, wup, wdn, gf, tm, ff_tile):
    rows, d = ya.shape
    row_spec = pl.BlockSpec((tm, d), lambda i: (i, 0))
    return pl.pallas_call(
        functools.partial(_out_mlp_kernel, ff_tile=ff_tile),
        grid=(rows // tm,),
        in_specs=[row_spec, row_spec, row_spec]
        + [_const_spec(a.shape, single_buffer=True) for a in (woa, wob, gm, wup, wdn, gf)],
        out_specs=row_spec,
        out_shape=jax.ShapeDtypeStruct((rows, d), F32),
        compiler_params=_params(("parallel",)),
        name="out_mlp",
    )(x_rows, ya, yb, woa, wob, gm, wup, wdn, gf)


def _row_tile(rows, want):
    tm = min(rows, want)
    assert rows % tm == 0
    return tm


def kernel(x, meta_tokens, g_mix, w_in, conv_w, conv_b, dt_bias, a_log, d_ssd, g_ssd, lam_re, lam_im, log_step,
           b_re, b_im, c_re, c_im, d_s5, w_glu, b_glu, g_s5, w_out, g_mlp, w_up, w_down, g_final):
    bsz, seq, d_model = x.shape
    assert w_in.shape[0] == 1, "single-layer block: meta-token outputs are only consumed as state"
    assert seq % SSD_CHUNK == 0 and d_model % LANES == 0
    n_heads = dt_bias.shape[-1]
    d_ssd_w = n_heads * SSD_HEAD_DIM
    d_xbc = conv_w.shape[-1]
    n_grp = d_s5.shape[-2]
    o_dt = d_ssd_w + d_xbc
    o_u = o_dt + n_heads
    row = lambda v: v.reshape(1, -1).astype(F32)

    wi = w_in[0]
    wz = wi[:, :d_ssd_w].astype(BF16)
    wxbc = wi[:, d_ssd_w:o_dt].astype(BF16)
    wdt = jnp.pad(wi[:, o_dt:o_u], ((0, 0), (0, LANES - n_heads))).astype(BF16)
    wu = wi[:, o_u:].astype(BF16)
    pad_heads = lambda v: jnp.pad(row(v), ((0, 0), (0, LANES - n_heads)))
    convw, convb, gmix = conv_w[0].astype(F32), row(conv_b[0]), row(g_mix[0])

    x_rows = x.reshape(bsz * seq, d_model)
    meta_chunk = jnp.concatenate([jnp.zeros((SSD_CHUNK - N_META, d_model), x.dtype), meta_tokens.astype(x.dtype)], axis=0)
    tm = _row_tile(seq, 1024)
    no_hist = jnp.zeros(((SSD_CONV - 1) * PLANE, d_xbc), F32)
    z_m, xa_m, bc_m, dt_m, f_m, tail_m = _in_proj(meta_chunk, no_hist, gmix, wz, wxbc, wdt, wu,
                                                  convw, convb, SSD_CHUNK, 1)
    z_r, xa_r, bc_r, dt_r, f_r, _ = _in_proj(x_rows, tail_m, gmix, wz, wxbc, wdt, wu, convw, convb, tm, seq // tm)

    seq3 = lambda a, b: a.reshape(b, -1, a.shape[-1])
    expand = (jnp.arange(d_ssd_w)[None, :] // SSD_HEAD_DIM == jnp.arange(LANES)[:, None]).astype(BF16)
    ssd_args = (pad_heads(dt_bias[0]), pad_heads(a_log[0]), row(jnp.repeat(d_ssd[0], SSD_HEAD_DIM)), row(g_ssd[0]), expand)
    no_state = jnp.zeros((SSD_GROUPS, SSD_STATE, d_ssd_w // SSD_GROUPS), F32)
    _, s_meta = _ssd(seq3(xa_m, 1), seq3(bc_m, 1), seq3(z_m, 1), seq3(dt_m, 1), no_state, *ssd_args,
                     padded_rows=SSD_CHUNK - N_META, chunks_per_step=1)
    y_ssd, _ = _ssd(seq3(xa_r, bsz), seq3(bc_r, bsz), seq3(z_r, bsz), seq3(dt_r, bsz), s_meta, *ssd_args, padded_rows=0,
                    chunks_per_step=2)

    t, gs, p = S5_BLOCK, S5_PREP_GROUPS, S5_STATE
    lanes = lambda v: v.astype(F32).reshape(n_grp // gs, 1, gs * p)
    per_grp = lambda v: (v.astype(F32).reshape(n_grp // gs, gs, S5_WIDTH, p).transpose(0, 2, 1, 3)
                         .reshape(n_grp // gs, S5_WIDTH, gs * p))
    mt, wc, ec, dp_re, dp_im = _s5_prep(lanes(lam_re[0]), lanes(lam_im[0]), lanes(jnp.repeat(log_step[0], p)),
                                        per_grp(jnp.swapaxes(b_re[0], 1, 2)), per_grp(jnp.swapaxes(b_im[0], 1, 2)),
                                        per_grp(c_re[0]), per_grp(c_im[0]))
    dsk = jnp.tile(d_s5[0].astype(F32), (1, t)).reshape(n_grp, 1, S5_FLAT)
    yg = _s5_core(f_r, f_m, mt, wc, ec, dp_re, dp_im, dsk, bsz)
    y_s5 = _s5_glu(yg, w_glu[0].astype(BF16), row(b_glu[0]), row(g_s5[0]), tm)

    wo = w_out[0].astype(BF16)
    out = _out_mlp(x_rows, y_ssd.reshape(bsz * seq, d_ssd_w), y_s5, wo[:d_ssd_w], wo[d_ssd_w:], row(g_mlp[0]),
                   w_up[0].astype(BF16), w_down[0].astype(BF16), row(g_final), tm, 1024)
    return out.reshape(bsz, seq, d_model).astype(x.dtype)
```

```python
import functools

import jax
import jax.numpy as jnp
from jax import lax
from jax.experimental import pallas as pl
from jax.experimental.pallas import tpu as pltpu

F32 = jnp.float32
BF16 = jnp.bfloat16

NORM_EPS = 1e-5
N_META = 16
SSD_CHUNK = 256
SSD_SUB = 128
SSD_HEAD_DIM = 64
SSD_GROUPS = 2
SSD_STATE = 128
SSD_CONV = 4
S5_WIDTH = 16
S5_STATE = 64
S5_BLOCK = 16
S5_FLAT = S5_BLOCK * S5_WIDTH
S5_PREP_GROUPS = 8
PLANE = SSD_CHUNK // S5_BLOCK
LANES = 128
SUBLANES = 8
LANE_BLOCKS = LANES // S5_WIDTH
COL_TILE = 256
VMEM_LIMIT_BYTES = 58 * 1024 * 1024

assert PLANE == S5_BLOCK == 2 * SUBLANES
assert LANES == 2 * SSD_HEAD_DIM


def _rmsnorm(x, g):
    return x * lax.rsqrt(jnp.mean(x * x, axis=-1, keepdims=True) + NORM_EPS) * g


def _silu(x):
    return x * jax.nn.sigmoid(x)


def _gelu(x):
    return 0.5 * x * (1.0 + lax.erf(x * (2.0 ** -0.5)))


def _dot(a, b):
    return jnp.dot(a, b, preferred_element_type=F32)


def _dot_nt(a, b, **kw):
    return lax.dot_general(a, b, (((1,), (1,)), ((), ())), preferred_element_type=F32, **kw)


def _dot_tn(a, b):
    return lax.dot_general(a, b, (((0,), (0,)), ((), ())), preferred_element_type=F32)


def _dot_split(a, m01, parts):
    acc = None
    rem = a
    for _ in range(parts):
        piece = rem.astype(BF16)
        rem = rem - piece.astype(F32)
        term = _dot(piece, m01)
        acc = term if acc is None else acc + term
    return acc


def _params(sem):
    return pltpu.CompilerParams(dimension_semantics=sem, vmem_limit_bytes=VMEM_LIMIT_BYTES)


def _const_spec(shape, single_buffer=False):
    nd = len(shape)
    mode = pl.Buffered(1) if single_buffer else None
    return pl.BlockSpec(shape, lambda *_: (0,) * nd, pipeline_mode=mode)


def _token_of_row(r):
    return (r % PLANE) * S5_BLOCK + r // PLANE


def _transpose8(arrays, pos, axis, unit):
    a = list(arrays)
    size = 8 * unit
    d = 4
    while d:
        keep_lo = (pos & (d * unit)) == 0
        for i in range(8):
            if i & d:
                continue
            lo, hi = a[i], a[i + d]
            a[i] = jnp.where(keep_lo, lo, pltpu.roll(hi, d * unit, axis=axis))
            a[i + d] = jnp.where(keep_lo, pltpu.roll(lo, size - d * unit, axis=axis), hi)
        d //= 2
    return a


def _transpose_lane_blocks(planes, lane):
    return _transpose8(planes, lane, 1, S5_WIDTH)


def _swap_row_order(v):
    sub = lax.broadcasted_iota(jnp.int32, (SUBLANES, v.shape[1]), 0)
    chunks = []
    for chunk in range(v.shape[0] // SSD_CHUNK):
        out = [[None, None] for _ in range(S5_BLOCK)]
        for ah in range(2):
            for bh in range(2):
                first = [chunk * SSD_CHUNK + (SUBLANES * ah + al) * PLANE + SUBLANES * bh for al in range(SUBLANES)]
                src = [v[r:r + SUBLANES] for r in first]
                dst = _transpose8(src, sub, 0, 1)
                for bl in range(SUBLANES):
                    out[SUBLANES * bh + bl][ah] = dst[bl]
        chunks.append(jnp.concatenate([piece for pair in out for piece in pair], axis=0))
    return jnp.concatenate(chunks, axis=0) if len(chunks) > 1 else chunks[0]


def _in_proj_kernel(x_ref, hist_ref, g_ref, wz_ref, wxbc_ref, wdt_ref, wu_ref, convw_ref, convb_ref,
                    z_ref, xa_ref, bc_ref, dt_ref, f_ref, tail_ref, hist_scr, *, tiles_per_seq):
    tm = z_ref.shape[0]
    d_x = xa_ref.shape[1]
    n_hist = (SSD_CONV - 1) * PLANE
    n_chunks = tm // SSD_CHUNK

    @pl.when(lax.rem(pl.program_id(0), tiles_per_seq) == 0)
    def _():
        hist_scr[...] = hist_ref[...]

    n = _rmsnorm(_swap_row_order(x_ref[...]), g_ref[...]).astype(BF16)

    lane = lax.broadcasted_iota(jnp.int32, (PLANE, LANES), 1)
    for ct in range(wu_ref.shape[1] // COL_TILE):
        u = _dot(n, wu_ref[:, ct * COL_TILE:(ct + 1) * COL_TILE])
        for chunk in range(n_chunks):
            for o in range(COL_TILE // LANES):
                oct_ = ct * (COL_TILE // LANES) + o
                for half in range(S5_BLOCK // LANE_BLOCKS):
                    tok = []
                    for s in range(half * LANE_BLOCKS, (half + 1) * LANE_BLOCKS):
                        lo = chunk * SSD_CHUNK + s * PLANE
                        tok.append(u[lo:lo + PLANE, o * LANES:(o + 1) * LANES])
                    grp = _transpose_lane_blocks(tok, lane)
                    for gi in range(LANE_BLOCKS):
                        f_ref[oct_ * LANE_BLOCKS + gi, chunk * PLANE:(chunk + 1) * PLANE,
                              half * LANES:(half + 1) * LANES] = grp[gi].astype(BF16)

    row0 = lax.broadcasted_iota(jnp.int32, (PLANE, 1), 0) == 0
    for ct in range(wxbc_ref.shape[1] // COL_TILE):
        cols = slice(ct * COL_TILE, (ct + 1) * COL_TILE)
        pre = _dot(n, wxbc_ref[:, cols])
        w = [convw_ref[k:k + 1, cols] for k in range(SSD_CONV)]
        prev = hist_scr[:, cols]
        for chunk in range(n_chunks):
            cur = pre[chunk * SSD_CHUNK:(chunk + 1) * SSD_CHUNK]
            last = cur[SSD_CHUNK - n_hist:]
            wrapped = [jnp.where(row0, pltpu.roll(prev[m * PLANE:(m + 1) * PLANE], 1, axis=0),
                                 pltpu.roll(last[m * PLANE:(m + 1) * PLANE], 1, axis=0)) for m in range(SSD_CONV - 1)]
            acc = convb_ref[:, cols] + w[SSD_CONV - 1] * cur
            for k in range(1, SSD_CONV):
                shifted = jnp.concatenate(wrapped[SSD_CONV - 1 - k:] + [cur[:SSD_CHUNK - k * PLANE]], axis=0)
                acc = acc + w[SSD_CONV - 1 - k] * shifted
            act = _silu(acc)
            rows = slice(chunk * SSD_CHUNK, (chunk + 1) * SSD_CHUNK)
            if ct * COL_TILE < d_x:
                xa_ref[rows, cols] = act
            else:
                bc_ref[rows, ct * COL_TILE - d_x:(ct + 1) * COL_TILE - d_x] = act.astype(BF16)
            prev = last
        hist_scr[:, cols] = prev
        tail_ref[:, cols] = prev

    for ct in range(wz_ref.shape[1] // COL_TILE):
        cols = slice(ct * COL_TILE, (ct + 1) * COL_TILE)
        z_ref[:, cols] = _dot(n, wz_ref[:, cols])
    dt_ref[...] = _dot(n, wdt_ref[...])


def _in_proj(x_rows, hist, g, wz, wxbc, wdt, wu, convw, convb, tm, tiles_per_seq):
    rows, d = x_rows.shape
    d_z, d_xbc, d_u = wz.shape[1], wxbc.shape[1], wu.shape[1]
    d_bc = d_xbc - d_z
    n_grp = d_u // S5_WIDTH
    nb = tm // S5_BLOCK
    row_spec = lambda w: pl.BlockSpec((tm, w), lambda i: (i, 0))
    return pl.pallas_call(
        functools.partial(_in_proj_kernel, tiles_per_seq=tiles_per_seq),
        grid=(rows // tm,),
        in_specs=[row_spec(d)]
        + [_const_spec(a.shape, single_buffer=True) for a in (hist, g, wz, wxbc, wdt, wu, convw, convb)],
        out_specs=[row_spec(d_z), row_spec(d_z), row_spec(d_bc), row_spec(LANES),
                   pl.BlockSpec((n_grp, nb, S5_FLAT), lambda i: (0, i, 0)), _const_spec(hist.shape)],
        out_shape=[jax.ShapeDtypeStruct((rows, d_z), F32), jax.ShapeDtypeStruct((rows, d_z), F32),
                   jax.ShapeDtypeStruct((rows, d_bc), BF16), jax.ShapeDtypeStruct((rows, LANES), F32),
                   jax.ShapeDtypeStruct((n_grp, rows // S5_BLOCK, S5_FLAT), BF16),
                   jax.ShapeDtypeStruct(hist.shape, F32)],
        scratch_shapes=[pltpu.VMEM(hist.shape, F32)],
        compiler_params=_params(("arbitrary",)),
        name="in_proj",
    )(x_rows, hist, g, wz, wxbc, wdt, wu, convw, convb)


def _ssd_chunk(xa_ref, bc_ref, z_ref, dt_ref, rows, dtb, alog, dskip_ref, g_ref, expand_ref, st_ref, yd_ref,
               padded_rows):
    q = SSD_SUB
    d_ssd = xa_ref.shape[-1]
    gw = d_ssd // SSD_GROUPS
    sub_planes = q // S5_BLOCK

    def half_of(v, half):
        return jnp.concatenate([v[s * PLANE + half * sub_planes:s * PLANE + (half + 1) * sub_planes]
                                for s in range(S5_BLOCK)], axis=0)

    def half_ref(ref, half, cols):
        first = rows.start + half * sub_planes
        return jnp.concatenate([ref[first + s * PLANE:first + s * PLANE + sub_planes, cols]
                                for s in range(S5_BLOCK)], axis=0)

    def token_of_row(r):
        return (r % sub_planes) * S5_BLOCK + r // sub_planes

    bc = bc_ref[rows, :].astype(F32)
    t_r = token_of_row(lax.broadcasted_iota(jnp.int32, (q, q), 0))
    t_c = token_of_row(lax.broadcasted_iota(jnp.int32, (q, q), 1))
    causal = t_r >= t_c
    tri = jnp.where(t_c >= t_r, 1.0, 0.0).astype(BF16)
    first_head = lax.broadcasted_iota(jnp.int32, (q, LANES), 1) < SSD_HEAD_DIM
    n_half = SSD_CHUNK // q
    n_tile = d_ssd // COL_TILE

    def prepare(half):
        bc_h = half_of(bc, half).astype(BF16)
        dt = jax.nn.softplus(half_ref(dt_ref, half, slice(None)) + dtb)
        if padded_rows:
            t_row = half * q + token_of_row(lax.broadcasted_iota(jnp.int32, dt.shape, 0))
            dt = jnp.where(t_row < padded_rows, 0.0, dt)
        d_a = dt * (-jnp.exp(alog))
        a_cs_t = _dot_split(d_a.T, tri, 3)
        return bc_h[:, :SSD_GROUPS * SSD_STATE], bc_h[:, SSD_GROUPS * SSD_STATE:], dt, a_cs_t.T, a_cs_t

    def expand_tile(half, t, dt, a_cs):
        cols = slice(t * COL_TILE, (t + 1) * COL_TILE)
        expand = expand_ref[:, cols]
        dt_x = _dot_split(dt, expand, 2)
        a_x = _dot_split(a_cs, expand, 2)
        a_last = a_x[q - 1:q, :]
        x_in = half_ref(xa_ref, half, cols)
        xdt = x_in * dt_x
        xw_b = (xdt * jnp.exp(a_last - a_x)).astype(BF16)
        return x_in, xdt.astype(BF16), xw_b, jnp.exp(a_x), jnp.exp(a_last)

    halves = [prepare(half) for half in range(n_half)]
    steps = [(half, t) for half in range(n_half) for t in range(n_tile)]
    expanded = expand_tile(*steps[0], halves[0][2], halves[0][3])
    ssq = [jnp.zeros((q, 1), F32) for _ in range(n_half)]
    cb = None
    for i, (half, t) in enumerate(steps):
        b_in, c_in, _, a_cs, a_cs_t = halves[half]
        x_in, xdt_b, xw_b, ea_x, ea_last = expanded
        if i + 1 < len(steps):
            nh, nt = steps[i + 1]
            expanded = expand_tile(nh, nt, halves[nh][2], halves[nh][3])
        cols = slice(t * COL_TILE, (t + 1) * COL_TILE)
        g = (t * COL_TILE) // gw
        gcols = slice(t * COL_TILE - g * gw, (t + 1) * COL_TILE - g * gw)
        cg = c_in[:, g * SSD_STATE:(g + 1) * SSD_STATE]
        bg = b_in[:, g * SSD_STATE:(g + 1) * SSD_STATE]
        if (t * COL_TILE) % gw == 0:
            cb = _dot_nt(cg, bg)
        s_prev = st_ref[g, :, gcols]
        y_off = _dot(cg, s_prev.astype(BF16)) * ea_x
        pairs = []
        for pair in range(COL_TILE // LANES):
            lo = pair * LANES
            att = []
            for h in (2 * (t * (COL_TILE // LANES) + pair), 2 * (t * (COL_TILE // LANES) + pair) + 1):
                seg = a_cs[:, h:h + 1] - a_cs_t[h:h + 1, :]
                att.append((cb * jnp.exp(jnp.where(causal, seg, -jnp.inf))).astype(BF16))
            x_pair = xdt_b[:, lo:lo + LANES]
            zero = jnp.zeros_like(x_pair)
            rhs = jnp.concatenate([jnp.where(first_head, x_pair, zero), jnp.where(first_head, zero, x_pair)], axis=0)
            pairs.append(_dot(jnp.concatenate(att, axis=1), rhs) + y_off[:, lo:lo + LANES])
        y = jnp.concatenate(pairs, axis=1) + x_in * dskip_ref[:, cols]
        y = y * _silu(half_ref(z_ref, half, cols))
        for s in range(S5_BLOCK):
            yd_ref[s * PLANE + half * sub_planes:s * PLANE + (half + 1) * sub_planes, cols] = (
                y[s * sub_planes:(s + 1) * sub_planes])
        ssq[half] = ssq[half] + jnp.sum(y * y, axis=-1, keepdims=True)
        st_ref[g, :, gcols] = s_prev * ea_last + _dot_tn(bg, xw_b)

    ssq = jnp.concatenate([ssq[half][s * sub_planes:(s + 1) * sub_planes]
                           for s in range(S5_BLOCK) for half in range(n_half)], axis=0)
    scale = lax.rsqrt(ssq * (1.0 / d_ssd) + NORM_EPS)
    return (yd_ref[...] * scale * g_ref[...]).astype(BF16)


def _ssd_kernel(xa_ref, bc_ref, z_ref, dt_ref, s0_ref, dtb_ref, alog_ref, dskip_ref, g_ref, expand_ref,
                y_ref, s_out_ref, st_ref, yd_ref, *, padded_rows):
    @pl.when(pl.program_id(1) == 0)
    def _():
        st_ref[...] = s0_ref[...]

    for c in range(xa_ref.shape[1] // SSD_CHUNK):
        rows = slice(c * SSD_CHUNK, (c + 1) * SSD_CHUNK)
        y_ref[0, rows, :] = _ssd_chunk(xa_ref.at[0], bc_ref.at[0], z_ref.at[0], dt_ref.at[0], rows, dtb_ref[...],
                                       alog_ref[...], dskip_ref, g_ref, expand_ref, st_ref, yd_ref, padded_rows)
    s_out_ref[...] = st_ref[...]


def _ssd(xa, bc, z, dt, s0, dtb, alog, dskip, g, expand, padded_rows, chunks_per_step):
    bsz, seq, d_ssd = xa.shape
    rows = SSD_CHUNK * chunks_per_step
    chunk = lambda w: pl.BlockSpec((1, rows, w), lambda b, c: (b, c, 0))
    return pl.pallas_call(
        functools.partial(_ssd_kernel, padded_rows=padded_rows),
        grid=(bsz, seq // rows),
        in_specs=[chunk(d_ssd), chunk(bc.shape[-1]), chunk(d_ssd), chunk(LANES)]
        + [_const_spec(a.shape) for a in (s0, dtb, alog, dskip, g, expand)],
        out_specs=[chunk(d_ssd), _const_spec(s0.shape)],
        out_shape=[jax.ShapeDtypeStruct((bsz, seq, d_ssd), BF16), jax.ShapeDtypeStruct(s0.shape, F32)],
        scratch_shapes=[pltpu.VMEM(s0.shape, F32), pltpu.VMEM((SSD_CHUNK, d_ssd), F32)],
        compiler_params=_params(("arbitrary", "arbitrary")),
        name="ssd",
    )(xa, bc, z, dt, s0, dtb, alog, dskip, g, expand)


def _lam_pow(lr, li, step, k):
    kf = k.astype(F32)
    mag = jnp.exp(kf * (lr * step))
    ang = kf * (li * step)
    return mag * jnp.cos(ang), mag * jnp.sin(ang)


def _s5_prep_kernel(lr_ref, li_ref, ls_ref, bt_re_ref, bt_im_ref, c_re_ref, c_im_ref,
                    mt_ref, wc_ref, ec_ref, dp_re_ref, dp_im_ref):
    t, hw, p = S5_BLOCK, S5_WIDTH, S5_STATE
    lr, li = lr_ref[0], li_ref[0]
    step = jnp.exp(ls_ref[0])
    k_up = lax.broadcasted_iota(jnp.int32, (3 * SUBLANES, 1), 0)
    pw_re, pw_im = _lam_pow(lr, li, step, k_up)
    k_dn = (t - 1) - lax.broadcasted_iota(jnp.int32, (t, 1), 0)
    pr_re, pr_im = _lam_pow(lr, li, step, k_dn)
    k_blk = t * lax.broadcasted_iota(jnp.int32, (2 * SUBLANES, 1), 0)
    dp_re, dp_im = _lam_pow(lr, li, step, k_blk)

    ab_re, ab_im = pw_re[1:2, :], pw_im[1:2, :]
    den = lr * lr + li * li
    coef_re = ((ab_re - 1.0) * lr + ab_im * li) / den
    coef_im = (ab_im * lr - (ab_re - 1.0) * li) / den
    bt_re, bt_im = bt_re_ref[0], bt_im_ref[0]
    bb_re = coef_re * bt_re - coef_im * bt_im
    bb_im = coef_re * bt_im + coef_im * bt_re
    c_re, c_im = c_re_ref[0], c_im_ref[0]

    def rep(m):
        return jnp.broadcast_to(m[:, None, :], (t, hw, p)).reshape(t * hw, p)

    def tile(m):
        return jnp.concatenate([m] * t, axis=0)

    hi = lax.Precision.HIGHEST
    lane = lax.broadcasted_iota(jnp.int32, (hw, t * hw), 1)
    for a in range(S5_PREP_GROUPS):
        sl = slice(a * p, (a + 1) * p)
        ct_re, ct_im = tile(c_re[:, sl]), tile(c_im[:, sl])
        l0_re, l0_im = rep(pw_re[0:t, sl]), rep(pw_im[0:t, sl])
        l1_re, l1_im = rep(pw_re[1:t + 1, sl]), rep(pw_im[1:t + 1, sl])
        x0_re = l0_re * ct_re - l0_im * ct_im
        x0_im = l0_re * ct_im + l0_im * ct_re
        kt = _dot_nt(bb_re[:, sl], x0_re, precision=hi) - _dot_nt(bb_im[:, sl], x0_im, precision=hi)
        for s in range(t):
            blk = kt if s == 0 else jnp.where(lane >= s * hw, pltpu.roll(kt, s * hw, axis=1), 0.0)
            mt_ref[a, s * hw:(s + 1) * hw, :] = blk.astype(mt_ref.dtype)
        x1_re = l1_re * ct_re - l1_im * ct_im
        x1_im = l1_re * ct_im + l1_im * ct_re
        ec_ref[a] = jnp.concatenate([x1_re, -x1_im], axis=1).T.astype(ec_ref.dtype)
        lw_re, lw_im = rep(pr_re[:, sl]), rep(pr_im[:, sl])
        bt_re_a, bt_im_a = tile(bb_re[:, sl]), tile(bb_im[:, sl])
        wc_ref[a] = jnp.concatenate([lw_re * bt_re_a - lw_im * bt_im_a, lw_re * bt_im_a + lw_im * bt_re_a],
                                    axis=1).astype(wc_ref.dtype)
        dp_re_ref[a] = jnp.concatenate([dp_re[:, sl]] * 2, axis=1)
        dp_im_ref[a] = jnp.concatenate([dp_im[:, sl]] * 2, axis=1)


def _s5_prep(lr, li, ls, bt_re, bt_im, c_re, c_im):
    steps = lr.shape[0]
    gs, p, fl = S5_PREP_GROUPS, S5_STATE, S5_FLAT
    n_grp = steps * gs
    in_spec = lambda a: pl.BlockSpec((1,) + a.shape[1:], lambda i: (i, 0, 0))
    out_spec = lambda *s: pl.BlockSpec((gs,) + s, lambda i: (i, 0, 0))
    out = lambda *s, dt=BF16: jax.ShapeDtypeStruct((n_grp,) + s, dt)
    args = (lr, li, ls, bt_re, bt_im, c_re, c_im)
    return pl.pallas_call(
        _s5_prep_kernel,
        grid=(steps,),
        in_specs=[in_spec(a) for a in args],
        out_specs=[out_spec(fl, fl), out_spec(fl, 2 * p), out_spec(2 * p, fl), out_spec(2 * SUBLANES, 2 * p),
                   out_spec(2 * SUBLANES, 2 * p)],
        out_shape=[out(fl, fl), out(fl, 2 * p), out(2 * p, fl), out(2 * SUBLANES, 2 * p, dt=F32),
                   out(2 * SUBLANES, 2 * p, dt=F32)],
        compiler_params=_params(("parallel",)),
        name="s5_prep",
    )(*args)


def _s5_core_kernel(f_r_ref, f_m_ref, mt_ref, wc_ref, ec_ref, dp_re_ref, dp_im_ref, dskip_ref,
                    y_ref, wu_re_ref, wu_im_ref, s_re_ref, s_im_ref, *, n_batch):
    p = S5_STATE
    rows_m = f_m_ref.shape[1]
    rows_r = f_r_ref.shape[1] // n_batch
    per_batch = rows_m + rows_r
    n_tiles = per_batch // SUBLANES
    lane_lo = lax.broadcasted_iota(jnp.int32, (1, 2 * p), 1) < p

    def pack(a, b):
        return jnp.where(lane_lo, a, pltpu.roll(b, p, axis=1)), jnp.where(lane_lo, pltpu.roll(a, p, axis=1), b)

    wu_m = [_dot(f_m_ref[a], wc_ref[a]) for a in range(2)]
    wu_r = [_dot(f_r_ref[a], wc_ref[a]) for a in range(2)]
    m_re, m_im = pack(*wu_m)
    r_re, r_im = pack(*wu_r)
    for b in range(n_batch):
        wu_re_ref[b * per_batch:b * per_batch + rows_m, :] = m_re
        wu_im_ref[b * per_batch:b * per_batch + rows_m, :] = m_im
        wu_re_ref[b * per_batch + rows_m:(b + 1) * per_batch, :] = r_re[b * rows_r:(b + 1) * rows_r]
        wu_im_ref[b * per_batch + rows_m:(b + 1) * per_batch, :] = r_im[b * rows_r:(b + 1) * rows_r]

    dp_re = jnp.where(lane_lo, dp_re_ref[0], dp_re_ref[1])
    dp_im = jnp.where(lane_lo, dp_im_ref[0], dp_im_ref[1])
    sub = lax.broadcasted_iota(jnp.int32, (SUBLANES, 2 * p), 0)

    def cmul_add(a_re, a_im, d_re, d_im, x_re, x_im):
        return a_re + d_re * x_re - d_im * x_im, a_im + d_re * x_im + d_im * x_re

    def shift_down(x, k):
        return jnp.where(sub >= k, pltpu.roll(x, k, axis=0), 0.0)

    steps = {k: (jnp.where(sub >= k, dp_re[k:k + 1, :], 0.0), jnp.where(sub >= k, dp_im[k:k + 1, :], 0.0))
             for k in (1, 2, 4)}

    def tile_step(i, carry):
        new = []
        for b in range(n_batch):
            c_re, c_im = carry[2 * b], carry[2 * b + 1]
            base = pl.multiple_of(b * per_batch + i * SUBLANES, SUBLANES)
            v_re = wu_re_ref[pl.ds(base, SUBLANES), :]
            v_im = wu_im_ref[pl.ds(base, SUBLANES), :]
            for k, (m_re, m_im) in steps.items():
                v_re, v_im = cmul_add(v_re, v_im, m_re, m_im, pltpu.roll(v_re, k, axis=0), pltpu.roll(v_im, k, axis=0))
            e_re, e_im = cmul_add(shift_down(v_re, 1), shift_down(v_im, 1), dp_re[0:SUBLANES, :],
                                  dp_im[0:SUBLANES, :], c_re, c_im)
            s_re_ref[pl.ds(base, SUBLANES), :] = e_re
            s_im_ref[pl.ds(base, SUBLANES), :] = e_im
            n_re, n_im = cmul_add(v_re[SUBLANES - 1:, :], v_im[SUBLANES - 1:, :], dp_re[SUBLANES:SUBLANES + 1, :],
                                  dp_im[SUBLANES:SUBLANES + 1, :], c_re, c_im)
            new += [n_re, n_im]
        return tuple(new)

    zero = jnp.zeros((1, 2 * p), F32)
    lax.fori_loop(0, n_tiles, tile_step, (zero,) * (2 * n_batch), unroll=True)

    for b in range(n_batch):
        lo = b * per_batch + rows_m
        s_a, s_b = pack(s_re_ref[lo:lo + rows_r, :], s_im_ref[lo:lo + rows_r, :])
        for a, s_ab in enumerate((s_a, s_b)):
            f = f_r_ref[a, b * rows_r:(b + 1) * rows_r, :]
            y = _dot(f, mt_ref[a]) + _dot(s_ab.astype(BF16), ec_ref[a]) + f.astype(F32) * dskip_ref[a]
            y_ref[a, b * rows_r:(b + 1) * rows_r, :] = _gelu(y).astype(y_ref.dtype)


def _s5_core(f_r, f_m, mt, wc, ec, dp_re, dp_im, dskip, n_batch):
    n_grp, rows_r, fl = f_r.shape
    rows_m = f_m.shape[1]
    p2 = 2 * S5_STATE
    rows = n_batch * rows_m + rows_r
    spec = lambda *s: pl.BlockSpec((2,) + s, lambda i: (i, 0, 0))
    return pl.pallas_call(
        functools.partial(_s5_core_kernel, n_batch=n_batch),
        grid=(n_grp // 2,),
        in_specs=[spec(rows_r, fl), spec(rows_m, fl), spec(fl, fl), spec(fl, p2), spec(p2, fl),
                  spec(2 * SUBLANES, p2), spec(2 * SUBLANES, p2), spec(1, fl)],
        out_specs=spec(rows_r, fl),
        out_shape=jax.ShapeDtypeStruct((n_grp, rows_r, fl), BF16),
        scratch_shapes=[pltpu.VMEM((rows, p2), F32)] * 4,
        compiler_params=_params(("parallel",)),
        name="s5_core",
    )(f_r, f_m, mt, wc, ec, dp_re, dp_im, dskip)


def _s5_glu_kernel(yg_ref, w_ref, b_ref, g_ref, o_ref):
    d = o_ref.shape[-1]
    tm = o_ref.shape[0]
    lane = lax.broadcasted_iota(jnp.int32, (PLANE, LANES), 1)
    chunks = []
    for chunk in range(tm // SSD_CHUNK):
        planes = [[None] * (d // LANES) for _ in range(S5_BLOCK)]
        for oct_ in range(d // LANES):
            for half in range(S5_BLOCK // LANE_BLOCKS):
                grp = [yg_ref[oct_ * LANE_BLOCKS + gi, chunk * PLANE:(chunk + 1) * PLANE,
                              half * LANES:(half + 1) * LANES].astype(F32) for gi in range(LANE_BLOCKS)]
                tok = _transpose_lane_blocks(grp, lane)
                for s8 in range(LANE_BLOCKS):
                    planes[half * LANE_BLOCKS + s8][oct_] = tok[s8].astype(BF16)
        chunks.append(jnp.concatenate([jnp.concatenate(pieces, axis=1) for pieces in planes], axis=0))
    y = jnp.concatenate(chunks, axis=0) if len(chunks) > 1 else chunks[0]
    v = _dot(y, w_ref[...]) + b_ref[...]
    o = v[:, :d] * jax.nn.sigmoid(v[:, d:])
    o_ref[...] = _rmsnorm(o, g_ref[...]).astype(o_ref.dtype)


def _s5_glu(yg, w, b, g, tm):
    n_grp, blocks, fl = yg.shape
    rows, d = blocks * S5_BLOCK, n_grp * S5_WIDTH
    nb = tm // S5_BLOCK
    return pl.pallas_call(
        _s5_glu_kernel,
        grid=(rows // tm,),
        in_specs=[pl.BlockSpec((n_grp, nb, fl), lambda i: (0, i, 0))]
        + [_const_spec(a.shape, single_buffer=True) for a in (w, b, g)],
        out_specs=pl.BlockSpec((tm, d), lambda i: (i, 0)),
        out_shape=jax.ShapeDtypeStruct((rows, d), BF16),
        compiler_params=_params(("parallel",)),
        name="s5_glu",
    )(yg, w, b, g)


def _out_mlp_kernel(x_ref, ya_ref, yb_ref, woa_ref, wob_ref, gm_ref, wup_ref, wdn_ref, gf_ref, o_ref, *, ff_tile):
    h = x_ref[...] + _swap_row_order(_dot(ya_ref[...], woa_ref[...]) + _dot(yb_ref[...], wob_ref[...]))
    n = _rmsnorm(h, gm_ref[...]).astype(BF16)
    d_ff = wup_ref.shape[1]
    mlp = None
    for k in range(d_ff // ff_tile):
        m = _dot(n, wup_ref[:, k * ff_tile:(k + 1) * ff_tile])
        m = jnp.square(jnp.maximum(m, 0.0)).astype(BF16)
        part = _dot(m, wdn_ref[k * ff_tile:(k + 1) * ff_tile, :])
        mlp = part if mlp is None else mlp + part
    o_ref[...] = _rmsnorm(h + mlp, gf_ref[...])


def _out_mlp(x_rows, ya, yb, woa, wob, gm, wup, wdn, gf, tm, ff_tile):
    rows, d = ya.shape
    row_spec = pl.BlockSpec((tm, d), lambda i: (i, 0))
    return pl.pallas_call(
        functools.partial(_out_mlp_kernel, ff_tile=ff_tile),
        grid=(rows // tm,),
        in_specs=[row_spec, row_spec, row_spec]
        + [_const_spec(a.shape, single_buffer=True) for a in (woa, wob, gm, wup, wdn, gf)],
        out_specs=row_spec,
        out_shape=jax.ShapeDtypeStruct((rows, d), F32),
        compiler_params=_params(("parallel",)),
        name="out_mlp",
    )(x_rows, ya, yb, woa, wob, gm, wup, wdn, gf)


def _row_tile(rows, want):
    tm = min(rows, want)
    assert rows % tm == 0
    return tm


def kernel(x, meta_tokens, g_mix, w_in, conv_w, conv_b, dt_bias, a_log, d_ssd, g_ssd, lam_re, lam_im, log_step,
           b_re, b_im, c_re, c_im, d_s5, w_glu, b_glu, g_s5, w_out, g_mlp, w_up, w_down, g_final):
    bsz, seq, d_model = x.shape
    assert w_in.shape[0] == 1, "single-layer block: meta-token outputs are only consumed as state"
    assert seq % SSD_CHUNK == 0 and d_model % LANES == 0
    n_heads = dt_bias.shape[-1]
    d_ssd_w = n_heads * SSD_HEAD_DIM
    d_xbc = conv_w.shape[-1]
    n_grp = d_s5.shape[-2]
    o_dt = d_ssd_w + d_xbc
    o_u = o_dt + n_heads
    row = lambda v: v.reshape(1, -1).astype(F32)

    wi = w_in[0]
    wz = wi[:, :d_ssd_w].astype(BF16)
    wxbc = wi[:, d_ssd_w:o_dt].astype(BF16)
    wdt = jnp.pad(wi[:, o_dt:o_u], ((0, 0), (0, LANES - n_heads))).astype(BF16)
    wu = wi[:, o_u:].astype(BF16)
    pad_heads = lambda v: jnp.pad(row(v), ((0, 0), (0, LANES - n_heads)))
    convw, convb, gmix = conv_w[0].astype(F32), row(conv_b[0]), row(g_mix[0])

    x_rows = x.reshape(bsz * seq, d_model)
    meta_chunk = jnp.concatenate([jnp.zeros((SSD_CHUNK - N_META, d_model), x.dtype), meta_tokens.astype(x.dtype)], axis=0)
    tm = _row_tile(seq, 1024)
    no_hist = jnp.zeros(((SSD_CONV - 1) * PLANE, d_xbc), F32)
    z_m, xa_m, bc_m, dt_m, f_m, tail_m = _in_proj(meta_chunk, no_hist, gmix, wz, wxbc, wdt, wu,
                                                  convw, convb, SSD_CHUNK, 1)
    z_r, xa_r, bc_r, dt_r, f_r, _ = _in_proj(x_rows, tail_m, gmix, wz, wxbc, wdt, wu, convw, convb, tm, seq // tm)

    seq3 = lambda a, b: a.reshape(b, -1, a.shape[-1])
    expand = (jnp.arange(d_ssd_w)[None, :] // SSD_HEAD_DIM == jnp.arange(LANES)[:, None]).astype(BF16)
    ssd_args = (pad_heads(dt_bias[0]), pad_heads(a_log[0]), row(jnp.repeat(d_ssd[0], SSD_HEAD_DIM)), row(g_ssd[0]), expand)
    no_state = jnp.zeros((SSD_GROUPS, SSD_STATE, d_ssd_w // SSD_GROUPS), F32)
    _, s_meta = _ssd(seq3(xa_m, 1), seq3(bc_m, 1), seq3(z_m, 1), seq3(dt_m, 1), no_state, *ssd_args,
                     padded_rows=SSD_CHUNK - N_META, chunks_per_step=1)
    y_ssd, _ = _ssd(seq3(xa_r, bsz), seq3(bc_r, bsz), seq3(z_r, bsz), seq3(dt_r, bsz), s_meta, *ssd_args, padded_rows=0,
                    chunks_per_step=4)

    t, gs, p = S5_BLOCK, S5_PREP_GROUPS, S5_STATE
    lanes = lambda v: v.astype(F32).reshape(n_grp // gs, 1, gs * p)
    per_grp = lambda v: (v.astype(F32).reshape(n_grp // gs, gs, S5_WIDTH, p).transpose(0, 2, 1, 3)
                         .reshape(n_grp // gs, S5_WIDTH, gs * p))
    mt, wc, ec, dp_re, dp_im = _s5_prep(lanes(lam_re[0]), lanes(lam_im[0]), lanes(jnp.repeat(log_step[0], p)),
                                        per_grp(jnp.swapaxes(b_re[0], 1, 2)), per_grp(jnp.swapaxes(b_im[0], 1, 2)),
                                        per_grp(c_re[0]), per_grp(c_im[0]))
    dsk = jnp.tile(d_s5[0].astype(F32), (1, t)).reshape(n_grp, 1, S5_FLAT)
    yg = _s5_core(f_r, f_m, mt, wc, ec, dp_re, dp_im, dsk, bsz)
    y_s5 = _s5_glu(yg, w_glu[0].astype(BF16), row(b_glu[0]), row(g_s5[0]), tm)

    wo = w_out[0].astype(BF16)
    out = _out_mlp(x_rows, y_ssd.reshape(bsz * seq, d_ssd_w), y_s5, wo[:d_ssd_w], wo[d_ssd_w:], row(g_mlp[0]),
                   w_up[0].astype(BF16), w_down[0].astype(BF16), row(g_final), tm, 1024)
    return out.reshape(bsz, seq, d_model).astype(x.dtype)
```

```python
import functools

import jax
import jax.numpy as jnp
from jax import lax
from jax.experimental import pallas as pl
from jax.experimental.pallas import tpu as pltpu

F32 = jnp.float32
BF16 = jnp.bfloat16

NORM_EPS = 1e-5
N_META = 16
SSD_CHUNK = 256
SSD_SUB = 128
SSD_HEAD_DIM = 64
SSD_GROUPS = 2
SSD_STATE = 128
SSD_CONV = 4
S5_WIDTH = 16
S5_STATE = 64
S5_BLOCK = 16
S5_FLAT = S5_BLOCK * S5_WIDTH
S5_PREP_GROUPS = 8
PLANE = SSD_CHUNK // S5_BLOCK
LANES = 128
SUBLANES = 8
LANE_BLOCKS = LANES // S5_WIDTH
COL_TILE = 256
VMEM_LIMIT_BYTES = 58 * 1024 * 1024

assert PLANE == S5_BLOCK == 2 * SUBLANES
assert LANES == 2 * SSD_HEAD_DIM


def _rmsnorm(x, g):
    return x * lax.rsqrt(jnp.mean(x * x, axis=-1, keepdims=True) + NORM_EPS) * g


def _silu(x):
    return x * jax.nn.sigmoid(x)


def _gelu(x):
    return 0.5 * x * (1.0 + lax.erf(x * (2.0 ** -0.5)))


def _dot(a, b):
    return jnp.dot(a, b, preferred_element_type=F32)


def _dot_nt(a, b):
    return lax.dot_general(a, b, (((1,), (1,)), ((), ())), preferred_element_type=F32)


def _dot_nt_3x(a, b):
    a_hi, b_hi = a.astype(BF16), b.astype(BF16)
    a_lo = (a - a_hi.astype(F32)).astype(BF16)
    b_lo = (b - b_hi.astype(F32)).astype(BF16)
    return _dot_nt(a_hi, b_hi) + _dot_nt(a_hi, b_lo) + _dot_nt(a_lo, b_hi)


def _dot_tn(a, b):
    return lax.dot_general(a, b, (((0,), (0,)), ((), ())), preferred_element_type=F32)


def _dot_split(a, m01, parts):
    acc = None
    rem = a
    for _ in range(parts):
        piece = rem.astype(BF16)
        rem = rem - piece.astype(F32)
        term = _dot(piece, m01)
        acc = term if acc is None else acc + term
    return acc


def _params(sem):
    return pltpu.CompilerParams(dimension_semantics=sem, vmem_limit_bytes=VMEM_LIMIT_BYTES)


def _const_spec(shape, single_buffer=False):
    nd = len(shape)
    mode = pl.Buffered(1) if single_buffer else None
    return pl.BlockSpec(shape, lambda *_: (0,) * nd, pipeline_mode=mode)


def _token_of_row(r):
    return (r % PLANE) * S5_BLOCK + r // PLANE


def _transpose8(arrays, pos, axis, unit):
    a = list(arrays)
    size = 8 * unit
    d = 4
    while d:
        keep_lo = (pos & (d * unit)) == 0
        for i in range(8):
            if i & d:
                continue
            lo, hi = a[i], a[i + d]
            a[i] = jnp.where(keep_lo, lo, pltpu.roll(hi, d * unit, axis=axis))
            a[i + d] = jnp.where(keep_lo, pltpu.roll(lo, size - d * unit, axis=axis), hi)
        d //= 2
    return a


def _transpose_lane_blocks(planes, lane):
    return _transpose8(planes, lane, 1, S5_WIDTH)


def _swap_row_order(v):
    sub = lax.broadcasted_iota(jnp.int32, (SUBLANES, v.shape[1]), 0)
    chunks = []
    for chunk in range(v.shape[0] // SSD_CHUNK):
        out = [[None, None] for _ in range(S5_BLOCK)]
        for ah in range(2):
            for bh in range(2):
                first = [chunk * SSD_CHUNK + (SUBLANES * ah + al) * PLANE + SUBLANES * bh for al in range(SUBLANES)]
                src = [v[r:r + SUBLANES] for r in first]
                dst = _transpose8(src, sub, 0, 1)
                for bl in range(SUBLANES):
                    out[SUBLANES * bh + bl][ah] = dst[bl]
        chunks.append(jnp.concatenate([piece for pair in out for piece in pair], axis=0))
    return jnp.concatenate(chunks, axis=0) if len(chunks) > 1 else chunks[0]


def _in_proj_kernel(x_ref, hist_ref, g_ref, wz_ref, wxbc_ref, wdt_ref, wu_ref, convw_ref, convb_ref,
                    z_ref, xa_ref, bc_ref, dt_ref, f_ref, tail_ref, hist_scr, *, tiles_per_seq):
    tm = z_ref.shape[0]
    d_x = xa_ref.shape[1]
    n_hist = (SSD_CONV - 1) * PLANE
    n_chunks = tm // SSD_CHUNK

    @pl.when(lax.rem(pl.program_id(0), tiles_per_seq) == 0)
    def _():
        hist_scr[...] = hist_ref[...]

    n = _rmsnorm(_swap_row_order(x_ref[...]), g_ref[...]).astype(BF16)

    lane = lax.broadcasted_iota(jnp.int32, (PLANE, LANES), 1)
    for ct in range(wu_ref.shape[1] // COL_TILE):
        u = _dot(n, wu_ref[:, ct * COL_TILE:(ct + 1) * COL_TILE])
        for chunk in range(n_chunks):
            for o in range(COL_TILE // LANES):
                oct_ = ct * (COL_TILE // LANES) + o
                for half in range(S5_BLOCK // LANE_BLOCKS):
                    tok = []
                    for s in range(half * LANE_BLOCKS, (half + 1) * LANE_BLOCKS):
                        lo = chunk * SSD_CHUNK + s * PLANE
                        tok.append(u[lo:lo + PLANE, o * LANES:(o + 1) * LANES])
                    grp = _transpose_lane_blocks(tok, lane)
                    for gi in range(LANE_BLOCKS):
                        f_ref[oct_ * LANE_BLOCKS + gi, chunk * PLANE:(chunk + 1) * PLANE,
                              half * LANES:(half + 1) * LANES] = grp[gi].astype(BF16)

    row0 = lax.broadcasted_iota(jnp.int32, (PLANE, 1), 0) == 0
    for ct in range(wxbc_ref.shape[1] // COL_TILE):
        cols = slice(ct * COL_TILE, (ct + 1) * COL_TILE)
        pre = _dot(n, wxbc_ref[:, cols])
        w = [convw_ref[k:k + 1, cols] for k in range(SSD_CONV)]
        prev = hist_scr[:, cols]
        for chunk in range(n_chunks):
            cur = pre[chunk * SSD_CHUNK:(chunk + 1) * SSD_CHUNK]
            last = cur[SSD_CHUNK - n_hist:]
            wrapped = [jnp.where(row0, pltpu.roll(prev[m * PLANE:(m + 1) * PLANE], 1, axis=0),
                                 pltpu.roll(last[m * PLANE:(m + 1) * PLANE], 1, axis=0)) for m in range(SSD_CONV - 1)]
            acc = convb_ref[:, cols] + w[SSD_CONV - 1] * cur
            for k in range(1, SSD_CONV):
                shifted = jnp.concatenate(wrapped[SSD_CONV - 1 - k:] + [cur[:SSD_CHUNK - k * PLANE]], axis=0)
                acc = acc + w[SSD_CONV - 1 - k] * shifted
            act = _silu(acc)
            rows = slice(chunk * SSD_CHUNK, (chunk + 1) * SSD_CHUNK)
            if ct * COL_TILE < d_x:
                xa_ref[rows, cols] = act
            else:
                bc_ref[rows, ct * COL_TILE - d_x:(ct + 1) * COL_TILE - d_x] = act.astype(BF16)
            prev = last
        hist_scr[:, cols] = prev
        tail_ref[:, cols] = prev

    for ct in range(wz_ref.shape[1] // COL_TILE):
        cols = slice(ct * COL_TILE, (ct + 1) * COL_TILE)
        z_ref[:, cols] = _dot(n, wz_ref[:, cols])
    dt_ref[...] = _dot(n, wdt_ref[...])


def _in_proj(x_rows, hist, g, wz, wxbc, wdt, wu, convw, convb, tm, tiles_per_seq):
    rows, d = x_rows.shape
    d_z, d_xbc, d_u = wz.shape[1], wxbc.shape[1], wu.shape[1]
    d_bc = d_xbc - d_z
    n_grp = d_u // S5_WIDTH
    nb = tm // S5_BLOCK
    row_spec = lambda w: pl.BlockSpec((tm, w), lambda i: (i, 0))
    return pl.pallas_call(
        functools.partial(_in_proj_kernel, tiles_per_seq=tiles_per_seq),
        grid=(rows // tm,),
        in_specs=[row_spec(d)]
        + [_const_spec(a.shape, single_buffer=True) for a in (hist, g, wz, wxbc, wdt, wu, convw, convb)],
        out_specs=[row_spec(d_z), row_spec(d_z), row_spec(d_bc), row_spec(LANES),
                   pl.BlockSpec((n_grp, nb, S5_FLAT), lambda i: (0, i, 0)), _const_spec(hist.shape)],
        out_shape=[jax.ShapeDtypeStruct((rows, d_z), F32), jax.ShapeDtypeStruct((rows, d_z), F32),
                   jax.ShapeDtypeStruct((rows, d_bc), BF16), jax.ShapeDtypeStruct((rows, LANES), F32),
                   jax.ShapeDtypeStruct((n_grp, rows // S5_BLOCK, S5_FLAT), BF16),
                   jax.ShapeDtypeStruct(hist.shape, F32)],
        scratch_shapes=[pltpu.VMEM(hist.shape, F32)],
        compiler_params=_params(("arbitrary",)),
        name="in_proj",
    )(x_rows, hist, g, wz, wxbc, wdt, wu, convw, convb)


def _ssd_chunk(xa_ref, bc_ref, z_ref, dt_ref, rows, dtb, alog, dskip_ref, g_ref, expand_ref, st_ref, yd_ref,
               padded_rows):
    q = SSD_SUB
    d_ssd = xa_ref.shape[-1]
    gw = d_ssd // SSD_GROUPS
    sub_planes = q // S5_BLOCK

    def half_of(v, half):
        return jnp.concatenate([v[s * PLANE + half * sub_planes:s * PLANE + (half + 1) * sub_planes]
                                for s in range(S5_BLOCK)], axis=0)

    def half_ref(ref, half, cols):
        first = rows.start + half * sub_planes
        return jnp.concatenate([ref[first + s * PLANE:first + s * PLANE + sub_planes, cols]
                                for s in range(S5_BLOCK)], axis=0)

    def token_of_row(r):
        return (r % sub_planes) * S5_BLOCK + r // sub_planes

    bc = bc_ref[rows, :].astype(F32)
    t_r = token_of_row(lax.broadcasted_iota(jnp.int32, (q, q), 0))
    t_c = token_of_row(lax.broadcasted_iota(jnp.int32, (q, q), 1))
    causal = t_r >= t_c
    tri = jnp.where(t_c >= t_r, 1.0, 0.0).astype(BF16)
    first_head = lax.broadcasted_iota(jnp.int32, (q, LANES), 1) < SSD_HEAD_DIM
    n_half = SSD_CHUNK // q
    n_tile = d_ssd // COL_TILE

    def prepare(half):
        bc_h = half_of(bc, half).astype(BF16)
        dt = jax.nn.softplus(half_ref(dt_ref, half, slice(None)) + dtb)
        if padded_rows:
            t_row = half * q + token_of_row(lax.broadcasted_iota(jnp.int32, dt.shape, 0))
            dt = jnp.where(t_row < padded_rows, 0.0, dt)
        d_a = dt * (-jnp.exp(alog))
        a_cs_t = _dot_split(d_a.T, tri, 3)
        return bc_h[:, :SSD_GROUPS * SSD_STATE], bc_h[:, SSD_GROUPS * SSD_STATE:], dt, a_cs_t.T, a_cs_t

    def expand_tile(half, t, dt, a_cs):
        cols = slice(t * COL_TILE, (t + 1) * COL_TILE)
        expand = expand_ref[:, cols]
        dt_x = _dot_split(dt, expand, 2)
        a_x = _dot_split(a_cs, expand, 2)
        a_last = a_x[q - 1:q, :]
        x_in = half_ref(xa_ref, half, cols)
        xdt = x_in * dt_x
        xw_b = (xdt * jnp.exp(a_last - a_x)).astype(BF16)
        return x_in, xdt.astype(BF16), xw_b, jnp.exp(a_x), jnp.exp(a_last)

    halves = [prepare(half) for half in range(n_half)]
    steps = [(half, t) for half in range(n_half) for t in range(n_tile)]
    expanded = expand_tile(*steps[0], halves[0][2], halves[0][3])
    ssq = [jnp.zeros((q, 1), F32) for _ in range(n_half)]
    cb = None
    for i, (half, t) in enumerate(steps):
        b_in, c_in, _, a_cs, a_cs_t = halves[half]
        x_in, xdt_b, xw_b, ea_x, ea_last = expanded
        if i + 1 < len(steps):
            nh, nt = steps[i + 1]
            expanded = expand_tile(nh, nt, halves[nh][2], halves[nh][3])
        cols = slice(t * COL_TILE, (t + 1) * COL_TILE)
        g = (t * COL_TILE) // gw
        gcols = slice(t * COL_TILE - g * gw, (t + 1) * COL_TILE - g * gw)
        cg = c_in[:, g * SSD_STATE:(g + 1) * SSD_STATE]
        bg = b_in[:, g * SSD_STATE:(g + 1) * SSD_STATE]
        if (t * COL_TILE) % gw == 0:
            cb = _dot_nt(cg, bg)
        s_prev = st_ref[g, :, gcols]
        y_off = _dot(cg, s_prev.astype(BF16)) * ea_x
        pairs = []
        for pair in range(COL_TILE // LANES):
            lo = pair * LANES
            att = []
            for h in (2 * (t * (COL_TILE // LANES) + pair), 2 * (t * (COL_TILE // LANES) + pair) + 1):
                seg = a_cs[:, h:h + 1] - a_cs_t[h:h + 1, :]
                att.append((cb * jnp.exp(jnp.where(causal, seg, -jnp.inf))).astype(BF16))
            x_pair = xdt_b[:, lo:lo + LANES]
            zero = jnp.zeros_like(x_pair)
            rhs = jnp.concatenate([jnp.where(first_head, x_pair, zero), jnp.where(first_head, zero, x_pair)], axis=0)
            pairs.append(_dot(jnp.concatenate(att, axis=1), rhs) + y_off[:, lo:lo + LANES])
        y = jnp.concatenate(pairs, axis=1) + x_in * dskip_ref[:, cols]
        y = y * _silu(half_ref(z_ref, half, cols))
        for s in range(S5_BLOCK):
            yd_ref[s * PLANE + half * sub_planes:s * PLANE + (half + 1) * sub_planes, cols] = (
                y[s * sub_planes:(s + 1) * sub_planes])
        ssq[half] = ssq[half] + jnp.sum(y * y, axis=-1, keepdims=True)
        st_ref[g, :, gcols] = s_prev * ea_last + _dot_tn(bg, xw_b)

    ssq = jnp.concatenate([ssq[half][s * sub_planes:(s + 1) * sub_planes]
                           for s in range(S5_BLOCK) for half in range(n_half)], axis=0)
    scale = lax.rsqrt(ssq * (1.0 / d_ssd) + NORM_EPS)
    return (yd_ref[...] * scale * g_ref[...]).astype(BF16)


def _ssd_kernel(xa_ref, bc_ref, z_ref, dt_ref, s0_ref, dtb_ref, alog_ref, dskip_ref, g_ref, expand_ref,
                y_ref, s_out_ref, st_ref, yd_ref, *, padded_rows):
    @pl.when(pl.program_id(1) == 0)
    def _():
        st_ref[...] = s0_ref[...]

    for c in range(xa_ref.shape[1] // SSD_CHUNK):
        rows = slice(c * SSD_CHUNK, (c + 1) * SSD_CHUNK)
        y_ref[0, rows, :] = _ssd_chunk(xa_ref.at[0], bc_ref.at[0], z_ref.at[0], dt_ref.at[0], rows, dtb_ref[...],
                                       alog_ref[...], dskip_ref, g_ref, expand_ref, st_ref, yd_ref, padded_rows)
    s_out_ref[...] = st_ref[...]


def _ssd(xa, bc, z, dt, s0, dtb, alog, dskip, g, expand, padded_rows, chunks_per_step):
    bsz, seq, d_ssd = xa.shape
    rows = SSD_CHUNK * chunks_per_step
    chunk = lambda w: pl.BlockSpec((1, rows, w), lambda b, c: (b, c, 0))
    return pl.pallas_call(
        functools.partial(_ssd_kernel, padded_rows=padded_rows),
        grid=(bsz, seq // rows),
        in_specs=[chunk(d_ssd), chunk(bc.shape[-1]), chunk(d_ssd), chunk(LANES)]
        + [_const_spec(a.shape) for a in (s0, dtb, alog, dskip, g, expand)],
        out_specs=[chunk(d_ssd), _const_spec(s0.shape)],
        out_shape=[jax.ShapeDtypeStruct((bsz, seq, d_ssd), BF16), jax.ShapeDtypeStruct(s0.shape, F32)],
        scratch_shapes=[pltpu.VMEM(s0.shape, F32), pltpu.VMEM((SSD_CHUNK, d_ssd), F32)],
        compiler_params=_params(("arbitrary", "arbitrary")),
        name="ssd",
    )(xa, bc, z, dt, s0, dtb, alog, dskip, g, expand)


def _lam_pow(lr, li, step, k):
    kf = k.astype(F32)
    mag = jnp.exp(kf * (lr * step))
    ang = kf * (li * step)
    return mag * jnp.cos(ang), mag * jnp.sin(ang)


def _s5_prep_kernel(lr_ref, li_ref, ls_ref, bt_re_ref, bt_im_ref, c_re_ref, c_im_ref,
                    mt_ref, wc_ref, ec_ref, dp_re_ref, dp_im_ref):
    t, hw, p = S5_BLOCK, S5_WIDTH, S5_STATE
    lr, li = lr_ref[0], li_ref[0]
    step = jnp.exp(ls_ref[0])
    k_up = lax.broadcasted_iota(jnp.int32, (3 * SUBLANES, 1), 0)
    pw_re, pw_im = _lam_pow(lr, li, step, k_up)
    k_dn = (t - 1) - lax.broadcasted_iota(jnp.int32, (t, 1), 0)
    pr_re, pr_im = _lam_pow(lr, li, step, k_dn)
    k_blk = t * lax.broadcasted_iota(jnp.int32, (2 * SUBLANES, 1), 0)
    dp_re, dp_im = _lam_pow(lr, li, step, k_blk)

    ab_re, ab_im = pw_re[1:2, :], pw_im[1:2, :]
    den = lr * lr + li * li
    coef_re = ((ab_re - 1.0) * lr + ab_im * li) / den
    coef_im = (ab_im * lr - (ab_re - 1.0) * li) / den
    bt_re, bt_im = bt_re_ref[0], bt_im_ref[0]
    bb_re = coef_re * bt_re - coef_im * bt_im
    bb_im = coef_re * bt_im + coef_im * bt_re
    c_re, c_im = c_re_ref[0], c_im_ref[0]

    def rep(m):
        return jnp.broadcast_to(m[:, None, :], (t, hw, p)).reshape(t * hw, p)

    def tile(m):
        return jnp.concatenate([m] * t, axis=0)

    lane = lax.broadcasted_iota(jnp.int32, (hw, t * hw), 1)
    for a in range(S5_PREP_GROUPS):
        sl = slice(a * p, (a + 1) * p)
        ct_re, ct_im = tile(c_re[:, sl]), tile(c_im[:, sl])
        l0_re, l0_im = rep(pw_re[0:t, sl]), rep(pw_im[0:t, sl])
        l1_re, l1_im = rep(pw_re[1:t + 1, sl]), rep(pw_im[1:t + 1, sl])
        x0_re = l0_re * ct_re - l0_im * ct_im
        x0_im = l0_re * ct_im + l0_im * ct_re
        kt = _dot_nt_3x(bb_re[:, sl], x0_re) - _dot_nt_3x(bb_im[:, sl], x0_im)
        for s in range(t):
            blk = kt if s == 0 else jnp.where(lane >= s * hw, pltpu.roll(kt, s * hw, axis=1), 0.0)
            mt_ref[a, s * hw:(s + 1) * hw, :] = blk.astype(mt_ref.dtype)
        x1_re = l1_re * ct_re - l1_im * ct_im
        x1_im = l1_re * ct_im + l1_im * ct_re
        ec_ref[a] = jnp.concatenate([x1_re, -x1_im], axis=1).T.astype(ec_ref.dtype)
        lw_re, lw_im = rep(pr_re[:, sl]), rep(pr_im[:, sl])
        bt_re_a, bt_im_a = tile(bb_re[:, sl]), tile(bb_im[:, sl])
        wc_ref[a] = jnp.concatenate([lw_re * bt_re_a - lw_im * bt_im_a, lw_re * bt_im_a + lw_im * bt_re_a],
                                    axis=1).astype(wc_ref.dtype)
        dp_re_ref[a] = jnp.concatenate([dp_re[:, sl]] * 2, axis=1)
        dp_im_ref[a] = jnp.concatenate([dp_im[:, sl]] * 2, axis=1)


def _s5_prep(lr, li, ls, bt_re, bt_im, c_re, c_im):
    steps = lr.shape[0]
    gs, p, fl = S5_PREP_GROUPS, S5_STATE, S5_FLAT
    n_grp = steps * gs
    in_spec = lambda a: pl.BlockSpec((1,) + a.shape[1:], lambda i: (i, 0, 0))
    out_spec = lambda *s: pl.BlockSpec((gs,) + s, lambda i: (i, 0, 0))
    out = lambda *s, dt=BF16: jax.ShapeDtypeStruct((n_grp,) + s, dt)
    args = (lr, li, ls, bt_re, bt_im, c_re, c_im)
    return pl.pallas_call(
        _s5_prep_kernel,
        grid=(steps,),
        in_specs=[in_spec(a) for a in args],
        out_specs=[out_spec(fl, fl), out_spec(fl, 2 * p), out_spec(2 * p, fl), out_spec(2 * SUBLANES, 2 * p),
                   out_spec(2 * SUBLANES, 2 * p)],
        out_shape=[out(fl, fl), out(fl, 2 * p), out(2 * p, fl), out(2 * SUBLANES, 2 * p, dt=F32),
                   out(2 * SUBLANES, 2 * p, dt=F32)],
        compiler_params=_params(("parallel",)),
        name="s5_prep",
    )(*args)


def _s5_core_kernel(f_r_ref, f_m_ref, mt_ref, wc_ref, ec_ref, dp_re_ref, dp_im_ref, dskip_ref,
                    y_ref, wu_re_ref, wu_im_ref, s_re_ref, s_im_ref, *, n_batch):
    p = S5_STATE
    rows_m = f_m_ref.shape[1]
    rows_r = f_r_ref.shape[1] // n_batch
    per_batch = rows_m + rows_r
    n_tiles = per_batch // SUBLANES
    lane_lo = lax.broadcasted_iota(jnp.int32, (1, 2 * p), 1) < p

    def pack(a, b):
        return jnp.where(lane_lo, a, pltpu.roll(b, p, axis=1)), jnp.where(lane_lo, pltpu.roll(a, p, axis=1), b)

    wu_m = [_dot(f_m_ref[a], wc_ref[a]) for a in range(2)]
    wu_r = [_dot(f_r_ref[a], wc_ref[a]) for a in range(2)]
    m_re, m_im = pack(*wu_m)
    r_re, r_im = pack(*wu_r)
    for b in range(n_batch):
        wu_re_ref[b * per_batch:b * per_batch + rows_m, :] = m_re
        wu_im_ref[b * per_batch:b * per_batch + rows_m, :] = m_im
        wu_re_ref[b * per_batch + rows_m:(b + 1) * per_batch, :] = r_re[b * rows_r:(b + 1) * rows_r]
        wu_im_ref[b * per_batch + rows_m:(b + 1) * per_batch, :] = r_im[b * rows_r:(b + 1) * rows_r]

    dp_re = jnp.where(lane_lo, dp_re_ref[0], dp_re_ref[1])
    dp_im = jnp.where(lane_lo, dp_im_ref[0], dp_im_ref[1])
    sub = lax.broadcasted_iota(jnp.int32, (SUBLANES, 2 * p), 0)

    def cmul_add(a_re, a_im, d_re, d_im, x_re, x_im):
        return a_re + d_re * x_re - d_im * x_im, a_im + d_re * x_im + d_im * x_re

    def shift_down(x, k):
        return jnp.where(sub >= k, pltpu.roll(x, k, axis=0), 0.0)

    steps = {k: (jnp.where(sub >= k, dp_re[k:k + 1, :], 0.0), jnp.where(sub >= k, dp_im[k:k + 1, :], 0.0))
             for k in (1, 2, 4)}

    def tile_step(i, carry):
        new = []
        for b in range(n_batch):
            c_re, c_im = carry[2 * b], carry[2 * b + 1]
            base = pl.multiple_of(b * per_batch + i * SUBLANES, SUBLANES)
            v_re = wu_re_ref[pl.ds(base, SUBLANES), :]
            v_im = wu_im_ref[pl.ds(base, SUBLANES), :]
            for k, (m_re, m_im) in steps.items():
                v_re, v_im = cmul_add(v_re, v_im, m_re, m_im, pltpu.roll(v_re, k, axis=0), pltpu.roll(v_im, k, axis=0))
            e_re, e_im = cmul_add(shift_down(v_re, 1), shift_down(v_im, 1), dp_re[0:SUBLANES, :],
                                  dp_im[0:SUBLANES, :], c_re, c_im)
            s_re_ref[pl.ds(base, SUBLANES), :] = e_re
            s_im_ref[pl.ds(base, SUBLANES), :] = e_im
            n_re, n_im = cmul_add(v_re[SUBLANES - 1:, :], v_im[SUBLANES - 1:, :], dp_re[SUBLANES:SUBLANES + 1, :],
                                  dp_im[SUBLANES:SUBLANES + 1, :], c_re, c_im)
            new += [n_re, n_im]
        return tuple(new)

    zero = jnp.zeros((1, 2 * p), F32)
    lax.fori_loop(0, n_tiles, tile_step, (zero,) * (2 * n_batch), unroll=True)

    for b in range(n_batch):
        lo = b * per_batch + rows_m
        s_a, s_b = pack(s_re_ref[lo:lo + rows_r, :], s_im_ref[lo:lo + rows_r, :])
        for a, s_ab in enumerate((s_a, s_b)):
            f = f_r_ref[a, b * rows_r:(b + 1) * rows_r, :]
            y = _dot(f, mt_ref[a]) + _dot(s_ab.astype(BF16), ec_ref[a]) + f.astype(F32) * dskip_ref[a]
            y_ref[a, b * rows_r:(b + 1) * rows_r, :] = _gelu(y).astype(y_ref.dtype)


def _s5_core(f_r, f_m, mt, wc, ec, dp_re, dp_im, dskip, n_batch):
    n_grp, rows_r, fl = f_r.shape
    rows_m = f_m.shape[1]
    p2 = 2 * S5_STATE
    rows = n_batch * rows_m + rows_r
    spec = lambda *s: pl.BlockSpec((2,) + s, lambda i: (i, 0, 0))
    return pl.pallas_call(
        functools.partial(_s5_core_kernel, n_batch=n_batch),
        grid=(n_grp // 2,),
        in_specs=[spec(rows_r, fl), spec(rows_m, fl), spec(fl, fl), spec(fl, p2), spec(p2, fl),
                  spec(2 * SUBLANES, p2), spec(2 * SUBLANES, p2), spec(1, fl)],
        out_specs=spec(rows_r, fl),
        out_shape=jax.ShapeDtypeStruct((n_grp, rows_r, fl), BF16),
        scratch_shapes=[pltpu.VMEM((rows, p2), F32)] * 4,
        compiler_params=_params(("parallel",)),
        name="s5_core",
    )(f_r, f_m, mt, wc, ec, dp_re, dp_im, dskip)


def _s5_glu_kernel(yg_ref, w_ref, b_ref, g_ref, o_ref):
    d = o_ref.shape[-1]
    tm = o_ref.shape[0]
    lane = lax.broadcasted_iota(jnp.int32, (PLANE, LANES), 1)
    chunks = []
    for chunk in range(tm // SSD_CHUNK):
        planes = [[None] * (d // LANES) for _ in range(S5_BLOCK)]
        for oct_ in range(d // LANES):
            for half in range(S5_BLOCK // LANE_BLOCKS):
                grp = [yg_ref[oct_ * LANE_BLOCKS + gi, chunk * PLANE:(chunk + 1) * PLANE,
                              half * LANES:(half + 1) * LANES].astype(F32) for gi in range(LANE_BLOCKS)]
                tok = _transpose_lane_blocks(grp, lane)
                for s8 in range(LANE_BLOCKS):
                    planes[half * LANE_BLOCKS + s8][oct_] = tok[s8].astype(BF16)
        chunks.append(jnp.concatenate([jnp.concatenate(pieces, axis=1) for pieces in planes], axis=0))
    y = jnp.concatenate(chunks, axis=0) if len(chunks) > 1 else chunks[0]
    v = _dot(y, w_ref[...]) + b_ref[...]
    o = v[:, :d] * jax.nn.sigmoid(v[:, d:])
    o_ref[...] = _rmsnorm(o, g_ref[...]).astype(o_ref.dtype)


def _s5_glu(yg, w, b, g, tm):
    n_grp, blocks, fl = yg.shape
    rows, d = blocks * S5_BLOCK, n_grp * S5_WIDTH
    nb = tm // S5_BLOCK
    return pl.pallas_call(
        _s5_glu_kernel,
        grid=(rows // tm,),
        in_specs=[pl.BlockSpec((n_grp, nb, fl), lambda i: (0, i, 0))]
        + [_const_spec(a.shape, single_buffer=True) for a in (w, b, g)],
        out_specs=pl.BlockSpec((tm, d), lambda i: (i, 0)),
        out_shape=jax.ShapeDtypeStruct((rows, d), BF16),
        compiler_params=_params(("parallel",)),
        name="s5_glu",
    )(yg, w, b, g)


def _out_mlp_kernel(x_ref, ya_ref, yb_ref, woa_ref, wob_ref, gm_ref, wup_ref, wdn_ref, gf_ref, o_ref, *, ff_tile):
    h = x_ref[...] + _swap_row_order(_dot(ya_ref[...], woa_ref[...]) + _dot(yb_ref[...], wob_ref[...]))
    n = _rmsnorm(h, gm_ref[...]).astype(BF16)
    d_ff = wup_ref.shape[1]
    mlp = None
    for k in range(d_ff // ff_tile):
        m = _dot(n, wup_ref[:, k * ff_tile:(k + 1) * ff_tile])
        m = jnp.square(jnp.maximum(m, 0.0)).astype(BF16)
        part = _dot(m, wdn_ref[k * ff_tile:(k + 1) * ff_tile, :])
        mlp = part if mlp is None else mlp + part
    o_ref[...] = _rmsnorm(h + mlp, gf_ref[...])


def _out_mlp(x_rows, ya, yb, woa, wob, gm, wup, wdn, gf, tm, ff_tile):
    rows, d = ya.shape
    row_spec = pl.BlockSpec((tm, d), lambda i: (i, 0))
    return pl.pallas_call(
        functools.partial(_out_mlp_kernel, ff_tile=ff_tile),
        grid=(rows // tm,),
        in_specs=[row_spec, row_spec, row_spec]
        + [_const_spec(a.shape, single_buffer=True) for a in (woa, wob, gm, wup, wdn, gf)],
        out_specs=row_spec,
        out_shape=jax.ShapeDtypeStruct((rows, d), F32),
        compiler_params=_params(("parallel",)),
        name="out_mlp",
    )(x_rows, ya, yb, woa, wob, gm, wup, wdn, gf)


def _row_tile(rows, want):
    tm = min(rows, want)
    assert rows % tm == 0
    return tm


def kernel(x, meta_tokens, g_mix, w_in, conv_w, conv_b, dt_bias, a_log, d_ssd, g_ssd, lam_re, lam_im, log_step,
           b_re, b_im, c_re, c_im, d_s5, w_glu, b_glu, g_s5, w_out, g_mlp, w_up, w_down, g_final):
    bsz, seq, d_model = x.shape
    assert w_in.shape[0] == 1, "single-layer block: meta-token outputs are only consumed as state"
    assert seq % SSD_CHUNK == 0 and d_model % LANES == 0
    n_heads = dt_bias.shape[-1]
    d_ssd_w = n_heads * SSD_HEAD_DIM
    d_xbc = conv_w.shape[-1]
    n_grp = d_s5.shape[-2]
    o_dt = d_ssd_w + d_xbc
    o_u = o_dt + n_heads
    row = lambda v: v.reshape(1, -1).astype(F32)

    wi = w_in[0]
    wz = wi[:, :d_ssd_w].astype(BF16)
    wxbc = wi[:, d_ssd_w:o_dt].astype(BF16)
    wdt = jnp.pad(wi[:, o_dt:o_u], ((0, 0), (0, LANES - n_heads))).astype(BF16)
    wu = wi[:, o_u:].astype(BF16)
    pad_heads = lambda v: jnp.pad(row(v), ((0, 0), (0, LANES - n_heads)))
    convw, convb, gmix = conv_w[0].astype(F32), row(conv_b[0]), row(g_mix[0])

    x_rows = x.reshape(bsz * seq, d_model)
    meta_chunk = jnp.concatenate([jnp.zeros((SSD_CHUNK - N_META, d_model), x.dtype), meta_tokens.astype(x.dtype)], axis=0)
    tm = _row_tile(seq, 1024)
    no_hist = jnp.zeros(((SSD_CONV - 1) * PLANE, d_xbc), F32)
    z_m, xa_m, bc_m, dt_m, f_m, tail_m = _in_proj(meta_chunk, no_hist, gmix, wz, wxbc, wdt, wu,
                                                  convw, convb, SSD_CHUNK, 1)
    z_r, xa_r, bc_r, dt_r, f_r, _ = _in_proj(x_rows, tail_m, gmix, wz, wxbc, wdt, wu, convw, convb, tm, seq // tm)

    seq3 = lambda a, b: a.reshape(b, -1, a.shape[-1])
    expand = (jnp.arange(d_ssd_w)[None, :] // SSD_HEAD_DIM == jnp.arange(LANES)[:, None]).astype(BF16)
    ssd_args = (pad_heads(dt_bias[0]), pad_heads(a_log[0]), row(jnp.repeat(d_ssd[0], SSD_HEAD_DIM)), row(g_ssd[0]), expand)
    no_state = jnp.zeros((SSD_GROUPS, SSD_STATE, d_ssd_w // SSD_GROUPS), F32)
    _, s_meta = _ssd(seq3(xa_m, 1), seq3(bc_m, 1), seq3(z_m, 1), seq3(dt_m, 1), no_state, *ssd_args,
                     padded_rows=SSD_CHUNK - N_META, chunks_per_step=1)
    y_ssd, _ = _ssd(seq3(xa_r, bsz), seq3(bc_r, bsz), seq3(z_r, bsz), seq3(dt_r, bsz), s_meta, *ssd_args, padded_rows=0,
                    chunks_per_step=4)

    t, gs, p = S5_BLOCK, S5_PREP_GROUPS, S5_STATE
    lanes = lambda v: v.astype(F32).reshape(n_grp // gs, 1, gs * p)
    per_grp = lambda v: (v.astype(F32).reshape(n_grp // gs, gs, S5_WIDTH, p).transpose(0, 2, 1, 3)
                         .reshape(n_grp // gs, S5_WIDTH, gs * p))
    mt, wc, ec, dp_re, dp_im = _s5_prep(lanes(lam_re[0]), lanes(lam_im[0]), lanes(jnp.repeat(log_step[0], p)),
                                        per_grp(jnp.swapaxes(b_re[0], 1, 2)), per_grp(jnp.swapaxes(b_im[0], 1, 2)),
                                        per_grp(c_re[0]), per_grp(c_im[0]))
    dsk = jnp.tile(d_s5[0].astype(F32), (1, t)).reshape(n_grp, 1, S5_FLAT)
    yg = _s5_core(f_r, f_m, mt, wc, ec, dp_re, dp_im, dsk, bsz)
    y_s5 = _s5_glu(yg, w_glu[0].astype(BF16), row(b_glu[0]), row(g_s5[0]), tm)

    wo = w_out[0].astype(BF16)
    out = _out_mlp(x_rows, y_ssd.reshape(bsz * seq, d_ssd_w), y_s5, wo[:d_ssd_w], wo[d_ssd_w:], row(g_mlp[0]),
                   w_up[0].astype(BF16), w_down[0].astype(BF16), row(g_final), tm, 1024)
    return out.reshape(bsz, seq, d_model).astype(x.dtype)
```

```python
import functools

import jax
import jax.numpy as jnp
from jax import lax
from jax.experimental import pallas as pl
from jax.experimental.pallas import tpu as pltpu

F32 = jnp.float32
BF16 = jnp.bfloat16

NORM_EPS = 1e-5
N_META = 16
SSD_CHUNK = 256
SSD_SUB = 128
SSD_HEAD_DIM = 64
SSD_GROUPS = 2
SSD_STATE = 128
SSD_CONV = 4
S5_WIDTH = 16
S5_STATE = 64
S5_BLOCK = 16
S5_FLAT = S5_BLOCK * S5_WIDTH
S5_PREP_GROUPS = 8
PLANE = SSD_CHUNK // S5_BLOCK
LANES = 128
SUBLANES = 8
LANE_BLOCKS = LANES // S5_WIDTH
COL_TILE = 256
GLU_DOTS = 4
VMEM_LIMIT_BYTES = 58 * 1024 * 1024

assert PLANE == S5_BLOCK == 2 * SUBLANES
assert LANES == 2 * SSD_HEAD_DIM


def _rmsnorm(x, g):
    return x * lax.rsqrt(jnp.mean(x * x, axis=-1, keepdims=True) + NORM_EPS) * g


def _silu(x):
    return x * jax.nn.sigmoid(x)


def _gelu(x):
    return 0.5 * x * (1.0 + lax.erf(x * (2.0 ** -0.5)))


def _dot(a, b):
    return jnp.dot(a, b, preferred_element_type=F32)


def _dot_nt(a, b):
    return lax.dot_general(a, b, (((1,), (1,)), ((), ())), preferred_element_type=F32)


def _dot_nt_3x(a, b):
    a_hi, b_hi = a.astype(BF16), b.astype(BF16)
    a_lo = (a - a_hi.astype(F32)).astype(BF16)
    b_lo = (b - b_hi.astype(F32)).astype(BF16)
    return _dot_nt(a_hi, b_hi) + _dot_nt(a_hi, b_lo) + _dot_nt(a_lo, b_hi)


def _dot_tn(a, b):
    return lax.dot_general(a, b, (((0,), (0,)), ((), ())), preferred_element_type=F32)


def _dot_split(a, m01, parts):
    acc = None
    rem = a
    for _ in range(parts):
        piece = rem.astype(BF16)
        rem = rem - piece.astype(F32)
        term = _dot(piece, m01)
        acc = term if acc is None else acc + term
    return acc


def _params(sem):
    return pltpu.CompilerParams(dimension_semantics=sem, vmem_limit_bytes=VMEM_LIMIT_BYTES)


def _const_spec(shape, single_buffer=False):
    nd = len(shape)
    mode = pl.Buffered(1) if single_buffer else None
    return pl.BlockSpec(shape, lambda *_: (0,) * nd, pipeline_mode=mode)


def _token_of_row(r):
    return (r % PLANE) * S5_BLOCK + r // PLANE


def _transpose8(arrays, pos, axis, unit):
    a = list(arrays)
    size = 8 * unit
    d = 4
    while d:
        keep_lo = (pos & (d * unit)) == 0
        for i in range(8):
            if i & d:
                continue
            lo, hi = a[i], a[i + d]
            a[i] = jnp.where(keep_lo, lo, pltpu.roll(hi, d * unit, axis=axis))
            a[i + d] = jnp.where(keep_lo, pltpu.roll(lo, size - d * unit, axis=axis), hi)
        d //= 2
    return a


def _transpose_lane_blocks(planes, lane):
    return _transpose8(planes, lane, 1, S5_WIDTH)


def _swap_row_order(v):
    sub = lax.broadcasted_iota(jnp.int32, (SUBLANES, v.shape[1]), 0)
    chunks = []
    for chunk in range(v.shape[0] // SSD_CHUNK):
        out = [[None, None] for _ in range(S5_BLOCK)]
        for ah in range(2):
            for bh in range(2):
                first = [chunk * SSD_CHUNK + (SUBLANES * ah + al) * PLANE + SUBLANES * bh for al in range(SUBLANES)]
                src = [v[r:r + SUBLANES] for r in first]
                dst = _transpose8(src, sub, 0, 1)
                for bl in range(SUBLANES):
                    out[SUBLANES * bh + bl][ah] = dst[bl]
        chunks.append(jnp.concatenate([piece for pair in out for piece in pair], axis=0))
    return jnp.concatenate(chunks, axis=0) if len(chunks) > 1 else chunks[0]


def _in_proj_kernel(x_ref, hist_ref, g_ref, wz_ref, wxbc_ref, wdt_ref, wu_ref, convw_ref, convb_ref,
                    z_ref, xa_ref, bc_ref, dt_ref, f_ref, tail_ref, hist_scr, *, tiles_per_seq):
    tm = z_ref.shape[0]
    d_x = xa_ref.shape[1]
    n_hist = (SSD_CONV - 1) * PLANE
    n_chunks = tm // SSD_CHUNK

    @pl.when(lax.rem(pl.program_id(0), tiles_per_seq) == 0)
    def _():
        hist_scr[...] = hist_ref[...]

    n = _rmsnorm(_swap_row_order(x_ref[...]), g_ref[...]).astype(BF16)

    lane = lax.broadcasted_iota(jnp.int32, (PLANE, LANES), 1)
    for ct in range(wu_ref.shape[1] // COL_TILE):
        u = _dot(n, wu_ref[:, ct * COL_TILE:(ct + 1) * COL_TILE])
        for chunk in range(n_chunks):
            for o in range(COL_TILE // LANES):
                oct_ = ct * (COL_TILE // LANES) + o
                for half in range(S5_BLOCK // LANE_BLOCKS):
                    tok = []
                    for s in range(half * LANE_BLOCKS, (half + 1) * LANE_BLOCKS):
                        lo = chunk * SSD_CHUNK + s * PLANE
                        tok.append(u[lo:lo + PLANE, o * LANES:(o + 1) * LANES])
                    grp = _transpose_lane_blocks(tok, lane)
                    for gi in range(LANE_BLOCKS):
                        f_ref[oct_ * LANE_BLOCKS + gi, chunk * PLANE:(chunk + 1) * PLANE,
                              half * LANES:(half + 1) * LANES] = grp[gi].astype(BF16)

    row0 = lax.broadcasted_iota(jnp.int32, (PLANE, 1), 0) == 0
    for ct in range(wxbc_ref.shape[1] // COL_TILE):
        cols = slice(ct * COL_TILE, (ct + 1) * COL_TILE)
        pre = _dot(n, wxbc_ref[:, cols])
        w = [convw_ref[k:k + 1, cols] for k in range(SSD_CONV)]
        prev = hist_scr[:, cols]
        for chunk in range(n_chunks):
            cur = pre[chunk * SSD_CHUNK:(chunk + 1) * SSD_CHUNK]
            last = cur[SSD_CHUNK - n_hist:]
            wrapped = [jnp.where(row0, pltpu.roll(prev[m * PLANE:(m + 1) * PLANE], 1, axis=0),
                                 pltpu.roll(last[m * PLANE:(m + 1) * PLANE], 1, axis=0)) for m in range(SSD_CONV - 1)]
            acc = convb_ref[:, cols] + w[SSD_CONV - 1] * cur
            for k in range(1, SSD_CONV):
                shifted = jnp.concatenate(wrapped[SSD_CONV - 1 - k:] + [cur[:SSD_CHUNK - k * PLANE]], axis=0)
                acc = acc + w[SSD_CONV - 1 - k] * shifted
            act = _silu(acc)
            rows = slice(chunk * SSD_CHUNK, (chunk + 1) * SSD_CHUNK)
            if ct * COL_TILE < d_x:
                xa_ref[rows, cols] = act
            else:
                bc_ref[rows, ct * COL_TILE - d_x:(ct + 1) * COL_TILE - d_x] = act.astype(BF16)
            prev = last
        hist_scr[:, cols] = prev
        tail_ref[:, cols] = prev

    for ct in range(wz_ref.shape[1] // COL_TILE):
        cols = slice(ct * COL_TILE, (ct + 1) * COL_TILE)
        z_ref[:, cols] = _dot(n, wz_ref[:, cols])
    dt_ref[...] = _dot(n, wdt_ref[...])


def _in_proj(x_rows, hist, g, wz, wxbc, wdt, wu, convw, convb, tm, tiles_per_seq):
    rows, d = x_rows.shape
    d_z, d_xbc, d_u = wz.shape[1], wxbc.shape[1], wu.shape[1]
    d_bc = d_xbc - d_z
    n_grp = d_u // S5_WIDTH
    nb = tm // S5_BLOCK
    row_spec = lambda w: pl.BlockSpec((tm, w), lambda i: (i, 0))
    return pl.pallas_call(
        functools.partial(_in_proj_kernel, tiles_per_seq=tiles_per_seq),
        grid=(rows // tm,),
        in_specs=[row_spec(d)]
        + [_const_spec(a.shape, single_buffer=True) for a in (hist, g, wz, wxbc, wdt, wu, convw, convb)],
        out_specs=[row_spec(d_z), row_spec(d_z), row_spec(d_bc), row_spec(LANES),
                   pl.BlockSpec((n_grp, nb, S5_FLAT), lambda i: (0, i, 0)), _const_spec(hist.shape)],
        out_shape=[jax.ShapeDtypeStruct((rows, d_z), F32), jax.ShapeDtypeStruct((rows, d_z), F32),
                   jax.ShapeDtypeStruct((rows, d_bc), BF16), jax.ShapeDtypeStruct((rows, LANES), F32),
                   jax.ShapeDtypeStruct((n_grp, rows // S5_BLOCK, S5_FLAT), BF16),
                   jax.ShapeDtypeStruct(hist.shape, F32)],
        scratch_shapes=[pltpu.VMEM(hist.shape, F32)],
        compiler_params=_params(("arbitrary",)),
        name="in_proj",
    )(x_rows, hist, g, wz, wxbc, wdt, wu, convw, convb)


def _ssd_chunk(xa_ref, bc_ref, z_ref, dt_ref, rows, dtb, alog, dskip_ref, g_ref, expand_ref, st_ref, yd_ref,
               padded_rows):
    q = SSD_SUB
    d_ssd = xa_ref.shape[-1]
    gw = d_ssd // SSD_GROUPS
    sub_planes = q // S5_BLOCK

    def half_of(v, half):
        return jnp.concatenate([v[s * PLANE + half * sub_planes:s * PLANE + (half + 1) * sub_planes]
                                for s in range(S5_BLOCK)], axis=0)

    def half_ref(ref, half, cols):
        first = rows.start + half * sub_planes
        return jnp.concatenate([ref[first + s * PLANE:first + s * PLANE + sub_planes, cols]
                                for s in range(S5_BLOCK)], axis=0)

    def token_of_row(r):
        return (r % sub_planes) * S5_BLOCK + r // sub_planes

    bc = bc_ref[rows, :].astype(F32)
    t_r = token_of_row(lax.broadcasted_iota(jnp.int32, (q, q), 0))
    t_c = token_of_row(lax.broadcasted_iota(jnp.int32, (q, q), 1))
    causal = t_r >= t_c
    tri = jnp.where(t_c >= t_r, 1.0, 0.0).astype(BF16)
    first_head = lax.broadcasted_iota(jnp.int32, (q, LANES), 1) < SSD_HEAD_DIM
    n_half = SSD_CHUNK // q
    n_tile = d_ssd // COL_TILE

    def prepare(half):
        bc_h = half_of(bc, half).astype(BF16)
        dt = jax.nn.softplus(half_ref(dt_ref, half, slice(None)) + dtb)
        if padded_rows:
            t_row = half * q + token_of_row(lax.broadcasted_iota(jnp.int32, dt.shape, 0))
            dt = jnp.where(t_row < padded_rows, 0.0, dt)
        d_a = dt * (-jnp.exp(alog))
        a_cs_t = _dot_split(d_a.T, tri, 3)
        return bc_h[:, :SSD_GROUPS * SSD_STATE], bc_h[:, SSD_GROUPS * SSD_STATE:], dt, a_cs_t.T, a_cs_t

    def expand_tile(half, t, dt, a_cs):
        cols = slice(t * COL_TILE, (t + 1) * COL_TILE)
        expand = expand_ref[:, cols]
        dt_x = _dot_split(dt, expand, 2)
        a_x = _dot_split(a_cs, expand, 2)
        a_last = a_x[q - 1:q, :]
        x_in = half_ref(xa_ref, half, cols)
        xdt = x_in * dt_x
        xw_b = (xdt * jnp.exp(a_last - a_x)).astype(BF16)
        return x_in, xdt.astype(BF16), xw_b, jnp.exp(a_x), jnp.exp(a_last)

    halves = [prepare(half) for half in range(n_half)]
    steps = [(half, t) for half in range(n_half) for t in range(n_tile)]
    expanded = expand_tile(*steps[0], halves[0][2], halves[0][3])
    ssq = [jnp.zeros((q, 1), F32) for _ in range(n_half)]
    cb = None
    for i, (half, t) in enumerate(steps):
        b_in, c_in, _, a_cs, a_cs_t = halves[half]
        x_in, xdt_b, xw_b, ea_x, ea_last = expanded
        if i + 1 < len(steps):
            nh, nt = steps[i + 1]
            expanded = expand_tile(nh, nt, halves[nh][2], halves[nh][3])
        cols = slice(t * COL_TILE, (t + 1) * COL_TILE)
        g = (t * COL_TILE) // gw
        gcols = slice(t * COL_TILE - g * gw, (t + 1) * COL_TILE - g * gw)
        cg = c_in[:, g * SSD_STATE:(g + 1) * SSD_STATE]
        bg = b_in[:, g * SSD_STATE:(g + 1) * SSD_STATE]
        if (t * COL_TILE) % gw == 0:
            cb = _dot_nt(cg, bg)
        s_prev = st_ref[g, :, gcols]
        y_off = _dot(cg, s_prev.astype(BF16)) * ea_x
        pairs = []
        for pair in range(COL_TILE // LANES):
            lo = pair * LANES
            att = []
            for h in (2 * (t * (COL_TILE // LANES) + pair), 2 * (t * (COL_TILE // LANES) + pair) + 1):
                seg = a_cs[:, h:h + 1] - a_cs_t[h:h + 1, :]
                att.append((cb * jnp.exp(jnp.where(causal, seg, -jnp.inf))).astype(BF16))
            x_pair = xdt_b[:, lo:lo + LANES]
            zero = jnp.zeros_like(x_pair)
            rhs = jnp.concatenate([jnp.where(first_head, x_pair, zero), jnp.where(first_head, zero, x_pair)], axis=0)
            pairs.append(_dot(jnp.concatenate(att, axis=1), rhs) + y_off[:, lo:lo + LANES])
        y = jnp.concatenate(pairs, axis=1) + x_in * dskip_ref[:, cols]
        y = y * _silu(half_ref(z_ref, half, cols))
        for s in range(S5_BLOCK):
            yd_ref[s * PLANE + half * sub_planes:s * PLANE + (half + 1) * sub_planes, cols] = (
                y[s * sub_planes:(s + 1) * sub_planes])
        ssq[half] = ssq[half] + jnp.sum(y * y, axis=-1, keepdims=True)
        st_ref[g, :, gcols] = s_prev * ea_last + _dot_tn(bg, xw_b)

    ssq = jnp.concatenate([ssq[half][s * sub_planes:(s + 1) * sub_planes]
                           for s in range(S5_BLOCK) for half in range(n_half)], axis=0)
    scale = lax.rsqrt(ssq * (1.0 / d_ssd) + NORM_EPS)
    return (yd_ref[...] * scale * g_ref[...]).astype(BF16)


def _ssd_kernel(xa_ref, bc_ref, z_ref, dt_ref, s0_ref, dtb_ref, alog_ref, dskip_ref, g_ref, expand_ref,
                y_ref, s_out_ref, st_ref, yd_ref, *, padded_rows):
    @pl.when(pl.program_id(1) == 0)
    def _():
        st_ref[...] = s0_ref[...]

    for c in range(xa_ref.shape[1] // SSD_CHUNK):
        rows = slice(c * SSD_CHUNK, (c + 1) * SSD_CHUNK)
        y_ref[0, rows, :] = _ssd_chunk(xa_ref.at[0], bc_ref.at[0], z_ref.at[0], dt_ref.at[0], rows, dtb_ref[...],
                                       alog_ref[...], dskip_ref, g_ref, expand_ref, st_ref, yd_ref, padded_rows)
    s_out_ref[...] = st_ref[...]


def _ssd(xa, bc, z, dt, s0, dtb, alog, dskip, g, expand, padded_rows, chunks_per_step):
    bsz, seq, d_ssd = xa.shape
    rows = SSD_CHUNK * chunks_per_step
    chunk = lambda w: pl.BlockSpec((1, rows, w), lambda b, c: (b, c, 0))
    return pl.pallas_call(
        functools.partial(_ssd_kernel, padded_rows=padded_rows),
        grid=(bsz, seq // rows),
        in_specs=[chunk(d_ssd), chunk(bc.shape[-1]), chunk(d_ssd), chunk(LANES)]
        + [_const_spec(a.shape) for a in (s0, dtb, alog, dskip, g, expand)],
        out_specs=[chunk(d_ssd), _const_spec(s0.shape)],
        out_shape=[jax.ShapeDtypeStruct((bsz, seq, d_ssd), BF16), jax.ShapeDtypeStruct(s0.shape, F32)],
        scratch_shapes=[pltpu.VMEM(s0.shape, F32), pltpu.VMEM((SSD_CHUNK, d_ssd), F32)],
        compiler_params=_params(("arbitrary", "arbitrary")),
        name="ssd",
    )(xa, bc, z, dt, s0, dtb, alog, dskip, g, expand)


def _lam_pow(lr, li, step, k):
    kf = k.astype(F32)
    mag = jnp.exp(kf * (lr * step))
    ang = kf * (li * step)
    return mag * jnp.cos(ang), mag * jnp.sin(ang)


def _s5_prep_kernel(lr_ref, li_ref, ls_ref, bt_re_ref, bt_im_ref, c_re_ref, c_im_ref,
                    mt_ref, wc_ref, ec_ref, dp_re_ref, dp_im_ref):
    t, hw, p = S5_BLOCK, S5_WIDTH, S5_STATE
    lr, li = lr_ref[0], li_ref[0]
    step = jnp.exp(ls_ref[0])
    k_up = lax.broadcasted_iota(jnp.int32, (3 * SUBLANES, 1), 0)
    pw_re, pw_im = _lam_pow(lr, li, step, k_up)
    k_dn = (t - 1) - lax.broadcasted_iota(jnp.int32, (t, 1), 0)
    pr_re, pr_im = _lam_pow(lr, li, step, k_dn)
    k_blk = t * lax.broadcasted_iota(jnp.int32, (2 * SUBLANES, 1), 0)
    dp_re, dp_im = _lam_pow(lr, li, step, k_blk)

    ab_re, ab_im = pw_re[1:2, :], pw_im[1:2, :]
    den = lr * lr + li * li
    coef_re = ((ab_re - 1.0) * lr + ab_im * li) / den
    coef_im = (ab_im * lr - (ab_re - 1.0) * li) / den
    bt_re, bt_im = bt_re_ref[0], bt_im_ref[0]
    bb_re = coef_re * bt_re - coef_im * bt_im
    bb_im = coef_re * bt_im + coef_im * bt_re
    c_re, c_im = c_re_ref[0], c_im_ref[0]

    def rep(m):
        return jnp.broadcast_to(m[:, None, :], (t, hw, p)).reshape(t * hw, p)

    def tile(m):
        return jnp.concatenate([m] * t, axis=0)

    lane = lax.broadcasted_iota(jnp.int32, (hw, t * hw), 1)
    for a in range(S5_PREP_GROUPS):
        sl = slice(a * p, (a + 1) * p)
        ct_re, ct_im = tile(c_re[:, sl]), tile(c_im[:, sl])
        l0_re, l0_im = rep(pw_re[0:t, sl]), rep(pw_im[0:t, sl])
        l1_re, l1_im = rep(pw_re[1:t + 1, sl]), rep(pw_im[1:t + 1, sl])
        x0_re = l0_re * ct_re - l0_im * ct_im
        x0_im = l0_re * ct_im + l0_im * ct_re
        kt = _dot_nt_3x(bb_re[:, sl], x0_re) - _dot_nt_3x(bb_im[:, sl], x0_im)
        for s in range(t):
            blk = kt if s == 0 else jnp.where(lane >= s * hw, pltpu.roll(kt, s * hw, axis=1), 0.0)
            mt_ref[a, s * hw:(s + 1) * hw, :] = blk.astype(mt_ref.dtype)
        x1_re = l1_re * ct_re - l1_im * ct_im
        x1_im = l1_re * ct_im + l1_im * ct_re
        ec_ref[a] = jnp.concatenate([x1_re, -x1_im], axis=1).T.astype(ec_ref.dtype)
        lw_re, lw_im = rep(pr_re[:, sl]), rep(pr_im[:, sl])
        bt_re_a, bt_im_a = tile(bb_re[:, sl]), tile(bb_im[:, sl])
        wc_ref[a] = jnp.concatenate([lw_re * bt_re_a - lw_im * bt_im_a, lw_re * bt_im_a + lw_im * bt_re_a],
                                    axis=1).astype(wc_ref.dtype)
        dp_re_ref[a] = jnp.concatenate([dp_re[:, sl]] * 2, axis=1)
        dp_im_ref[a] = jnp.concatenate([dp_im[:, sl]] * 2, axis=1)


def _s5_prep(lr, li, ls, bt_re, bt_im, c_re, c_im):
    steps = lr.shape[0]
    gs, p, fl = S5_PREP_GROUPS, S5_STATE, S5_FLAT
    n_grp = steps * gs
    in_spec = lambda a: pl.BlockSpec((1,) + a.shape[1:], lambda i: (i, 0, 0))
    out_spec = lambda *s: pl.BlockSpec((gs,) + s, lambda i: (i, 0, 0))
    out = lambda *s, dt=BF16: jax.ShapeDtypeStruct((n_grp,) + s, dt)
    args = (lr, li, ls, bt_re, bt_im, c_re, c_im)
    return pl.pallas_call(
        _s5_prep_kernel,
        grid=(steps,),
        in_specs=[in_spec(a) for a in args],
        out_specs=[out_spec(fl, fl), out_spec(fl, 2 * p), out_spec(2 * p, fl), out_spec(2 * SUBLANES, 2 * p),
                   out_spec(2 * SUBLANES, 2 * p)],
        out_shape=[out(fl, fl), out(fl, 2 * p), out(2 * p, fl), out(2 * SUBLANES, 2 * p, dt=F32),
                   out(2 * SUBLANES, 2 * p, dt=F32)],
        compiler_params=_params(("parallel",)),
        name="s5_prep",
    )(*args)


def _s5_core_kernel(f_r_ref, f_m_ref, mt_ref, wc_ref, ec_ref, dp_re_ref, dp_im_ref, dskip_ref,
                    y_ref, wu_re_ref, wu_im_ref, s_re_ref, s_im_ref, *, n_batch):
    p = S5_STATE
    rows_m = f_m_ref.shape[1]
    rows_r = f_r_ref.shape[1] // n_batch
    per_batch = rows_m + rows_r
    n_tiles = per_batch // SUBLANES
    lane_lo = lax.broadcasted_iota(jnp.int32, (1, 2 * p), 1) < p

    def pack(a, b):
        return jnp.where(lane_lo, a, pltpu.roll(b, p, axis=1)), jnp.where(lane_lo, pltpu.roll(a, p, axis=1), b)

    wu_m = [_dot(f_m_ref[a], wc_ref[a]) for a in range(2)]
    wu_r = [_dot(f_r_ref[a], wc_ref[a]) for a in range(2)]
    m_re, m_im = pack(*wu_m)
    r_re, r_im = pack(*wu_r)
    for b in range(n_batch):
        wu_re_ref[b * per_batch:b * per_batch + rows_m, :] = m_re
        wu_im_ref[b * per_batch:b * per_batch + rows_m, :] = m_im
        wu_re_ref[b * per_batch + rows_m:(b + 1) * per_batch, :] = r_re[b * rows_r:(b + 1) * rows_r]
        wu_im_ref[b * per_batch + rows_m:(b + 1) * per_batch, :] = r_im[b * rows_r:(b + 1) * rows_r]

    dp_re = jnp.where(lane_lo, dp_re_ref[0], dp_re_ref[1])
    dp_im = jnp.where(lane_lo, dp_im_ref[0], dp_im_ref[1])
    sub = lax.broadcasted_iota(jnp.int32, (SUBLANES, 2 * p), 0)

    def cmul_add(a_re, a_im, d_re, d_im, x_re, x_im):
        return a_re + d_re * x_re - d_im * x_im, a_im + d_re * x_im + d_im * x_re

    def shift_down(x, k):
        return jnp.where(sub >= k, pltpu.roll(x, k, axis=0), 0.0)

    steps = {k: (jnp.where(sub >= k, dp_re[k:k + 1, :], 0.0), jnp.where(sub >= k, dp_im[k:k + 1, :], 0.0))
             for k in (1, 2, 4)}

    def tile_step(i, carry):
        new = []
        for b in range(n_batch):
            c_re, c_im = carry[2 * b], carry[2 * b + 1]
            base = pl.multiple_of(b * per_batch + i * SUBLANES, SUBLANES)
            v_re = wu_re_ref[pl.ds(base, SUBLANES), :]
            v_im = wu_im_ref[pl.ds(base, SUBLANES), :]
            for k, (m_re, m_im) in steps.items():
                v_re, v_im = cmul_add(v_re, v_im, m_re, m_im, pltpu.roll(v_re, k, axis=0), pltpu.roll(v_im, k, axis=0))
            e_re, e_im = cmul_add(shift_down(v_re, 1), shift_down(v_im, 1), dp_re[0:SUBLANES, :],
                                  dp_im[0:SUBLANES, :], c_re, c_im)
            s_re_ref[pl.ds(base, SUBLANES), :] = e_re
            s_im_ref[pl.ds(base, SUBLANES), :] = e_im
            n_re, n_im = cmul_add(v_re[SUBLANES - 1:, :], v_im[SUBLANES - 1:, :], dp_re[SUBLANES:SUBLANES + 1, :],
                                  dp_im[SUBLANES:SUBLANES + 1, :], c_re, c_im)
            new += [n_re, n_im]
        return tuple(new)

    zero = jnp.zeros((1, 2 * p), F32)
    lax.fori_loop(0, n_tiles, tile_step, (zero,) * (2 * n_batch), unroll=True)

    for b in range(n_batch):
        lo = b * per_batch + rows_m
        s_a, s_b = pack(s_re_ref[lo:lo + rows_r, :], s_im_ref[lo:lo + rows_r, :])
        for a, s_ab in enumerate((s_a, s_b)):
            f = f_r_ref[a, b * rows_r:(b + 1) * rows_r, :]
            y = _dot(f, mt_ref[a]) + _dot(s_ab.astype(BF16), ec_ref[a]) + f.astype(F32) * dskip_ref[a]
            y_ref[a, b * rows_r:(b + 1) * rows_r, :] = _gelu(y).astype(y_ref.dtype)


def _s5_core(f_r, f_m, mt, wc, ec, dp_re, dp_im, dskip, n_batch):
    n_grp, rows_r, fl = f_r.shape
    rows_m = f_m.shape[1]
    p2 = 2 * S5_STATE
    rows = n_batch * rows_m + rows_r
    spec = lambda *s: pl.BlockSpec((2,) + s, lambda i: (i, 0, 0))
    return pl.pallas_call(
        functools.partial(_s5_core_kernel, n_batch=n_batch),
        grid=(n_grp // 2,),
        in_specs=[spec(rows_r, fl), spec(rows_m, fl), spec(fl, fl), spec(fl, p2), spec(p2, fl),
                  spec(2 * SUBLANES, p2), spec(2 * SUBLANES, p2), spec(1, fl)],
        out_specs=spec(rows_r, fl),
        out_shape=jax.ShapeDtypeStruct((n_grp, rows_r, fl), BF16),
        scratch_shapes=[pltpu.VMEM((rows, p2), F32)] * 4,
        compiler_params=_params(("parallel",)),
        name="s5_core",
    )(f_r, f_m, mt, wc, ec, dp_re, dp_im, dskip)


def _s5_glu_kernel(yg_ref, w_ref, b_ref, g_ref, o_ref):
    d = o_ref.shape[-1]
    tm = o_ref.shape[0]
    lane = lax.broadcasted_iota(jnp.int32, (PLANE, LANES), 1)
    chunks = []
    for chunk in range(tm // SSD_CHUNK):
        planes = [[None] * (d // LANES) for _ in range(S5_BLOCK)]
        for oct_ in range(d // LANES):
            for half in range(S5_BLOCK // LANE_BLOCKS):
                grp = [yg_ref[oct_ * LANE_BLOCKS + gi, chunk * PLANE:(chunk + 1) * PLANE,
                              half * LANES:(half + 1) * LANES].astype(F32) for gi in range(LANE_BLOCKS)]
                tok = _transpose_lane_blocks(grp, lane)
                for s8 in range(LANE_BLOCKS):
                    planes[half * LANE_BLOCKS + s8][oct_] = tok[s8].astype(BF16)
        chunks.append(jnp.concatenate([jnp.concatenate(pieces, axis=1) for pieces in planes], axis=0))
    per_dot = max(1, len(chunks) // GLU_DOTS)
    for first in range(0, len(chunks), per_dot):
        y = jnp.concatenate(chunks[first:first + per_dot], axis=0) if per_dot > 1 else chunks[first]
        v = _dot(y, w_ref[...]) + b_ref[...]
        o = v[:, :d] * jax.nn.sigmoid(v[:, d:])
        o_ref[first * SSD_CHUNK:(first + per_dot) * SSD_CHUNK, :] = _rmsnorm(o, g_ref[...]).astype(o_ref.dtype)


def _s5_glu(yg, w, b, g, tm):
    n_grp, blocks, fl = yg.shape
    rows, d = blocks * S5_BLOCK, n_grp * S5_WIDTH
    nb = tm // S5_BLOCK
    return pl.pallas_call(
        _s5_glu_kernel,
        grid=(rows // tm,),
        in_specs=[pl.BlockSpec((n_grp, nb, fl), lambda i: (0, i, 0))]
        + [_const_spec(a.shape, single_buffer=True) for a in (w, b, g)],
        out_specs=pl.BlockSpec((tm, d), lambda i: (i, 0)),
        out_shape=jax.ShapeDtypeStruct((rows, d), BF16),
        compiler_params=_params(("parallel",)),
        name="s5_glu",
    )(yg, w, b, g)


def _out_mlp_kernel(x_ref, ya_ref, yb_ref, woa_ref, wob_ref, gm_ref, wup_ref, wdn_ref, gf_ref, o_ref, *, ff_tile):
    h = x_ref[...] + _swap_row_order(_dot(ya_ref[...], woa_ref[...]) + _dot(yb_ref[...], wob_ref[...]))
    n = _rmsnorm(h, gm_ref[...]).astype(BF16)
    d_ff = wup_ref.shape[1]
    mlp = None
    for k in range(d_ff // ff_tile):
        m = _dot(n, wup_ref[:, k * ff_tile:(k + 1) * ff_tile])
        m = jnp.square(jnp.maximum(m, 0.0)).astype(BF16)
        part = _dot(m, wdn_ref[k * ff_tile:(k + 1) * ff_tile, :])
        mlp = part if mlp is None else mlp + part
    o_ref[...] = _rmsnorm(h + mlp, gf_ref[...])


def _out_mlp(x_rows, ya, yb, woa, wob, gm, wup, wdn, gf, tm, ff_tile):
    rows, d = ya.shape
    row_spec = pl.BlockSpec((tm, d), lambda i: (i, 0))
    return pl.pallas_call(
        functools.partial(_out_mlp_kernel, ff_tile=ff_tile),
        grid=(rows // tm,),
        in_specs=[row_spec, row_spec, row_spec]
        + [_const_spec(a.shape, single_buffer=True) for a in (woa, wob, gm, wup, wdn, gf)],
        out_specs=row_spec,
        out_shape=jax.ShapeDtypeStruct((rows, d), F32),
        compiler_params=_params(("parallel",)),
        name="out_mlp",
    )(x_rows, ya, yb, woa, wob, gm, wup, wdn, gf)


def _row_tile(rows, want):
    tm = min(rows, want)
    assert rows % tm == 0
    return tm


def kernel(x, meta_tokens, g_mix, w_in, conv_w, conv_b, dt_bias, a_log, d_ssd, g_ssd, lam_re, lam_im, log_step,
           b_re, b_im, c_re, c_im, d_s5, w_glu, b_glu, g_s5, w_out, g_mlp, w_up, w_down, g_final):
    bsz, seq, d_model = x.shape
    assert w_in.shape[0] == 1, "single-layer block: meta-token outputs are only consumed as state"
    assert seq % SSD_CHUNK == 0 and d_model % LANES == 0
    n_heads = dt_bias.shape[-1]
    d_ssd_w = n_heads * SSD_HEAD_DIM
    d_xbc = conv_w.shape[-1]
    n_grp = d_s5.shape[-2]
    o_dt = d_ssd_w + d_xbc
    o_u = o_dt + n_heads
    row = lambda v: v.reshape(1, -1).astype(F32)

    wi = w_in[0]
    wz = wi[:, :d_ssd_w].astype(BF16)
    wxbc = wi[:, d_ssd_w:o_dt].astype(BF16)
    wdt = jnp.pad(wi[:, o_dt:o_u], ((0, 0), (0, LANES - n_heads))).astype(BF16)
    wu = wi[:, o_u:].astype(BF16)
    pad_heads = lambda v: jnp.pad(row(v), ((0, 0), (0, LANES - n_heads)))
    convw, convb, gmix = conv_w[0].astype(F32), row(conv_b[0]), row(g_mix[0])

    x_rows = x.reshape(bsz * seq, d_model)
    meta_chunk = jnp.concatenate([jnp.zeros((SSD_CHUNK - N_META, d_model), x.dtype), meta_tokens.astype(x.dtype)], axis=0)
    tm = _row_tile(seq, 1024)
    no_hist = jnp.zeros(((SSD_CONV - 1) * PLANE, d_xbc), F32)
    z_m, xa_m, bc_m, dt_m, f_m, tail_m = _in_proj(meta_chunk, no_hist, gmix, wz, wxbc, wdt, wu,
                                                  convw, convb, SSD_CHUNK, 1)
    z_r, xa_r, bc_r, dt_r, f_r, _ = _in_proj(x_rows, tail_m, gmix, wz, wxbc, wdt, wu, convw, convb, tm, seq // tm)

    seq3 = lambda a, b: a.reshape(b, -1, a.shape[-1])
    expand = (jnp.arange(d_ssd_w)[None, :] // SSD_HEAD_DIM == jnp.arange(LANES)[:, None]).astype(BF16)
    ssd_args = (pad_heads(dt_bias[0]), pad_heads(a_log[0]), row(jnp.repeat(d_ssd[0], SSD_HEAD_DIM)), row(g_ssd[0]), expand)
    no_state = jnp.zeros((SSD_GROUPS, SSD_STATE, d_ssd_w // SSD_GROUPS), F32)
    _, s_meta = _ssd(seq3(xa_m, 1), seq3(bc_m, 1), seq3(z_m, 1), seq3(dt_m, 1), no_state, *ssd_args,
                     padded_rows=SSD_CHUNK - N_META, chunks_per_step=1)
    y_ssd, _ = _ssd(seq3(xa_r, bsz), seq3(bc_r, bsz), seq3(z_r, bsz), seq3(dt_r, bsz), s_meta, *ssd_args, padded_rows=0,
                    chunks_per_step=4)

    t, gs, p = S5_BLOCK, S5_PREP_GROUPS, S5_STATE
    lanes = lambda v: v.astype(F32).reshape(n_grp // gs, 1, gs * p)
    per_grp = lambda v: (v.astype(F32).reshape(n_grp // gs, gs, S5_WIDTH, p).transpose(0, 2, 1, 3)
                         .reshape(n_grp // gs, S5_WIDTH, gs * p))
    mt, wc, ec, dp_re, dp_im = _s5_prep(lanes(lam_re[0]), lanes(lam_im[0]), lanes(jnp.repeat(log_step[0], p)),
                                        per_grp(jnp.swapaxes(b_re[0], 1, 2)), per_grp(jnp.swapaxes(b_im[0], 1, 2)),
                                        per_grp(c_re[0]), per_grp(c_im[0]))
    dsk = jnp.tile(d_s5[0].astype(F32), (1, t)).reshape(n_grp, 1, S5_FLAT)
    yg = _s5_core(f_r, f_m, mt, wc, ec, dp_re, dp_im, dsk, bsz)
    y_s5 = _s5_glu(yg, w_glu[0].astype(BF16), row(b_glu[0]), row(g_s5[0]), tm)

    wo = w_out[0].astype(BF16)
    out = _out_mlp(x_rows, y_ssd.reshape(bsz * seq, d_ssd_w), y_s5, wo[:d_ssd_w], wo[d_ssd_w:], row(g_mlp[0]),
                   w_up[0].astype(BF16), w_down[0].astype(BF16), row(g_final), tm, 1024)
    return out.reshape(bsz, seq, d_model).astype(x.dtype)
```

```python
import functools

import jax
import jax.numpy as jnp
from jax import lax
from jax.experimental import pallas as pl
from jax.experimental.pallas import tpu as pltpu

F32 = jnp.float32
BF16 = jnp.bfloat16

NORM_EPS = 1e-5
N_META = 16
SSD_CHUNK = 256
SSD_SUB = 128
SSD_HEAD_DIM = 64
SSD_GROUPS = 2
SSD_STATE = 128
SSD_CONV = 4
S5_WIDTH = 16
S5_STATE = 64
S5_BLOCK = 16
S5_FLAT = S5_BLOCK * S5_WIDTH
S5_PREP_GROUPS = 8
S5_CORE_PAIRS = 2
PLANE = SSD_CHUNK // S5_BLOCK
LANES = 128
SUBLANES = 8
LANE_BLOCKS = LANES // S5_WIDTH
COL_TILE = 256
GLU_DOTS = 4
VMEM_LIMIT_BYTES = 58 * 1024 * 1024

assert PLANE == S5_BLOCK == 2 * SUBLANES
assert LANES == 2 * SSD_HEAD_DIM


def _rmsnorm(x, g):
    return x * lax.rsqrt(jnp.mean(x * x, axis=-1, keepdims=True) + NORM_EPS) * g


def _silu(x):
    return x * jax.nn.sigmoid(x)


def _gelu(x):
    return 0.5 * x * (1.0 + lax.erf(x * (2.0 ** -0.5)))


def _dot(a, b):
    return jnp.dot(a, b, preferred_element_type=F32)


def _dot_nt(a, b):
    return lax.dot_general(a, b, (((1,), (1,)), ((), ())), preferred_element_type=F32)


def _dot_nt_3x(a, b):
    a_hi, b_hi = a.astype(BF16), b.astype(BF16)
    a_lo = (a - a_hi.astype(F32)).astype(BF16)
    b_lo = (b - b_hi.astype(F32)).astype(BF16)
    return _dot_nt(a_hi, b_hi) + _dot_nt(a_hi, b_lo) + _dot_nt(a_lo, b_hi)


def _dot_tn(a, b):
    return lax.dot_general(a, b, (((0,), (0,)), ((), ())), preferred_element_type=F32)


def _dot_split(a, m01, parts):
    acc = None
    rem = a
    for _ in range(parts):
        piece = rem.astype(BF16)
        rem = rem - piece.astype(F32)
        term = _dot(piece, m01)
        acc = term if acc is None else acc + term
    return acc


def _params(sem):
    return pltpu.CompilerParams(dimension_semantics=sem, vmem_limit_bytes=VMEM_LIMIT_BYTES)


def _const_spec(shape, single_buffer=False):
    nd = len(shape)
    mode = pl.Buffered(1) if single_buffer else None
    return pl.BlockSpec(shape, lambda *_: (0,) * nd, pipeline_mode=mode)


def _token_of_row(r):
    return (r % PLANE) * S5_BLOCK + r // PLANE


def _transpose8(arrays, pos, axis, unit):
    a = list(arrays)
    size = 8 * unit
    d = 4
    while d:
        keep_lo = (pos & (d * unit)) == 0
        for i in range(8):
            if i & d:
                continue
            lo, hi = a[i], a[i + d]
            a[i] = jnp.where(keep_lo, lo, pltpu.roll(hi, d * unit, axis=axis))
            a[i + d] = jnp.where(keep_lo, pltpu.roll(lo, size - d * unit, axis=axis), hi)
        d //= 2
    return a


def _transpose_lane_blocks(planes, lane):
    return _transpose8(planes, lane, 1, S5_WIDTH)


def _swap_row_order(v):
    sub = lax.broadcasted_iota(jnp.int32, (SUBLANES, v.shape[1]), 0)
    chunks = []
    for chunk in range(v.shape[0] // SSD_CHUNK):
        out = [[None, None] for _ in range(S5_BLOCK)]
        for ah in range(2):
            for bh in range(2):
                first = [chunk * SSD_CHUNK + (SUBLANES * ah + al) * PLANE + SUBLANES * bh for al in range(SUBLANES)]
                src = [v[r:r + SUBLANES] for r in first]
                dst = _transpose8(src, sub, 0, 1)
                for bl in range(SUBLANES):
                    out[SUBLANES * bh + bl][ah] = dst[bl]
        chunks.append(jnp.concatenate([piece for pair in out for piece in pair], axis=0))
    return jnp.concatenate(chunks, axis=0) if len(chunks) > 1 else chunks[0]


def _in_proj_kernel(x_ref, hist_ref, g_ref, wz_ref, wxbc_ref, wdt_ref, wu_ref, convw_ref, convb_ref,
                    z_ref, xa_ref, bc_ref, dt_ref, f_ref, tail_ref, hist_scr, *, tiles_per_seq):
    tm = z_ref.shape[0]
    d_x = xa_ref.shape[1]
    n_hist = (SSD_CONV - 1) * PLANE
    n_chunks = tm // SSD_CHUNK

    @pl.when(lax.rem(pl.program_id(0), tiles_per_seq) == 0)
    def _():
        hist_scr[...] = hist_ref[...]

    n = _rmsnorm(_swap_row_order(x_ref[...]), g_ref[...]).astype(BF16)

    lane = lax.broadcasted_iota(jnp.int32, (PLANE, LANES), 1)
    for ct in range(wu_ref.shape[1] // COL_TILE):
        u = _dot(n, wu_ref[:, ct * COL_TILE:(ct + 1) * COL_TILE])
        for chunk in range(n_chunks):
            for o in range(COL_TILE // LANES):
                oct_ = ct * (COL_TILE // LANES) + o
                for half in range(S5_BLOCK // LANE_BLOCKS):
                    tok = []
                    for s in range(half * LANE_BLOCKS, (half + 1) * LANE_BLOCKS):
                        lo = chunk * SSD_CHUNK + s * PLANE
                        tok.append(u[lo:lo + PLANE, o * LANES:(o + 1) * LANES])
                    grp = _transpose_lane_blocks(tok, lane)
                    for gi in range(LANE_BLOCKS):
                        f_ref[oct_ * LANE_BLOCKS + gi, chunk * PLANE:(chunk + 1) * PLANE,
                              half * LANES:(half + 1) * LANES] = grp[gi].astype(BF16)

    row0 = lax.broadcasted_iota(jnp.int32, (PLANE, 1), 0) == 0
    for ct in range(wxbc_ref.shape[1] // COL_TILE):
        cols = slice(ct * COL_TILE, (ct + 1) * COL_TILE)
        pre = _dot(n, wxbc_ref[:, cols])
        w = [convw_ref[k:k + 1, cols] for k in range(SSD_CONV)]
        prev = hist_scr[:, cols]
        for chunk in range(n_chunks):
            cur = pre[chunk * SSD_CHUNK:(chunk + 1) * SSD_CHUNK]
            last = cur[SSD_CHUNK - n_hist:]
            wrapped = [jnp.where(row0, pltpu.roll(prev[m * PLANE:(m + 1) * PLANE], 1, axis=0),
                                 pltpu.roll(last[m * PLANE:(m + 1) * PLANE], 1, axis=0)) for m in range(SSD_CONV - 1)]
            acc = convb_ref[:, cols] + w[SSD_CONV - 1] * cur
            for k in range(1, SSD_CONV):
                shifted = jnp.concatenate(wrapped[SSD_CONV - 1 - k:] + [cur[:SSD_CHUNK - k * PLANE]], axis=0)
                acc = acc + w[SSD_CONV - 1 - k] * shifted
            act = _silu(acc)
            rows = slice(chunk * SSD_CHUNK, (chunk + 1) * SSD_CHUNK)
            if ct * COL_TILE < d_x:
                xa_ref[rows, cols] = act
            else:
                bc_ref[rows, ct * COL_TILE - d_x:(ct + 1) * COL_TILE - d_x] = act.astype(BF16)
            prev = last
        hist_scr[:, cols] = prev
        tail_ref[:, cols] = prev

    for ct in range(wz_ref.shape[1] // COL_TILE):
        cols = slice(ct * COL_TILE, (ct + 1) * COL_TILE)
        z_ref[:, cols] = _dot(n, wz_ref[:, cols])
    dt_ref[...] = _dot(n, wdt_ref[...])


def _in_proj(x_rows, hist, g, wz, wxbc, wdt, wu, convw, convb, tm, tiles_per_seq):
    rows, d = x_rows.shape
    d_z, d_xbc, d_u = wz.shape[1], wxbc.shape[1], wu.shape[1]
    d_bc = d_xbc - d_z
    n_grp = d_u // S5_WIDTH
    nb = tm // S5_BLOCK
    row_spec = lambda w: pl.BlockSpec((tm, w), lambda i: (i, 0))
    return pl.pallas_call(
        functools.partial(_in_proj_kernel, tiles_per_seq=tiles_per_seq),
        grid=(rows // tm,),
        in_specs=[row_spec(d)]
        + [_const_spec(a.shape, single_buffer=True) for a in (hist, g, wz, wxbc, wdt, wu, convw, convb)],
        out_specs=[row_spec(d_z), row_spec(d_z), row_spec(d_bc), row_spec(LANES),
                   pl.BlockSpec((n_grp, nb, S5_FLAT), lambda i: (0, i, 0)), _const_spec(hist.shape)],
        out_shape=[jax.ShapeDtypeStruct((rows, d_z), F32), jax.ShapeDtypeStruct((rows, d_z), F32),
                   jax.ShapeDtypeStruct((rows, d_bc), BF16), jax.ShapeDtypeStruct((rows, LANES), F32),
                   jax.ShapeDtypeStruct((n_grp, rows // S5_BLOCK, S5_FLAT), BF16),
                   jax.ShapeDtypeStruct(hist.shape, F32)],
        scratch_shapes=[pltpu.VMEM(hist.shape, F32)],
        compiler_params=_params(("arbitrary",)),
        name="in_proj",
    )(x_rows, hist, g, wz, wxbc, wdt, wu, convw, convb)


def _ssd_chunk(xa_ref, bc_ref, z_ref, dt_ref, rows, dtb, alog, dskip_ref, g_ref, expand_ref, st_ref, yd_ref,
               padded_rows):
    q = SSD_SUB
    d_ssd = xa_ref.shape[-1]
    gw = d_ssd // SSD_GROUPS
    sub_planes = q // S5_BLOCK

    def half_of(v, half):
        return jnp.concatenate([v[s * PLANE + half * sub_planes:s * PLANE + (half + 1) * sub_planes]
                                for s in range(S5_BLOCK)], axis=0)

    def half_ref(ref, half, cols):
        first = rows.start + half * sub_planes
        return jnp.concatenate([ref[first + s * PLANE:first + s * PLANE + sub_planes, cols]
                                for s in range(S5_BLOCK)], axis=0)

    def token_of_row(r):
        return (r % sub_planes) * S5_BLOCK + r // sub_planes

    bc = bc_ref[rows, :].astype(F32)
    t_r = token_of_row(lax.broadcasted_iota(jnp.int32, (q, q), 0))
    t_c = token_of_row(lax.broadcasted_iota(jnp.int32, (q, q), 1))
    causal = t_r >= t_c
    tri = jnp.where(t_c >= t_r, 1.0, 0.0).astype(BF16)
    first_head = lax.broadcasted_iota(jnp.int32, (q, LANES), 1) < SSD_HEAD_DIM
    n_half = SSD_CHUNK // q
    n_tile = d_ssd // COL_TILE

    def prepare(half):
        bc_h = half_of(bc, half).astype(BF16)
        dt = jax.nn.softplus(half_ref(dt_ref, half, slice(None)) + dtb)
        if padded_rows:
            t_row = half * q + token_of_row(lax.broadcasted_iota(jnp.int32, dt.shape, 0))
            dt = jnp.where(t_row < padded_rows, 0.0, dt)
        d_a = dt * (-jnp.exp(alog))
        a_cs_t = _dot_split(d_a.T, tri, 3)
        return bc_h[:, :SSD_GROUPS * SSD_STATE], bc_h[:, SSD_GROUPS * SSD_STATE:], dt, a_cs_t.T, a_cs_t

    def expand_tile(half, t, dt, a_cs):
        cols = slice(t * COL_TILE, (t + 1) * COL_TILE)
        expand = expand_ref[:, cols]
        dt_x = _dot_split(dt, expand, 2)
        a_x = _dot_split(a_cs, expand, 2)
        a_last = a_x[q - 1:q, :]
        x_in = half_ref(xa_ref, half, cols)
        xdt = x_in * dt_x
        xw_b = (xdt * jnp.exp(a_last - a_x)).astype(BF16)
        return x_in, xdt.astype(BF16), xw_b, jnp.exp(a_x), jnp.exp(a_last)

    halves = [prepare(half) for half in range(n_half)]
    steps = [(half, t) for half in range(n_half) for t in range(n_tile)]
    expanded = expand_tile(*steps[0], halves[0][2], halves[0][3])
    ssq = [jnp.zeros((q, 1), F32) for _ in range(n_half)]
    cb = None
    for i, (half, t) in enumerate(steps):
        b_in, c_in, _, a_cs, a_cs_t = halves[half]
        x_in, xdt_b, xw_b, ea_x, ea_last = expanded
        if i + 1 < len(steps):
            nh, nt = steps[i + 1]
            expanded = expand_tile(nh, nt, halves[nh][2], halves[nh][3])
        cols = slice(t * COL_TILE, (t + 1) * COL_TILE)
        g = (t * COL_TILE) // gw
        gcols = slice(t * COL_TILE - g * gw, (t + 1) * COL_TILE - g * gw)
        cg = c_in[:, g * SSD_STATE:(g + 1) * SSD_STATE]
        bg = b_in[:, g * SSD_STATE:(g + 1) * SSD_STATE]
        if (t * COL_TILE) % gw == 0:
            cb = _dot_nt(cg, bg)
        s_prev = st_ref[g, :, gcols]
        y_off = _dot(cg, s_prev.astype(BF16)) * ea_x
        pairs = []
        for pair in range(COL_TILE // LANES):
            lo = pair * LANES
            att = []
            for h in (2 * (t * (COL_TILE // LANES) + pair), 2 * (t * (COL_TILE // LANES) + pair) + 1):
                seg = a_cs[:, h:h + 1] - a_cs_t[h:h + 1, :]
                att.append((cb * jnp.exp(jnp.where(causal, seg, -jnp.inf))).astype(BF16))
            x_pair = xdt_b[:, lo:lo + LANES]
            zero = jnp.zeros_like(x_pair)
            rhs = jnp.concatenate([jnp.where(first_head, x_pair, zero), jnp.where(first_head, zero, x_pair)], axis=0)
            pairs.append(_dot(jnp.concatenate(att, axis=1), rhs) + y_off[:, lo:lo + LANES])
        y = jnp.concatenate(pairs, axis=1) + x_in * dskip_ref[:, cols]
        y = y * _silu(half_ref(z_ref, half, cols))
        for s in range(S5_BLOCK):
            yd_ref[s * PLANE + half * sub_planes:s * PLANE + (half + 1) * sub_planes, cols] = (
                y[s * sub_planes:(s + 1) * sub_planes])
        ssq[half] = ssq[half] + jnp.sum(y * y, axis=-1, keepdims=True)
        st_ref[g, :, gcols] = s_prev * ea_last + _dot_tn(bg, xw_b)

    ssq = jnp.concatenate([ssq[half][s * sub_planes:(s + 1) * sub_planes]
                           for s in range(S5_BLOCK) for half in range(n_half)], axis=0)
    scale = lax.rsqrt(ssq * (1.0 / d_ssd) + NORM_EPS)
    return (yd_ref[...] * scale * g_ref[...]).astype(BF16)


def _ssd_kernel(xa_ref, bc_ref, z_ref, dt_ref, s0_ref, dtb_ref, alog_ref, dskip_ref, g_ref, expand_ref,
                y_ref, s_out_ref, st_ref, yd_ref, *, padded_rows):
    @pl.when(pl.program_id(1) == 0)
    def _():
        st_ref[...] = s0_ref[...]

    for c in range(xa_ref.shape[1] // SSD_CHUNK):
        rows = slice(c * SSD_CHUNK, (c + 1) * SSD_CHUNK)
        y_ref[0, rows, :] = _ssd_chunk(xa_ref.at[0], bc_ref.at[0], z_ref.at[0], dt_ref.at[0], rows, dtb_ref[...],
                                       alog_ref[...], dskip_ref, g_ref, expand_ref, st_ref, yd_ref, padded_rows)
    s_out_ref[...] = st_ref[...]


def _ssd(xa, bc, z, dt, s0, dtb, alog, dskip, g, expand, padded_rows, chunks_per_step):
    bsz, seq, d_ssd = xa.shape
    rows = SSD_CHUNK * chunks_per_step
    chunk = lambda w: pl.BlockSpec((1, rows, w), lambda b, c: (b, c, 0))
    return pl.pallas_call(
        functools.partial(_ssd_kernel, padded_rows=padded_rows),
        grid=(bsz, seq // rows),
        in_specs=[chunk(d_ssd), chunk(bc.shape[-1]), chunk(d_ssd), chunk(LANES)]
        + [_const_spec(a.shape) for a in (s0, dtb, alog, dskip, g, expand)],
        out_specs=[chunk(d_ssd), _const_spec(s0.shape)],
        out_shape=[jax.ShapeDtypeStruct((bsz, seq, d_ssd), BF16), jax.ShapeDtypeStruct(s0.shape, F32)],
        scratch_shapes=[pltpu.VMEM(s0.shape, F32), pltpu.VMEM((SSD_CHUNK, d_ssd), F32)],
        compiler_params=_params(("arbitrary", "arbitrary")),
        name="ssd",
    )(xa, bc, z, dt, s0, dtb, alog, dskip, g, expand)


def _lam_pow(lr, li, step, k):
    kf = k.astype(F32)
    mag = jnp.exp(kf * (lr * step))
    ang = kf * (li * step)
    return mag * jnp.cos(ang), mag * jnp.sin(ang)


def _s5_prep_kernel(lr_ref, li_ref, ls_ref, bt_re_ref, bt_im_ref, c_re_ref, c_im_ref,
                    mt_ref, wc_ref, ec_ref, dp_re_ref, dp_im_ref):
    t, hw, p = S5_BLOCK, S5_WIDTH, S5_STATE
    lr, li = lr_ref[0], li_ref[0]
    step = jnp.exp(ls_ref[0])
    k_up = lax.broadcasted_iota(jnp.int32, (3 * SUBLANES, 1), 0)
    pw_re, pw_im = _lam_pow(lr, li, step, k_up)
    k_dn = (t - 1) - lax.broadcasted_iota(jnp.int32, (t, 1), 0)
    pr_re, pr_im = _lam_pow(lr, li, step, k_dn)
    k_blk = t * lax.broadcasted_iota(jnp.int32, (2 * SUBLANES, 1), 0)
    dp_re, dp_im = _lam_pow(lr, li, step, k_blk)

    ab_re, ab_im = pw_re[1:2, :], pw_im[1:2, :]
    den = lr * lr + li * li
    coef_re = ((ab_re - 1.0) * lr + ab_im * li) / den
    coef_im = (ab_im * lr - (ab_re - 1.0) * li) / den
    bt_re, bt_im = bt_re_ref[0], bt_im_ref[0]
    bb_re = coef_re * bt_re - coef_im * bt_im
    bb_im = coef_re * bt_im + coef_im * bt_re
    c_re, c_im = c_re_ref[0], c_im_ref[0]

    def rep(m):
        return jnp.broadcast_to(m[:, None, :], (t, hw, p)).reshape(t * hw, p)

    def tile(m):
        return jnp.concatenate([m] * t, axis=0)

    lane = lax.broadcasted_iota(jnp.int32, (hw, t * hw), 1)
    for a in range(S5_PREP_GROUPS):
        sl = slice(a * p, (a + 1) * p)
        ct_re, ct_im = tile(c_re[:, sl]), tile(c_im[:, sl])
        l0_re, l0_im = rep(pw_re[0:t, sl]), rep(pw_im[0:t, sl])
        l1_re, l1_im = rep(pw_re[1:t + 1, sl]), rep(pw_im[1:t + 1, sl])
        x0_re = l0_re * ct_re - l0_im * ct_im
        x0_im = l0_re * ct_im + l0_im * ct_re
        kt = _dot_nt_3x(bb_re[:, sl], x0_re) - _dot_nt_3x(bb_im[:, sl], x0_im)
        for s in range(t):
            blk = kt if s == 0 else jnp.where(lane >= s * hw, pltpu.roll(kt, s * hw, axis=1), 0.0)
            mt_ref[a, s * hw:(s + 1) * hw, :] = blk.astype(mt_ref.dtype)
        x1_re = l1_re * ct_re - l1_im * ct_im
        x1_im = l1_re * ct_im + l1_im * ct_re
        ec_ref[a] = jnp.concatenate([x1_re, -x1_im], axis=1).T.astype(ec_ref.dtype)
        lw_re, lw_im = rep(pr_re[:, sl]), rep(pr_im[:, sl])
        bt_re_a, bt_im_a = tile(bb_re[:, sl]), tile(bb_im[:, sl])
        wc_ref[a] = jnp.concatenate([lw_re * bt_re_a - lw_im * bt_im_a, lw_re * bt_im_a + lw_im * bt_re_a],
                                    axis=1).astype(wc_ref.dtype)
        dp_re_ref[a] = jnp.concatenate([dp_re[:, sl]] * 2, axis=1)
        dp_im_ref[a] = jnp.concatenate([dp_im[:, sl]] * 2, axis=1)


def _s5_prep(lr, li, ls, bt_re, bt_im, c_re, c_im):
    steps = lr.shape[0]
    gs, p, fl = S5_PREP_GROUPS, S5_STATE, S5_FLAT
    n_grp = steps * gs
    in_spec = lambda a: pl.BlockSpec((1,) + a.shape[1:], lambda i: (i, 0, 0))
    out_spec = lambda *s: pl.BlockSpec((gs,) + s, lambda i: (i, 0, 0))
    out = lambda *s, dt=BF16: jax.ShapeDtypeStruct((n_grp,) + s, dt)
    args = (lr, li, ls, bt_re, bt_im, c_re, c_im)
    return pl.pallas_call(
        _s5_prep_kernel,
        grid=(steps,),
        in_specs=[in_spec(a) for a in args],
        out_specs=[out_spec(fl, fl), out_spec(fl, 2 * p), out_spec(2 * p, fl), out_spec(2 * SUBLANES, 2 * p),
                   out_spec(2 * SUBLANES, 2 * p)],
        out_shape=[out(fl, fl), out(fl, 2 * p), out(2 * p, fl), out(2 * SUBLANES, 2 * p, dt=F32),
                   out(2 * SUBLANES, 2 * p, dt=F32)],
        compiler_params=_params(("parallel",)),
        name="s5_prep",
    )(*args)


def _s5_core_kernel(f_r_ref, f_m_ref, mt_ref, wc_ref, ec_ref, dp_re_ref, dp_im_ref, dskip_ref,
                    y_ref, wu_re_ref, wu_im_ref, s_re_ref, s_im_ref, *, n_batch):
    p = S5_STATE
    rows_m = f_m_ref.shape[1]
    rows_r = f_r_ref.shape[1] // n_batch
    per_batch = rows_m + rows_r
    n_tiles = per_batch // SUBLANES
    lane_lo = lax.broadcasted_iota(jnp.int32, (1, 2 * p), 1) < p
    sub = lax.broadcasted_iota(jnp.int32, (SUBLANES, 2 * p), 0)

    def pack(a, b):
        return jnp.where(lane_lo, a, pltpu.roll(b, p, axis=1)), jnp.where(lane_lo, pltpu.roll(a, p, axis=1), b)

    def cmul_add(a_re, a_im, d_re, d_im, x_re, x_im):
        return a_re + d_re * x_re - d_im * x_im, a_im + d_re * x_im + d_im * x_re

    def shift_down(x, k):
        return jnp.where(sub >= k, pltpu.roll(x, k, axis=0), 0.0)

    for pr in range(f_r_ref.shape[0] // 2):
        ga, gb = 2 * pr, 2 * pr + 1
        wu_re, wu_im, s_re, s_im = wu_re_ref.at[pr], wu_im_ref.at[pr], s_re_ref.at[pr], s_im_ref.at[pr]
        m_re, m_im = pack(*[_dot(f_m_ref[g], wc_ref[g]) for g in (ga, gb)])
        r_re, r_im = pack(*[_dot(f_r_ref[g], wc_ref[g]) for g in (ga, gb)])
        for b in range(n_batch):
            wu_re[b * per_batch:b * per_batch + rows_m, :] = m_re
            wu_im[b * per_batch:b * per_batch + rows_m, :] = m_im
            wu_re[b * per_batch + rows_m:(b + 1) * per_batch, :] = r_re[b * rows_r:(b + 1) * rows_r]
            wu_im[b * per_batch + rows_m:(b + 1) * per_batch, :] = r_im[b * rows_r:(b + 1) * rows_r]

        dp_re = jnp.where(lane_lo, dp_re_ref[ga], dp_re_ref[gb])
        dp_im = jnp.where(lane_lo, dp_im_ref[ga], dp_im_ref[gb])
        steps = {k: (jnp.where(sub >= k, dp_re[k:k + 1, :], 0.0), jnp.where(sub >= k, dp_im[k:k + 1, :], 0.0))
                 for k in (1, 2, 4)}

        def tile_step(i, carry):
            new = []
            for b in range(n_batch):
                c_re, c_im = carry[2 * b], carry[2 * b + 1]
                base = pl.multiple_of(b * per_batch + i * SUBLANES, SUBLANES)
                v_re = wu_re[pl.ds(base, SUBLANES), :]
                v_im = wu_im[pl.ds(base, SUBLANES), :]
                for k, (m_re, m_im) in steps.items():
                    v_re, v_im = cmul_add(v_re, v_im, m_re, m_im,
                                          pltpu.roll(v_re, k, axis=0), pltpu.roll(v_im, k, axis=0))
                e_re, e_im = cmul_add(shift_down(v_re, 1), shift_down(v_im, 1), dp_re[0:SUBLANES, :],
                                      dp_im[0:SUBLANES, :], c_re, c_im)
                s_re[pl.ds(base, SUBLANES), :] = e_re
                s_im[pl.ds(base, SUBLANES), :] = e_im
                n_re, n_im = cmul_add(v_re[SUBLANES - 1:, :], v_im[SUBLANES - 1:, :],
                                      dp_re[SUBLANES:SUBLANES + 1, :], dp_im[SUBLANES:SUBLANES + 1, :], c_re, c_im)
                new += [n_re, n_im]
            return tuple(new)

        zero = jnp.zeros((1, 2 * p), F32)
        lax.fori_loop(0, n_tiles, tile_step, (zero,) * (2 * n_batch), unroll=True)

        for b in range(n_batch):
            lo = b * per_batch + rows_m
            s_a, s_b = pack(s_re[lo:lo + rows_r, :], s_im[lo:lo + rows_r, :])
            for g, s_ab in ((ga, s_a), (gb, s_b)):
                f = f_r_ref[g, b * rows_r:(b + 1) * rows_r, :]
                y = _dot(f, mt_ref[g]) + _dot(s_ab.astype(BF16), ec_ref[g]) + f.astype(F32) * dskip_ref[g]
                y_ref[g, b * rows_r:(b + 1) * rows_r, :] = _gelu(y).astype(y_ref.dtype)


def _s5_core(f_r, f_m, mt, wc, ec, dp_re, dp_im, dskip, n_batch):
    n_grp, rows_r, fl = f_r.shape
    rows_m = f_m.shape[1]
    p2 = 2 * S5_STATE
    rows = n_batch * rows_m + rows_r
    gs = 2 * S5_CORE_PAIRS
    spec = lambda *s: pl.BlockSpec((gs,) + s, lambda i: (i, 0, 0))
    return pl.pallas_call(
        functools.partial(_s5_core_kernel, n_batch=n_batch),
        grid=(n_grp // gs,),
        in_specs=[spec(rows_r, fl), spec(rows_m, fl), spec(fl, fl), spec(fl, p2), spec(p2, fl),
                  spec(2 * SUBLANES, p2), spec(2 * SUBLANES, p2), spec(1, fl)],
        out_specs=spec(rows_r, fl),
        out_shape=jax.ShapeDtypeStruct((n_grp, rows_r, fl), BF16),
        scratch_shapes=[pltpu.VMEM((S5_CORE_PAIRS, rows, p2), F32)] * 4,
        compiler_params=_params(("parallel",)),
        name="s5_core",
    )(f_r, f_m, mt, wc, ec, dp_re, dp_im, dskip)


def _s5_glu_kernel(yg_ref, w_ref, b_ref, g_ref, o_ref):
    d = o_ref.shape[-1]
    tm = o_ref.shape[0]
    lane = lax.broadcasted_iota(jnp.int32, (PLANE, LANES), 1)
    chunks = []
    for chunk in range(tm // SSD_CHUNK):
        planes = [[None] * (d // LANES) for _ in range(S5_BLOCK)]
        for oct_ in range(d // LANES):
            for half in range(S5_BLOCK // LANE_BLOCKS):
                grp = [yg_ref[oct_ * LANE_BLOCKS + gi, chunk * PLANE:(chunk + 1) * PLANE,
                              half * LANES:(half + 1) * LANES].astype(F32) for gi in range(LANE_BLOCKS)]
                tok = _transpose_lane_blocks(grp, lane)
                for s8 in range(LANE_BLOCKS):
                    planes[half * LANE_BLOCKS + s8][oct_] = tok[s8].astype(BF16)
        chunks.append(jnp.concatenate([jnp.concatenate(pieces, axis=1) for pieces in planes], axis=0))
    per_dot = max(1, len(chunks) // GLU_DOTS)
    for first in range(0, len(chunks), per_dot):
        y = jnp.concatenate(chunks[first:first + per_dot], axis=0) if per_dot > 1 else chunks[first]
        v = _dot(y, w_ref[...]) + b_ref[...]
        o = v[:, :d] * jax.nn.sigmoid(v[:, d:])
        o_ref[first * SSD_CHUNK:(first + per_dot) * SSD_CHUNK, :] = _rmsnorm(o, g_ref[...]).astype(o_ref.dtype)


def _s5_glu(yg, w, b, g, tm):
    n_grp, blocks, fl = yg.shape
    rows, d = blocks * S5_BLOCK, n_grp * S5_WIDTH
    nb = tm // S5_BLOCK
    return pl.pallas_call(
        _s5_glu_kernel,
        grid=(rows // tm,),
        in_specs=[pl.BlockSpec((n_grp, nb, fl), lambda i: (0, i, 0))]
        + [_const_spec(a.shape, single_buffer=True) for a in (w, b, g)],
        out_specs=pl.BlockSpec((tm, d), lambda i: (i, 0)),
        out_shape=jax.ShapeDtypeStruct((rows, d), BF16),
        compiler_params=_params(("parallel",)),
        name="s5_glu",
    )(yg, w, b, g)


def _out_mlp_kernel(x_ref, ya_ref, yb_ref, woa_ref, wob_ref, gm_ref, wup_ref, wdn_ref, gf_ref, o_ref, *, ff_tile):
    h = x_ref[...] + _swap_row_order(_dot(ya_ref[...], woa_ref[...]) + _dot(yb_ref[...], wob_ref[...]))
    n = _rmsnorm(h, gm_ref[...]).astype(BF16)
    d_ff = wup_ref.shape[1]
    mlp = None
    for k in range(d_ff // ff_tile):
        m = _dot(n, wup_ref[:, k * ff_tile:(k + 1) * ff_tile])
        m = jnp.square(jnp.maximum(m, 0.0)).astype(BF16)
        part = _dot(m, wdn_ref[k * ff_tile:(k + 1) * ff_tile, :])
        mlp = part if mlp is None else mlp + part
    o_ref[...] = _rmsnorm(h + mlp, gf_ref[...])


def _out_mlp(x_rows, ya, yb, woa, wob, gm, wup, wdn, gf, tm, ff_tile):
    rows, d = ya.shape
    row_spec = pl.BlockSpec((tm, d), lambda i: (i, 0))
    return pl.pallas_call(
        functools.partial(_out_mlp_kernel, ff_tile=ff_tile),
        grid=(rows // tm,),
        in_specs=[row_spec, row_spec, row_spec]
        + [_const_spec(a.shape, single_buffer=True) for a in (woa, wob, gm, wup, wdn, gf)],
        out_specs=row_spec,
        out_shape=jax.ShapeDtypeStruct((rows, d), F32),
        compiler_params=_params(("parallel",)),
        name="out_mlp",
    )(x_rows, ya, yb, woa, wob, gm, wup, wdn, gf)


def _row_tile(rows, want):
    tm = min(rows, want)
    assert rows % tm == 0
    return tm


def kernel(x, meta_tokens, g_mix, w_in, conv_w, conv_b, dt_bias, a_log, d_ssd, g_ssd, lam_re, lam_im, log_step,
           b_re, b_im, c_re, c_im, d_s5, w_glu, b_glu, g_s5, w_out, g_mlp, w_up, w_down, g_final):
    bsz, seq, d_model = x.shape
    assert w_in.shape[0] == 1, "single-layer block: meta-token outputs are only consumed as state"
    assert seq % SSD_CHUNK == 0 and d_model % LANES == 0
    n_heads = dt_bias.shape[-1]
    d_ssd_w = n_heads * SSD_HEAD_DIM
    d_xbc = conv_w.shape[-1]
    n_grp = d_s5.shape[-2]
    o_dt = d_ssd_w + d_xbc
    o_u = o_dt + n_heads
    row = lambda v: v.reshape(1, -1).astype(F32)

    wi = w_in[0]
    wz = wi[:, :d_ssd_w].astype(BF16)
    wxbc = wi[:, d_ssd_w:o_dt].astype(BF16)
    wdt = jnp.pad(wi[:, o_dt:o_u], ((0, 0), (0, LANES - n_heads))).astype(BF16)
    wu = wi[:, o_u:].astype(BF16)
    pad_heads = lambda v: jnp.pad(row(v), ((0, 0), (0, LANES - n_heads)))
    convw, convb, gmix = conv_w[0].astype(F32), row(conv_b[0]), row(g_mix[0])

    x_rows = x.reshape(bsz * seq, d_model)
    meta_chunk = jnp.concatenate([jnp.zeros((SSD_CHUNK - N_META, d_model), x.dtype), meta_tokens.astype(x.dtype)], axis=0)
    tm = _row_tile(seq, 1024)
    no_hist = jnp.zeros(((SSD_CONV - 1) * PLANE, d_xbc), F32)
    z_m, xa_m, bc_m, dt_m, f_m, tail_m = _in_proj(meta_chunk, no_hist, gmix, wz, wxbc, wdt, wu,
                                                  convw, convb, SSD_CHUNK, 1)
    z_r, xa_r, bc_r, dt_r, f_r, _ = _in_proj(x_rows, tail_m, gmix, wz, wxbc, wdt, wu, convw, convb, tm, seq // tm)

    seq3 = lambda a, b: a.reshape(b, -1, a.shape[-1])
    expand = (jnp.arange(d_ssd_w)[None, :] // SSD_HEAD_DIM == jnp.arange(LANES)[:, None]).astype(BF16)
    ssd_args = (pad_heads(dt_bias[0]), pad_heads(a_log[0]), row(jnp.repeat(d_ssd[0], SSD_HEAD_DIM)), row(g_ssd[0]), expand)
    no_state = jnp.zeros((SSD_GROUPS, SSD_STATE, d_ssd_w // SSD_GROUPS), F32)
    _, s_meta = _ssd(seq3(xa_m, 1), seq3(bc_m, 1), seq3(z_m, 1), seq3(dt_m, 1), no_state, *ssd_args,
                     padded_rows=SSD_CHUNK - N_META, chunks_per_step=1)
    y_ssd, _ = _ssd(seq3(xa_r, bsz), seq3(bc_r, bsz), seq3(z_r, bsz), seq3(dt_r, bsz), s_meta, *ssd_args, padded_rows=0,
                    chunks_per_step=_row_tile(seq // SSD_CHUNK, 4))

    t, gs, p = S5_BLOCK, S5_PREP_GROUPS, S5_STATE
    lanes = lambda v: v.astype(F32).reshape(n_grp // gs, 1, gs * p)
    per_grp = lambda v: (v.astype(F32).reshape(n_grp // gs, gs, S5_WIDTH, p).transpose(0, 2, 1, 3)
                         .reshape(n_grp // gs, S5_WIDTH, gs * p))
    mt, wc, ec, dp_re, dp_im = _s5_prep(lanes(lam_re[0]), lanes(lam_im[0]), lanes(jnp.repeat(log_step[0], p)),
                                        per_grp(jnp.swapaxes(b_re[0], 1, 2)), per_grp(jnp.swapaxes(b_im[0], 1, 2)),
                                        per_grp(c_re[0]), per_grp(c_im[0]))
    dsk = jnp.tile(d_s5[0].astype(F32), (1, t)).reshape(n_grp, 1, S5_FLAT)
    yg = _s5_core(f_r, f_m, mt, wc, ec, dp_re, dp_im, dsk, bsz)
    y_s5 = _s5_glu(yg, w_glu[0].astype(BF16), row(b_glu[0]), row(g_s5[0]), tm)

    wo = w_out[0].astype(BF16)
    out = _out_mlp(x_rows, y_ssd.reshape(bsz * seq, d_ssd_w), y_s5, wo[:d_ssd_w], wo[d_ssd_w:], row(g_mlp[0]),
                   w_up[0].astype(BF16), w_down[0].astype(BF16), row(g_final), tm, 1024)
    return out.reshape(bsz, seq, d_model).astype(x.dtype)
```
